```python
import math
import jax
import jax.numpy as jnp
from jax import lax
import numpy as np

D_MODEL = 1024
BATCH = 8
SEQ = 2048
DEPTH = 2
DEC_BATCH = 32
DEC_SEQ = 4
PAST_LEN = 16384
PAGE_SIZE = 128

N_MIXERS = 2
N_FOX_LAYERS = (DEPTH + 1) // 2
N_MLA_LAYERS = DEPTH // 2
Q_BLOCK = 128
MEM_LEN = 256
X_HEADS = 4
X_HEAD_DIM = 64
X_WIDTH = X_HEADS * X_HEAD_DIM
FOX_HEADS = 12
FOX_HEAD_DIM = 64
FOX_WIDTH = FOX_HEADS * FOX_HEAD_DIM
FOX_IN = 3 * FOX_WIDTH + FOX_HEADS + X_WIDTH
MLA_HEADS = 12
MLA_NOPE = 64
MLA_ROPE = 32
MLA_V = 64
MLA_Q_LORA = 384
MLA_KV_LORA = 256
MLA_IN = MLA_Q_LORA + MLA_KV_LORA + MLA_ROPE + X_WIDTH
MIX_WIDTH = FOX_WIDTH + X_WIDTH
FFN_HIDDEN = 2816
ROPE_THETA = 10000.0
RMS_EPS = 1e-6
FORGET_BIAS_INIT = 2.0

kernel_name = 'fox_mla_macaron_memory_decoder_step'

F32 = jnp.float32


def _rmsnorm(x, g):
    xf = x.astype(F32)
    y = xf * lax.rsqrt(jnp.mean(xf * xf, axis=-1, keepdims=True) + RMS_EPS)
    return (y * g.astype(F32)).astype(x.dtype)


def _swiglu(x, w_up, w_down):
    gate, up = jnp.split(x @ w_up, 2, axis=-1)
    return (jax.nn.silu(gate) * up) @ w_down


def _ffn_half(x, g, w_up, w_down):
    return x + 0.5 * _swiglu(_rmsnorm(x, g), w_up, w_down)


def _rope(x, pos):
    half = x.shape[-1] // 2
    inv = ROPE_THETA ** (-jnp.arange(half, dtype=F32) / half)
    ang = pos.astype(F32)[:, None] * inv
    ang = ang.reshape(ang.shape[:1] + (1,) * (x.ndim - 3) + (half,))
    cos, sin = jnp.cos(ang), jnp.sin(ang)
    xf = x.astype(F32)
    x1, x2 = xf[..., :half], xf[..., half:]
    return jnp.concatenate([x1 * cos - x2 * sin, x2 * cos + x1 * sin], axis=-1).astype(x.dtype)


def _causal_blocks(scores_fn, v):
    b, s_len, h, d = v.shape
    kpos = jnp.arange(s_len)

    def one(blk):
        i0 = blk * Q_BLOCK
        s = scores_fn(i0)
        qpos = i0 + jnp.arange(Q_BLOCK)
        s = jnp.where(kpos[None, :] <= qpos[:, None], s, -jnp.inf)
        p = jax.nn.softmax(s, axis=-1)
        return jnp.einsum('bhis,bshd->bihd', p, v)

    out = lax.map(one, jnp.arange(s_len // Q_BLOCK))
    return out.transpose(1, 0, 2, 3, 4).reshape(b, s_len, h, d)


def _self_block_init(s, v, eq):
    t = s.shape[-1]
    mask = jnp.tril(jnp.ones((t, t), dtype=bool))
    s = jnp.where(mask, s, -jnp.inf)
    m = jnp.max(s, axis=-1)
    p = jnp.exp(s - m[..., None])
    return (m, jnp.sum(p, axis=-1), jnp.einsum(eq, p, v))


def _online_update(carry, s, v, eq):
    m, l, acc = carry
    m_new = jnp.maximum(m, jnp.max(s, axis=-1))
    alpha = jnp.exp(m - m_new)
    p = jnp.exp(s - m_new[..., None])
    return (m_new, l * alpha + jnp.sum(p, axis=-1), acc * alpha[..., None] + jnp.einsum(eq, p, v))


def _memory_kv(mem, g, w):
    b, m, _ = mem.shape
    kv = (_rmsnorm(mem, g) @ w).reshape(b, m, 2, X_HEADS, X_HEAD_DIM)
    return kv[:, :, 0], kv[:, :, 1]


def _cross_attend(q, mk, mv):
    s = jnp.einsum('bthd,bmhd->bhtm', q.astype(F32), mk.astype(F32)) * (X_HEAD_DIM ** -0.5)
    p = jax.nn.softmax(s, axis=-1)
    return jnp.einsum('bhtm,bmhd->bthd', p, mv.astype(F32))


def _merge(mix, cross, w_out, dt):
    b, t = mix.shape[:2]
    cat = jnp.concatenate([mix.reshape(b, t, -1), cross.reshape(b, t, -1)], axis=-1).astype(dt)
    return cat @ w_out


def _fox_project(h, w_in, b_f):
    b, t, _ = h.shape
    p = h @ w_in
    q = p[..., :FOX_WIDTH].reshape(b, t, FOX_HEADS, FOX_HEAD_DIM)
    k = p[..., FOX_WIDTH:2 * FOX_WIDTH].reshape(b, t, FOX_HEADS, FOX_HEAD_DIM)
    v = p[..., 2 * FOX_WIDTH:3 * FOX_WIDTH].reshape(b, t, FOX_HEADS, FOX_HEAD_DIM)
    f_logit = p[..., 3 * FOX_WIDTH:3 * FOX_WIDTH + FOX_HEADS] + b_f
    xq = p[..., 3 * FOX_WIDTH + FOX_HEADS:].reshape(b, t, X_HEADS, X_HEAD_DIM)
    logf = jax.nn.log_sigmoid(f_logit.astype(F32))
    return q, k, v, logf, xq


def _fox_prompt(q, k, v, logf):
    scale = FOX_HEAD_DIM ** -0.5
    ct = jnp.cumsum(logf, axis=1).transpose(0, 2, 1)
    qs = q.astype(F32) * scale
    kf = k.astype(F32)

    def scores(i0):
        qb = lax.dynamic_slice_in_dim(qs, i0, Q_BLOCK, axis=1)
        cb = lax.dynamic_slice_in_dim(ct, i0, Q_BLOCK, axis=2)
        return jnp.einsum('bihd,bshd->bhis', qb, kf) + cb[:, :, :, None] - ct[:, :, None, :]

    return _causal_blocks(scores, v.astype(F32))


def _fox_sample(q, k, v, logf, cache_k, cache_v, cache_lf, j, page_table):
    bd, t = q.shape[:2]
    n_pages = page_table.shape[1]
    scale = FOX_HEAD_DIM ** -0.5
    qs = q.astype(F32) * scale
    cn = jnp.cumsum(logf, axis=1).transpose(0, 2, 1)
    s_self = jnp.einsum('bihd,bjhd->bhij', qs, k.astype(F32)) + cn[:, :, :, None] - cn[:, :, None, :]
    carry = _self_block_init(s_self, v.astype(F32), 'bhij,bjhd->bhid')
    lf_past = cache_lf[j, page_table].astype(F32).reshape(bd, n_pages * PAGE_SIZE, FOX_HEADS)
    suf = lax.cumsum(lf_past, axis=1, reverse=True)
    suf = jnp.concatenate([suf[:, 1:], jnp.zeros((bd, 1, FOX_HEADS), F32)], axis=1)
    suf_pages = suf.reshape(bd, n_pages, PAGE_SIZE, FOX_HEADS).transpose(1, 0, 3, 2)

    def step(c, xs):
        pt, sp = xs
        kp = cache_k[j, pt].astype(F32)
        vp = cache_v[j, pt].astype(F32)
        s = jnp.einsum('bihd,bshd->bhis', qs, kp) + sp[:, :, None, :] + cn[:, :, :, None]
        return _online_update(c, s, vp, 'bhis,bshd->bhid'), None

    (m, l, acc), _ = lax.scan(step, carry, (page_table.T, suf_pages))
    return (acc / l[..., None]).transpose(0, 2, 1, 3)


def _mla_project(h, w_in, q_norm, w_q_b, kv_norm, pos):
    p = h @ w_in
    qa = p[..., :MLA_Q_LORA]
    kva = p[..., MLA_Q_LORA:MLA_Q_LORA + MLA_KV_LORA]
    kr = p[..., MLA_Q_LORA + MLA_KV_LORA:MLA_Q_LORA + MLA_KV_LORA + MLA_ROPE]
    b, t = h.shape[:2]
    xq = p[..., MLA_Q_LORA + MLA_KV_LORA + MLA_ROPE:].reshape(b, t, X_HEADS, X_HEAD_DIM)
    qfull = jnp.einsum('btr,rhe->bthe', _rmsnorm(qa, q_norm), w_q_b)
    qn = qfull[..., :MLA_NOPE]
    qr = _rope(qfull[..., MLA_NOPE:], pos)
    c = _rmsnorm(kva, kv_norm)
    kr = _rope(kr, pos)
    return qn, qr, c, kr, xq


def _mla_prompt(qn, qr, c, kr, w_kv_b):
    scale = (MLA_NOPE + MLA_ROPE) ** -0.5
    kn = jnp.einsum('bsr,rhd->bshd', c, w_kv_b[..., :MLA_NOPE]).astype(F32)
    v = jnp.einsum('bsr,rhd->bshd', c, w_kv_b[..., MLA_NOPE:]).astype(F32)
    qns = qn.astype(F32) * scale
    qrs = qr.astype(F32) * scale
    krf = kr.astype(F32)

    def scores(i0):
        a = lax.dynamic_slice_in_dim(qns, i0, Q_BLOCK, axis=1)
        r = lax.dynamic_slice_in_dim(qrs, i0, Q_BLOCK, axis=1)
        return jnp.einsum('bihd,bshd->bhis', a, kn) + jnp.einsum('bihe,bse->bhis', r, krf)

    return _causal_blocks(scores, v)


def _mla_sample(qn, qr, c, kr, w_kv_b, cache_c, cache_kr, j, page_table):
    scale = (MLA_NOPE + MLA_ROPE) ** -0.5
    w_uk = w_kv_b[..., :MLA_NOPE].astype(F32)
    w_uv = w_kv_b[..., MLA_NOPE:].astype(F32)
    q_lat = jnp.einsum('bihd,rhd->bihr', qn.astype(F32), w_uk) * scale
    qrs = qr.astype(F32) * scale

    def sc(cc, kk):
        return jnp.einsum('bihr,bsr->bhis', q_lat, cc) + jnp.einsum('bihe,bse->bhis', qrs, kk)

    cf = c.astype(F32)
    carry = _self_block_init(sc(cf, kr.astype(F32)), cf, 'bhij,bjr->bhir')

    def step(cr, pt):
        cp = cache_c[j, pt].astype(F32)
        kp = cache_kr[j, pt].astype(F32)
        return _online_update(cr, sc(cp, kp), cp, 'bhis,bsr->bhir'), None

    (m, l, acc), _ = lax.scan(step, carry, page_table.T)
    return jnp.einsum('bhir,rhd->bihd', acc / l[..., None], w_uv)


def setup_inputs(seed: int = 0) -> dict:
    key = jax.random.key(seed)
    ks = jax.random.split(key, 32)
    n_pages = PAST_LEN // PAGE_SIZE
    n_used = DEC_BATCH * n_pages
    n_pool = (5 * n_used + 3) // 4

    def nrm(k, shape, scale=1.0):
        return jax.random.normal(k, shape, F32) * scale

    page_table = jax.random.permutation(ks[0], n_pool)[:n_used].reshape(DEC_BATCH, n_pages).astype(jnp.int32)
    return {
        'x_prompt': nrm(ks[1], (BATCH, SEQ, D_MODEL)),
        'x_sample': nrm(ks[2], (DEC_BATCH, DEC_SEQ, D_MODEL)),
        'cache_fox_k': nrm(ks[3], (N_FOX_LAYERS, n_pool, PAGE_SIZE, FOX_HEADS, FOX_HEAD_DIM)),
        'cache_fox_v': nrm(ks[4], (N_FOX_LAYERS, n_pool, PAGE_SIZE, FOX_HEADS, FOX_HEAD_DIM)),
        'cache_fox_logf': jax.nn.log_sigmoid(FORGET_BIAS_INIT + nrm(ks[5], (N_FOX_LAYERS, n_pool, PAGE_SIZE, FOX_HEADS))),
        'cache_mla_ckv': nrm(ks[6], (N_MLA_LAYERS, n_pool, PAGE_SIZE, MLA_KV_LORA)),
        'cache_mla_krope': nrm(ks[7], (N_MLA_LAYERS, n_pool, PAGE_SIZE, MLA_ROPE)),
        'cache_mem_k': nrm(ks[8], (DEPTH, DEC_BATCH, MEM_LEN, X_HEADS, X_HEAD_DIM)),
        'cache_mem_v': nrm(ks[9], (DEPTH, DEC_BATCH, MEM_LEN, X_HEADS, X_HEAD_DIM)),
        'page_table': page_table,
        'mem_prompt': nrm(ks[10], (BATCH, MEM_LEN, D_MODEL)),
        'norm_gains': 1.0 + nrm(ks[11], (DEPTH, 3, D_MODEL), 0.01),
        'ffn_w_up': nrm(ks[12], (DEPTH, 2, D_MODEL, 2 * FFN_HIDDEN), D_MODEL ** -0.5),
        'ffn_w_down': nrm(ks[13], (DEPTH, 2, FFN_HIDDEN, D_MODEL), FFN_HIDDEN ** -0.5),
        'fox_w_in': nrm(ks[14], (N_FOX_LAYERS, D_MODEL, FOX_IN), D_MODEL ** -0.5),
        'fox_b_f': FORGET_BIAS_INIT + nrm(ks[15], (N_FOX_LAYERS, FOX_HEADS), 0.1),
        'mla_w_in': nrm(ks[16], (N_MLA_LAYERS, D_MODEL, MLA_IN), D_MODEL ** -0.5),
        'mla_q_norm': 1.0 + nrm(ks[17], (N_MLA_LAYERS, MLA_Q_LORA), 0.01),
        'mla_w_q_b': nrm(ks[18], (N_MLA_LAYERS, MLA_Q_LORA, MLA_HEADS, MLA_NOPE + MLA_ROPE), MLA_Q_LORA ** -0.5),
        'mla_kv_norm': 1.0 + nrm(ks[19], (N_MLA_LAYERS, MLA_KV_LORA), 0.01),
        'mla_w_kv_b': nrm(ks[20], (N_MLA_LAYERS, MLA_KV_LORA, MLA_HEADS, MLA_NOPE + MLA_V), MLA_KV_LORA ** -0.5),
        'mem_norm': 1.0 + nrm(ks[21], (DEPTH, D_MODEL), 0.01),
        'w_mem_kv': nrm(ks[22], (DEPTH, D_MODEL, 2 * X_WIDTH), D_MODEL ** -0.5),
        'w_out': nrm(ks[23], (DEPTH, MIX_WIDTH, D_MODEL), MIX_WIDTH ** -0.5),
        'final_norm': 1.0 + nrm(ks[24], (D_MODEL,), 0.01),
    }


def reference(x_prompt, x_sample, cache_fox_k, cache_fox_v, cache_fox_logf, cache_mla_ckv, cache_mla_krope,
              cache_mem_k, cache_mem_v, page_table, mem_prompt, norm_gains, ffn_w_up, ffn_w_down,
              fox_w_in, fox_b_f, mla_w_in, mla_q_norm, mla_w_q_b, mla_kv_norm, mla_w_kv_b,
              mem_norm, w_mem_kv, w_out, final_norm):
    dt = x_prompt.dtype
    past_len = page_table.shape[1] * PAGE_SIZE
    pos_p = jnp.arange(x_prompt.shape[1], dtype=jnp.int32)
    pos_s = past_len + jnp.arange(x_sample.shape[1], dtype=jnp.int32)
    xp, xs = x_prompt, x_sample
    fk_p, fv_p, fl_p, fk_s, fv_s, fl_s = [], [], [], [], [], []
    mc_p, mr_p, mc_s, mr_s = [], [], [], []
    mk_all, mv_all = [], []
    for layer in range(DEPTH):
        j = layer // N_MIXERS
        xp = _ffn_half(xp, norm_gains[layer, 0], ffn_w_up[layer, 0], ffn_w_down[layer, 0])
        xs = _ffn_half(xs, norm_gains[layer, 0], ffn_w_up[layer, 0], ffn_w_down[layer, 0])
        hp = _rmsnorm(xp, norm_gains[layer, 1])
        hs = _rmsnorm(xs, norm_gains[layer, 1])
        if layer % N_MIXERS == 0:
            q, k, v, lf, xq_p = _fox_project(hp, fox_w_in[j], fox_b_f[j])
            mix_p = _fox_prompt(q, k, v, lf)
            fk_p.append(k)
            fv_p.append(v)
            fl_p.append(lf)
            q, k, v, lf, xq_s = _fox_project(hs, fox_w_in[j], fox_b_f[j])
            mix_s = _fox_sample(q, k, v, lf, cache_fox_k, cache_fox_v, cache_fox_logf, j, page_table)
            fk_s.append(k)
            fv_s.append(v)
            fl_s.append(lf)
        else:
            qn, qr, c, kr, xq_p = _mla_project(hp, mla_w_in[j], mla_q_norm[j], mla_w_q_b[j], mla_kv_norm[j], pos_p)
            mix_p = _mla_prompt(qn, qr, c, kr, mla_w_kv_b[j])
            mc_p.append(c)
            mr_p.append(kr)
            qn, qr, c, kr, xq_s = _mla_project(hs, mla_w_in[j], mla_q_norm[j], mla_w_q_b[j], mla_kv_norm[j], pos_s)
            mix_s = _mla_sample(qn, qr, c, kr, mla_w_kv_b[j], cache_mla_ckv, cache_mla_krope, j, page_table)
            mc_s.append(c)
            mr_s.append(kr)
        mk, mv = _memory_kv(mem_prompt, mem_norm[layer], w_mem_kv[layer])
        mk_all.append(mk)
        mv_all.append(mv)
        cross_p = _cross_attend(xq_p, mk, mv)
        cross_s = _cross_attend(xq_s, cache_mem_k[layer], cache_mem_v[layer])
        xp = xp + _merge(mix_p, cross_p, w_out[layer], dt)
        xs = xs + _merge(mix_s, cross_s, w_out[layer], dt)
        xp = _ffn_half(xp, norm_gains[layer, 2], ffn_w_up[layer, 1], ffn_w_down[layer, 1])
        xs = _ffn_half(xs, norm_gains[layer, 2], ffn_w_up[layer, 1], ffn_w_down[layer, 1])
    y_prompt = _rmsnorm(xp, final_norm)
    y_sample = _rmsnorm(xs, final_norm)
    return (y_prompt, y_sample,
            jnp.stack(fk_p), jnp.stack(fv_p), jnp.stack(fl_p),
            jnp.stack(fk_s), jnp.stack(fv_s), jnp.stack(fl_s),
            jnp.stack(mc_p), jnp.stack(mr_p), jnp.stack(mc_s), jnp.stack(mr_s),
            jnp.stack(mk_all), jnp.stack(mv_all))
```

```python
import functools

import numpy as np
import jax
import jax.numpy as jnp
from jax import lax
from jax.experimental import pallas as pl
from jax.experimental.pallas import tpu as pltpu

D_MODEL = 1024
PAGE_SIZE = 128
X_HEADS = 4
X_HEAD_DIM = 64
X_WIDTH = X_HEADS * X_HEAD_DIM
FOX_HEADS = 12
FOX_HEAD_DIM = 64
FOX_WIDTH = FOX_HEADS * FOX_HEAD_DIM
MLA_HEADS = 12
MLA_NOPE = 64
MLA_ROPE = 32
MLA_V = 64
MLA_Q_LORA = 384
MLA_KV_LORA = 256
FFN_HIDDEN = 2816
ROPE_THETA = 10000.0
RMS_EPS = 1e-6

LANES = 128
HEAD_PAD = 16
PAIRS = FOX_HEADS // 2
VMEM_LIMIT = 56 * 1024 * 1024

F32 = jnp.float32
BF16 = jnp.bfloat16
NT_DIMS = (((1,), (1,)), ((), ()))
NEG_INF = float("-inf")


def _params(*sem):
    return pltpu.CompilerParams(dimension_semantics=sem, vmem_limit_bytes=VMEM_LIMIT)


def _rms(x, g):
    ms = jnp.mean(x * x, axis=-1, keepdims=True)
    return x * lax.rsqrt(ms + RMS_EPS) * g


def _dot(a, b):
    return jnp.dot(a, b, preferred_element_type=F32)


def _dot_nt(a, b):
    return lax.dot_general(a, b, NT_DIMS, preferred_element_type=F32)


def _dot_exact(a, b):
    return jnp.dot(a, b, preferred_element_type=F32, precision=lax.Precision.HIGHEST)


def _row_tile(t, pref):
    return pref if t % pref == 0 else t


def _ffn_kernel(x_ref, g_ref, wg_ref, wu_ref, wd_ref, o_ref, xn_ref, acc_ref):
    j = pl.program_id(1)

    @pl.when(j == 0)
    def _():
        xn_ref[...] = _rms(x_ref[...], g_ref[...]).astype(BF16)
        acc_ref[...] = jnp.zeros_like(acc_ref)

    xn = xn_ref[...]
    gate = _dot(xn, wg_ref[...])
    up = _dot(xn, wu_ref[...])
    h = (gate / (1.0 + jnp.exp(-gate)) * up).astype(BF16)
    acc_ref[...] += _dot(h, wd_ref[...])

    @pl.when(j == pl.num_programs(1) - 1)
    def _():
        o_ref[...] = x_ref[...] + 0.5 * acc_ref[...]


def _ffn_half(x, g, w_up, w_down, layer, idx):
    t = x.shape[0]
    tm = _row_tile(t, 512)
    th = FFN_HIDDEN // 2
    nj = FFN_HIDDEN // th
    return pl.pallas_call(
        _ffn_kernel,
        grid=(t // tm, nj),
        in_specs=[
            pl.BlockSpec((tm, D_MODEL), lambda m, j: (m, 0)),
            pl.BlockSpec((1, D_MODEL), lambda m, j: (0, 0)),
            pl.BlockSpec((None, None, D_MODEL, th), lambda m, j: (layer, idx, 0, j)),
            pl.BlockSpec((None, None, D_MODEL, th), lambda m, j: (layer, idx, 0, j + nj)),
            pl.BlockSpec((None, None, th, D_MODEL), lambda m, j: (layer, idx, j, 0)),
        ],
        out_specs=pl.BlockSpec((tm, D_MODEL), lambda m, j: (m, 0)),
        out_shape=jax.ShapeDtypeStruct((t, D_MODEL), F32),
        scratch_shapes=[pltpu.VMEM((tm, D_MODEL), BF16), pltpu.VMEM((tm, D_MODEL), F32)],
        compiler_params=_params("parallel", "arbitrary"),
        name="ffn_half",
    )(x, g, w_up, w_up, w_down)


def _fox_proj_kernel(x_ref, g_ref, wqkv_ref, wf_ref, bf_ref, wx_ref,
                     q_ref, k_ref, v_ref, lf_ref, xq_ref):
    hn = _rms(x_ref[...], g_ref[...]).astype(BF16)
    qkv = _dot(hn, wqkv_ref[...])
    q_ref[...] = (qkv[:, :FOX_WIDTH] * (FOX_HEAD_DIM ** -0.5)).astype(BF16)
    k_ref[...] = qkv[:, FOX_WIDTH:2 * FOX_WIDTH]
    v_ref[...] = qkv[:, 2 * FOX_WIDTH:]
    f = _dot(hn, wf_ref[...]) + bf_ref[...]
    lf_ref[...] = jnp.minimum(f, 0.0) - jnp.log(1.0 + jnp.exp(-jnp.abs(f)))
    xq_ref[...] = (_dot(hn, wx_ref[...]) * (X_HEAD_DIM ** -0.5)).astype(BF16)


def _fox_proj(x, g, wqkv, wf, bf, wx):
    t = x.shape[0]
    tm = _row_tile(t, 512)
    row = lambda n: pl.BlockSpec((tm, n), lambda m: (m, 0))
    full = lambda a: pl.BlockSpec(a.shape, lambda m: (0,) * a.ndim)
    return pl.pallas_call(
        _fox_proj_kernel,
        grid=(t // tm,),
        in_specs=[row(D_MODEL), full(g), full(wqkv), full(wf), full(bf), full(wx)],
        out_specs=[row(FOX_WIDTH), row(FOX_WIDTH), row(FOX_WIDTH), row(LANES), row(X_WIDTH)],
        out_shape=[
            jax.ShapeDtypeStruct((t, FOX_WIDTH), BF16),
            jax.ShapeDtypeStruct((t, FOX_WIDTH), F32),
            jax.ShapeDtypeStruct((t, FOX_WIDTH), F32),
            jax.ShapeDtypeStruct((t, LANES), F32),
            jax.ShapeDtypeStruct((t, X_WIDTH), BF16),
        ],
        compiler_params=_params("parallel"),
        name="fox_proj",
    )(x, g, wqkv, wf, bf, wx)


def _cumsum_kernel(lf_ref, tri_ref, c_ref, ct_ref, carry_ref):
    @pl.when(pl.program_id(1) == 0)
    def _():
        carry_ref[...] = jnp.zeros_like(carry_ref)

    c = _dot_exact(tri_ref[...], lf_ref[0]) + carry_ref[...]
    c_ref[0] = c
    ct_ref[0] = c.T[:HEAD_PAD, :]
    carry_ref[...] = c[-1:, :]


def _fox_cumsum(lf):
    b, s, _ = lf.shape
    ts = _row_tile(s, 256)
    tri = jnp.asarray(np.tril(np.ones((ts, ts), np.float32)))
    return pl.pallas_call(
        _cumsum_kernel,
        grid=(b, s // ts),
        in_specs=[
            pl.BlockSpec((1, ts, LANES), lambda i, j: (i, j, 0)),
            pl.BlockSpec((ts, ts), lambda i, j: (0, 0)),
        ],
        out_specs=[
            pl.BlockSpec((1, ts, LANES), lambda i, j: (i, j, 0)),
            pl.BlockSpec((1, HEAD_PAD, ts), lambda i, j: (i, 0, j)),
        ],
        out_shape=[
            jax.ShapeDtypeStruct((b, s, LANES), F32),
            jax.ShapeDtypeStruct((b, HEAD_PAD, s), F32),
        ],
        scratch_shapes=[pltpu.VMEM((1, LANES), F32)],
        compiler_params=_params("parallel", "arbitrary"),
        name="fox_cumsum",
    )(lf, tri)


def _flash_kernel(*refs, tq, tk, n_q, n_k, use_c):
    q_refs = refs[:n_q]
    k_refs = refs[n_q:n_q + n_k]
    v_ref = refs[n_q + n_k]
    pos = n_q + n_k + 1
    if use_c:
        c_ref, ct_ref = refs[pos], refs[pos + 1]
        pos += 2
    o_ref, kb_ref, vb_ref = refs[pos], refs[pos + 1], refs[pos + 2]
    pair = pl.program_id(1)
    qi = pl.program_id(2)

    @pl.when(qi == 0)
    def _():
        for n, k_ref in enumerate(k_refs):
            kb_ref[:, n * LANES:(n + 1) * LANES] = k_ref[...].astype(BF16)
        vb_ref[...] = v_ref[...].astype(BF16)

    q = jnp.concatenate([r[...] for r in q_refs], axis=1) if n_q > 1 else q_refs[0][...]
    kd = q.shape[1]
    lane = lax.broadcasted_iota(jnp.int32, (1, kd), 1)
    if n_q == 1:
        own0 = lane < FOX_HEAD_DIM
        own1 = lane >= FOX_HEAD_DIM
    else:
        own0 = (lane < MLA_NOPE) | ((lane >= LANES) & (lane < LANES + MLA_ROPE))
        own1 = ((lane >= MLA_NOPE) & (lane < LANES)) | (
            (lane >= LANES + MLA_ROPE) & (lane < LANES + 2 * MLA_ROPE))
    zero = jnp.zeros_like(q)
    qh = (jnp.where(own0, q, zero), jnp.where(own1, q, zero))

    if use_c:
        c = c_ref[0]
        cl = lax.broadcasted_iota(jnp.int32, c.shape, 1)
        cq = tuple(jnp.sum(jnp.where(cl == 2 * pair + h, c, 0.0), axis=1, keepdims=True)
                   for h in range(2))

    qpos = qi * tq + lax.broadcasted_iota(jnp.int32, (tq, tk), 0)
    kidx = lax.broadcasted_iota(jnp.int32, (tq, tk), 1)

    def body(kj, carry):
        k0 = pl.multiple_of(kj * tk, tk)
        kblk = kb_ref[pl.ds(k0, tk), :]
        vblk = vb_ref[pl.ds(k0, tk), :]
        keep = (k0 + kidx) <= qpos
        if use_c:
            ck = ct_ref[0, 0, :, pl.ds(k0, tk)]
        out = []
        for h in range(2):
            m, l, acc = carry[h]
            s = _dot_nt(qh[h], kblk)
            if use_c:
                s = s + (cq[h] - ck[h:h + 1, :])
            s = jnp.where(keep, s, NEG_INF)
            m_new = jnp.maximum(m, jnp.max(s, axis=1, keepdims=True))
            alpha = jnp.exp(m - m_new)
            p = jnp.exp(s - m_new)
            l = l * alpha + jnp.sum(p, axis=1, keepdims=True)
            acc = acc * alpha + _dot(p.astype(BF16), vblk)
            out.append((m_new, l, acc))
        return tuple(out)

    init = tuple((jnp.full((tq, 1), NEG_INF, F32), jnp.zeros((tq, 1), F32),
                  jnp.zeros((tq, LANES), F32)) for _ in range(2))
    n_blocks = (qi * tq + tq + tk - 1) // tk
    (_, l0, a0), (_, l1, a1) = lax.fori_loop(0, n_blocks, body, init)
    olane = lax.broadcasted_iota(jnp.int32, (1, LANES), 1)
    o_ref[...] = jnp.where(olane < FOX_HEAD_DIM, a0 / l0, a1 / l1).astype(o_ref.dtype)


def _flash(qs, ks, v, b, s, c=None, ct=None):
    tq = _row_tile(s, 256)
    tk = tq
    nq = s // tq
    use_c = c is not None
    kd = LANES * len(ks)
    qspec = pl.BlockSpec((tq, LANES), lambda i, p, j: (i * nq + j, p))
    kspec = pl.BlockSpec((s, LANES), lambda i, p, j: (i, p))
    in_specs = [qspec] * len(qs) + [kspec] * len(ks) + [kspec]
    args = list(qs) + list(ks) + [v]
    if use_c:
        in_specs += [
            pl.BlockSpec((1, tq, LANES), lambda i, p, j: (i, j, 0)),
            pl.BlockSpec((1, 1, 2, s), lambda i, p, j: (i, p, 0, 0)),
        ]
        args += [c, ct]
    kern = functools.partial(_flash_kernel, tq=tq, tk=tk, n_q=len(qs), n_k=len(ks), use_c=use_c)
    return pl.pallas_call(
        kern,
        grid=(b, PAIRS, nq),
        in_specs=in_specs,
        out_specs=qspec,
        out_shape=jax.ShapeDtypeStruct((b * s, PAIRS * LANES), BF16),
        scratch_shapes=[pltpu.VMEM((s, kd), BF16), pltpu.VMEM((s, LANES), BF16)],
        compiler_params=_params("parallel", "parallel", "arbitrary"),
        name="flash_mla" if len(qs) > 1 else "flash_fox",
    )(*args)


def _cross_kernel(q_ref, k_ref, v_ref, o_ref):
    q = q_ref[0].astype(BF16)
    kb = k_ref[0].astype(BF16)
    vb = v_ref[0].astype(BF16)
    lane = lax.broadcasted_iota(jnp.int32, (1, LANES), 1)
    first = lane < X_HEAD_DIM
    zero = jnp.zeros_like(q)
    outs = []
    for qh in (jnp.where(first, q, zero), jnp.where(first, zero, q)):
        s = _dot_nt(qh, kb)
        m = jnp.max(s, axis=1, keepdims=True)
        p = jnp.exp(s - m)
        l = jnp.sum(p, axis=1, keepdims=True)
        outs.append(_dot(p.astype(BF16), vb) / l)
    o_ref[0] = jnp.where(first, outs[0], outs[1]).astype(o_ref.dtype)


def _cross(xq, mk, mv):
    b, t, _ = xq.shape
    m = mk.shape[1]
    tq = _row_tile(t, 512)
    qspec = pl.BlockSpec((1, tq, LANES), lambda i, p, j: (i, j, p))
    kspec = pl.BlockSpec((1, m, LANES), lambda i, p, j: (i, 0, p))
    return pl.pallas_call(
        _cross_kernel,
        grid=(b, X_WIDTH // LANES, t // tq),
        in_specs=[qspec, kspec, kspec],
        out_specs=qspec,
        out_shape=jax.ShapeDtypeStruct((b, t, X_WIDTH), BF16),
        compiler_params=_params("parallel", "parallel", "arbitrary"),
        name="cross_attn",
    )(xq, mk, mv)


def _merge_kernel(x_ref, mix_ref, cross_ref, w_ref, o_ref):
    nm = mix_ref.shape[1]
    o_ref[...] = (x_ref[...] + _dot(mix_ref[...], w_ref[:nm, :])
                  + _dot(cross_ref[...], w_ref[nm:, :]))


def _merge(x, mix, cross, w):
    t = x.shape[0]
    tm = _row_tile(t, 512)
    row = lambda n: pl.BlockSpec((tm, n), lambda m: (m, 0))
    return pl.pallas_call(
        _merge_kernel,
        grid=(t // tm,),
        in_specs=[row(D_MODEL), row(mix.shape[1]), row(cross.shape[1]),
                  pl.BlockSpec(w.shape, lambda m: (0, 0))],
        out_specs=row(D_MODEL),
        out_shape=jax.ShapeDtypeStruct((t, D_MODEL), F32),
        compiler_params=_params("parallel"),
        name="out_proj",
    )(x, mix, cross, w)


def _norm_proj_kernel(x_ref, g_ref, w_ref, o_ref):
    o_ref[...] = _dot(_rms(x_ref[...], g_ref[...]).astype(BF16), w_ref[...])


def _norm_proj(x, g, w):
    t = x.shape[0]
    n = w.shape[1]
    tm = _row_tile(t, 512)
    return pl.pallas_call(
        _norm_proj_kernel,
        grid=(t // tm,),
        in_specs=[pl.BlockSpec((tm, D_MODEL), lambda m: (m, 0)),
                  pl.BlockSpec((1, D_MODEL), lambda m: (0, 0)),
                  pl.BlockSpec(w.shape, lambda m: (0, 0))],
        out_specs=pl.BlockSpec((tm, n), lambda m: (m, 0)),
        out_shape=jax.ShapeDtypeStruct((t, n), F32),
        compiler_params=_params("parallel"),
        name="norm_proj",
    )(x, g, w)


def _final_norm_kernel(x_ref, g_ref, o_ref):
    o_ref[...] = _rms(x_ref[...], g_ref[...])


def _final_norm(x, g):
    t = x.shape[0]
    tm = _row_tile(t, 512)
    return pl.pallas_call(
        _final_norm_kernel,
        grid=(t // tm,),
        in_specs=[pl.BlockSpec((tm, D_MODEL), lambda m: (m, 0)),
                  pl.BlockSpec((1, D_MODEL), lambda m: (0, 0))],
        out_specs=pl.BlockSpec((tm, D_MODEL), lambda m: (m, 0)),
        out_shape=jax.ShapeDtypeStruct((t, D_MODEL), F32),
        compiler_params=_params("parallel"),
        name="final_norm",
    )(x, g)


def _mla_proj_kernel(x_ref, g_ref, wqa_ref, wkva_ref, wkr_ref, wkrr_ref, wx_ref,
                     qg_ref, wqn_ref, wqr_ref, wqrr_ref, kvg_ref, wuk_ref, wuv_ref,
                     cos_ref, sin_ref,
                     qn_ref, qr_ref, c_ref, kr_ref, xq_ref, kn_ref, v_ref):
    scale = (MLA_NOPE + MLA_ROPE) ** -0.5
    hn = _rms(x_ref[...], g_ref[...]).astype(BF16)
    cos = cos_ref[...]
    sin = sin_ref[...]
    qa = _rms(_dot(hn, wqa_ref[...]), qg_ref[...]).astype(BF16)
    qn_ref[...] = (_dot(qa, wqn_ref[...]) * scale).astype(BF16)
    qr = _dot(qa, wqr_ref[...])
    qrr = _dot(qa, wqrr_ref[...])
    for p in range(PAIRS):
        sl = slice(p * LANES, (p + 1) * LANES)
        qr_ref[:, sl] = ((qr[:, sl] * cos + qrr[:, sl] * sin) * scale).astype(BF16)
    c = _rms(_dot(hn, wkva_ref[...]), kvg_ref[...])
    c_ref[...] = c
    cb = c.astype(BF16)
    kn_ref[...] = _dot(cb, wuk_ref[...]).astype(BF16)
    v_ref[...] = _dot(cb, wuv_ref[...]).astype(BF16)
    kr = _dot(hn, wkr_ref[...]) * cos + _dot(hn, wkrr_ref[...]) * sin
    kr_ref[...] = kr[:, :MLA_ROPE]
    xq_ref[...] = (_dot(hn, wx_ref[...]) * (X_HEAD_DIM ** -0.5)).astype(BF16)


def _mla_proj(x, g, w, cos, sin):
    t = x.shape[0]
    tm = _row_tile(t, 512)
    ntab = cos.shape[0] // tm
    row = lambda n: pl.BlockSpec((tm, n), lambda m: (m, 0))
    full = lambda a: pl.BlockSpec(a.shape, lambda m: (0,) * a.ndim)
    tab = pl.BlockSpec((tm, LANES), lambda m: (m % ntab, 0))
    wide = PAIRS * LANES
    return pl.pallas_call(
        _mla_proj_kernel,
        grid=(t // tm,),
        in_specs=[row(D_MODEL), full(g)] + [full(a) for a in w] + [tab, tab],
        out_specs=[row(wide), row(wide), row(MLA_KV_LORA), row(MLA_ROPE), row(X_WIDTH),
                   row(wide), row(wide)],
        out_shape=[
            jax.ShapeDtypeStruct((t, wide), BF16),
            jax.ShapeDtypeStruct((t, wide), BF16),
            jax.ShapeDtypeStruct((t, MLA_KV_LORA), F32),
            jax.ShapeDtypeStruct((t, MLA_ROPE), F32),
            jax.ShapeDtypeStruct((t, X_WIDTH), BF16),
            jax.ShapeDtypeStruct((t, wide), BF16),
            jax.ShapeDtypeStruct((t, wide), BF16),
        ],
        compiler_params=_params("parallel"),
        name="mla_proj",
    )(x, g, *w, cos, sin)


def _online_step(state, s_parts, v_parts, v_is_transposed):
    m, l, acc = state
    s = jnp.concatenate(s_parts, axis=1) if len(s_parts) > 1 else s_parts[0]
    m_new = jnp.maximum(m, jnp.max(s, axis=1, keepdims=True))
    alpha = jnp.exp(m - m_new)
    p = jnp.exp(s - m_new)
    l = l * alpha + jnp.sum(p, axis=1, keepdims=True)
    acc = acc * alpha
    for r, vb in enumerate(v_parts):
        pb = p[:, r * LANES:(r + 1) * LANES].astype(BF16)
        acc = acc + (_dot_nt(pb, vb) if v_is_transposed else _dot(pb, vb))
    return m_new, l, acc


def _head_diag(acc, t_new, width):
    n = acc.shape[1]
    rowh = lax.broadcasted_iota(jnp.int32, (HEAD_PAD, n), 0)
    laneh = lax.broadcasted_iota(jnp.int32, (HEAD_PAD, n), 1) // width
    orow = lax.broadcasted_iota(jnp.int32, (t_new, n), 0)
    out = jnp.zeros((t_new, n), F32)
    for i in range(t_new):
        blk = acc[i * HEAD_PAD:(i + 1) * HEAD_PAD, :]
        r = jnp.sum(jnp.where(rowh == laneh, blk, 0.0), axis=0, keepdims=True)
        out = jnp.where(orow == i, r, out)
    return out


def _fox_dec_kernel(pt_ref, qb_ref, kn_ref, vn_ref, lft_ref, sufm_ref, *rest, pb, t_new):
    kt_refs = rest[:pb]
    vt_refs = rest[pb:2 * pb]
    lf_refs = rest[2 * pb:3 * pb]
    o_ref, m_ref, l_ref, acc_ref, run_ref, pad_ref = rest[3 * pb:]
    step = pl.program_id(1)
    rows = t_new * HEAD_PAD
    qb = qb_ref[0]
    lft = lft_ref[0]
    lane = lax.broadcasted_iota(jnp.int32, (HEAD_PAD, LANES), 1)
    cn_cols = []
    run = jnp.zeros((HEAD_PAD, 1), F32)
    for i in range(t_new):
        run = run + lft[:, i:i + 1]
        cn_cols.append(run)
    cn_col = jnp.concatenate(cn_cols, axis=0)

    @pl.when(step == 0)
    def _():
        cn_lanes = jnp.zeros((HEAD_PAD, LANES), F32)
        for i in range(t_new):
            cn_lanes = jnp.where(lane == i, cn_cols[i], cn_lanes)
        pad_ref[...] = jnp.zeros_like(pad_ref)
        pad_ref[0:t_new, :] = kn_ref[0]
        kpad = pad_ref[...].astype(BF16)
        pad_ref[0:t_new, :] = vn_ref[0]
        vpad = pad_ref[...].astype(BF16)
        s = _dot_nt(qb, kpad)
        parts = []
        for i in range(t_new):
            blk = s[i * HEAD_PAD:(i + 1) * HEAD_PAD, :] + (cn_cols[i] - cn_lanes)
            parts.append(jnp.where(lane <= i, blk, NEG_INF))
        s = jnp.concatenate(parts, axis=0)
        m = jnp.max(s, axis=1, keepdims=True)
        p = jnp.exp(s - m)
        m_ref[...] = m
        l_ref[...] = jnp.sum(p, axis=1, keepdims=True)
        acc_ref[...] = _dot(p.astype(BF16), vpad)
        run_ref[...] = jnp.zeros_like(run_ref)

    run = run_ref[...]
    s_parts, v_parts = [], []
    for r in range(pb):
        lf = lf_refs[r][0]
        suf = _dot_exact(lf, sufm_ref[...]) + run
        run = run + jnp.sum(lf, axis=1, keepdims=True)
        ktb = kt_refs[r][0, 0].reshape(FOX_WIDTH, PAGE_SIZE).astype(BF16)
        s = _dot(qb, ktb)
        bias = jnp.concatenate([suf] * t_new, axis=0) + cn_col
        s_parts.append(s + bias)
        v_parts.append(vt_refs[r][0, 0].reshape(FOX_WIDTH, PAGE_SIZE).astype(BF16))
    run_ref[...] = run
    m, l, acc = _online_step((m_ref[...], l_ref[...], acc_ref[...]), s_parts, v_parts, True)
    m_ref[...] = m
    l_ref[...] = l
    acc_ref[...] = acc

    @pl.when(step == pl.num_programs(1) - 1)
    def _():
        o_ref[0] = _head_diag(acc / l, t_new, FOX_HEAD_DIM)


def _fox_decode(page_table, qblk, k_new, v_new, lft_new, cache_kt, cache_vt, cache_lf):
    bd, n_pages = page_table.shape
    t_new = k_new.shape[1]
    rows = t_new * HEAD_PAD
    pb = 8 if n_pages % 8 == 0 else 1
    sufm = jnp.asarray(np.tril(np.ones((PAGE_SIZE, PAGE_SIZE), np.float32), -1))

    def page(r):
        return lambda b, s, pt: pt[b, n_pages - 1 - (s * pb + r)]

    kv_specs = [pl.BlockSpec((1, 1, FOX_HEADS, FOX_HEAD_DIM, PAGE_SIZE),
                             (lambda b, s, pt, f=page(r): (0, f(b, s, pt), 0, 0, 0)))
                for r in range(pb)]
    lf_specs = [pl.BlockSpec((1, HEAD_PAD, PAGE_SIZE),
                             (lambda b, s, pt, f=page(r): (f(b, s, pt), 0, 0)))
                for r in range(pb)]
    per_b = lambda shape: pl.BlockSpec((1,) + shape, lambda b, s, pt: (b, 0, 0))
    grid_spec = pltpu.PrefetchScalarGridSpec(
        num_scalar_prefetch=1,
        grid=(bd, n_pages // pb),
        in_specs=[per_b((rows, FOX_WIDTH)), per_b((t_new, FOX_WIDTH)), per_b((t_new, FOX_WIDTH)),
                  per_b((HEAD_PAD, LANES)),
                  pl.BlockSpec((PAGE_SIZE, PAGE_SIZE), lambda b, s, pt: (0, 0))]
        + kv_specs + kv_specs + lf_specs,
        out_specs=per_b((t_new, FOX_WIDTH)),
        scratch_shapes=[pltpu.VMEM((rows, 1), F32), pltpu.VMEM((rows, 1), F32),
                        pltpu.VMEM((rows, FOX_WIDTH), F32), pltpu.VMEM((HEAD_PAD, 1), F32),
                        pltpu.VMEM((PAGE_SIZE, FOX_WIDTH), F32)],
    )
    return pl.pallas_call(
        functools.partial(_fox_dec_kernel, pb=pb, t_new=t_new),
        grid_spec=grid_spec,
        out_shape=jax.ShapeDtypeStruct((bd, t_new, FOX_WIDTH), F32),
        compiler_params=_params("parallel", "arbitrary"),
        name="fox_decode",
    )(page_table, qblk, k_new, v_new, lft_new, sufm,
      *([cache_kt] * pb), *([cache_vt] * pb), *([cache_lf] * pb))


def _mla_dec_kernel(pt_ref, qn_ref, qr_ref, cn_ref, krn_ref, wuk_ref, wuv_ref, *rest, pb, t_new):
    c_refs = rest[:pb]
    kr_refs = rest[pb:2 * pb]
    o_ref, m_ref, l_ref, acc_ref, ql_ref, cpad_ref, krt_ref = rest[2 * pb:]
    step = pl.program_id(1)
    lane = lax.broadcasted_iota(jnp.int32, (HEAD_PAD, LANES), 1)
    qr = qr_ref[0]

    @pl.when(step == 0)
    def _():
        ql = _dot(qn_ref[0], wuk_ref[...]).astype(BF16)
        ql_ref[...] = ql
        cpad_ref[...] = jnp.zeros_like(cpad_ref)
        cpad_ref[0:t_new, :] = cn_ref[0]
        cpad = cpad_ref[...].astype(BF16)
        s = _dot_nt(ql, cpad) + _dot_nt(qr, krn_ref[0])
        parts = [jnp.where(lane <= i, s[i * HEAD_PAD:(i + 1) * HEAD_PAD, :], NEG_INF)
                 for i in range(t_new)]
        s = jnp.concatenate(parts, axis=0)
        m = jnp.max(s, axis=1, keepdims=True)
        p = jnp.exp(s - m)
        m_ref[...] = m
        l_ref[...] = jnp.sum(p, axis=1, keepdims=True)
        acc_ref[...] = _dot(p.astype(BF16), cpad)
        krt_ref[...] = jnp.zeros_like(krt_ref)

    ql = ql_ref[...]
    s_parts, v_parts = [], []
    for r in range(pb):
        cb = c_refs[r][0, 0].astype(BF16)
        krt_ref[0:MLA_ROPE, :] = kr_refs[r][0, 0].astype(BF16)
        s_parts.append(_dot_nt(ql, cb) + _dot(qr, krt_ref[...]))
        v_parts.append(cb)
    m, l, acc = _online_step((m_ref[...], l_ref[...], acc_ref[...]), s_parts, v_parts, False)
    m_ref[...] = m
    l_ref[...] = l
    acc_ref[...] = acc

    @pl.when(step == pl.num_programs(1) - 1)
    def _():
        full = _dot((acc / l).astype(BF16), wuv_ref[...])
        o_ref[0] = _head_diag(full, t_new, MLA_V)


def _mla_decode(page_table, qnblk, qrblk, c_new, kr_new, wuk_t, wuv, cache_c, cache_krt):
    bd, n_pages = page_table.shape
    t_new = c_new.shape[1]
    rows = t_new * HEAD_PAD
    pb = 8 if n_pages % 8 == 0 else 1

    def page(r):
        return lambda b, s, pt: pt[b, s * pb + r]

    c_specs = [pl.BlockSpec((1, 1, PAGE_SIZE, MLA_KV_LORA),
                            (lambda b, s, pt, f=page(r): (0, f(b, s, pt), 0, 0)))
               for r in range(pb)]
    kr_specs = [pl.BlockSpec((1, 1, MLA_ROPE, PAGE_SIZE),
                             (lambda b, s, pt, f=page(r): (0, f(b, s, pt), 0, 0)))
                for r in range(pb)]
    per_b = lambda shape: pl.BlockSpec((1,) + shape, lambda b, s, pt: (b, 0, 0))
    full = lambda a: pl.BlockSpec(a.shape, lambda b, s, pt: (0,) * a.ndim)
    wide = MLA_HEADS * MLA_V
    grid_spec = pltpu.PrefetchScalarGridSpec(
        num_scalar_prefetch=1,
        grid=(bd, n_pages // pb),
        in_specs=[per_b((rows, MLA_HEADS * MLA_NOPE)), per_b((rows, LANES)),
                  per_b((t_new, MLA_KV_LORA)), per_b((PAGE_SIZE, LANES)),
                  full(wuk_t), full(wuv)] + c_specs + kr_specs,
        out_specs=per_b((t_new, wide)),
        scratch_shapes=[pltpu.VMEM((rows, 1), F32), pltpu.VMEM((rows, 1), F32),
                        pltpu.VMEM((rows, MLA_KV_LORA), F32),
                        pltpu.VMEM((rows, MLA_KV_LORA), BF16),
                        pltpu.VMEM((PAGE_SIZE, MLA_KV_LORA), F32),
                        pltpu.VMEM((LANES, PAGE_SIZE), BF16)],
    )
    return pl.pallas_call(
        functools.partial(_mla_dec_kernel, pb=pb, t_new=t_new),
        grid_spec=grid_spec,
        out_shape=jax.ShapeDtypeStruct((bd, t_new, wide), F32),
        compiler_params=_params("parallel", "arbitrary"),
        name="mla_decode",
    )(page_table, qnblk, qrblk, c_new, kr_new, wuk_t, wuv,
      *([cache_c] * pb), *([cache_krt] * pb))


def _rope_tables(pos):
    half = MLA_ROPE // 2
    inv = ROPE_THETA ** (-jnp.arange(half, dtype=F32) / half)
    ang = pos.astype(F32)[:, None] * inv
    reps = (LANES // 2) // half
    pad = jnp.zeros((pos.shape[0], LANES // 2), F32)
    cos = jnp.concatenate([jnp.tile(jnp.cos(ang), (1, reps)), pad], axis=1)
    sin = jnp.concatenate([jnp.tile(jnp.sin(ang), (1, reps)), pad], axis=1)
    return cos, sin


def _rot_cols(w):
    half = MLA_ROPE // 2
    return jnp.concatenate([-w[..., half:], w[..., :half]], axis=-1)


def _pair_rope_cols(w):
    r = w.shape[0]
    w = w.reshape(r, PAIRS, 2 * MLA_ROPE)
    return jnp.pad(w, ((0, 0), (0, 0), (0, LANES - 2 * MLA_ROPE))).reshape(r, PAIRS * LANES)


def _block_diag_rows(q, width):
    bd, t, n = q.shape
    rowh = np.arange(t * HEAD_PAD) % HEAD_PAD
    mask = jnp.asarray(rowh[:, None] == (np.arange(n) // width)[None, :])
    rep = jnp.repeat(q, HEAD_PAD, axis=1)
    return jnp.where(mask[None], rep, jnp.zeros_like(rep))


def kernel(x_prompt, x_sample, cache_fox_k, cache_fox_v, cache_fox_logf, cache_mla_ckv,
           cache_mla_krope, cache_mem_k, cache_mem_v, page_table, mem_prompt, norm_gains,
           ffn_w_up, ffn_w_down, fox_w_in, fox_b_f, mla_w_in, mla_q_norm, mla_w_q_b,
           mla_kv_norm, mla_w_kv_b, mem_norm, w_mem_kv, w_out, final_norm):
    b, s, d = x_prompt.shape
    bd, t_new, _ = x_sample.shape
    n_pages = page_table.shape[1]
    mem_len = mem_prompt.shape[1]
    depth = norm_gains.shape[0]
    xp = x_prompt.reshape(b * s, d)
    xs = x_sample.reshape(bd * t_new, d)

    w_up = ffn_w_up.astype(BF16)
    w_down = ffn_w_down.astype(BF16)
    w_out_b = w_out.astype(BF16)
    w_mem_b = w_mem_kv.astype(BF16)
    gains = norm_gains.reshape(depth, 3, 1, d)
    mem2d = mem_prompt.reshape(b * mem_len, d)

    outs = {}
    for layer in range(depth):
        j = layer // 2
        xp = _ffn_half(xp, gains[layer, 0], w_up, w_down, layer, 0)
        xs = _ffn_half(xs, gains[layer, 0], w_up, w_down, layer, 0)

        mkv = _norm_proj(mem2d, mem_norm[layer].reshape(1, d), w_mem_b[layer])
        mk = mkv[:, :X_WIDTH].reshape(b, mem_len, X_WIDTH)
        mv = mkv[:, X_WIDTH:].reshape(b, mem_len, X_WIDTH)
        outs.setdefault("mk", []).append(mk.reshape(b, mem_len, X_HEADS, X_HEAD_DIM))
        outs.setdefault("mv", []).append(mv.reshape(b, mem_len, X_HEADS, X_HEAD_DIM))
        mk_s = cache_mem_k[layer].reshape(bd, mem_len, X_WIDTH)
        mv_s = cache_mem_v[layer].reshape(bd, mem_len, X_WIDTH)

        if layer % 2 == 0:
            w_in = fox_w_in[j]
            wqkv = w_in[:, :3 * FOX_WIDTH].astype(BF16)
            wf = jnp.pad(w_in[:, 3 * FOX_WIDTH:3 * FOX_WIDTH + FOX_HEADS],
                         ((0, 0), (0, LANES - FOX_HEADS))).astype(BF16)
            bf = jnp.pad(fox_b_f[j], (0, LANES - FOX_HEADS)).reshape(1, LANES)
            wx = w_in[:, 3 * FOX_WIDTH + FOX_HEADS:].astype(BF16)
            g1 = gains[layer, 1]

            q, k, v, lf, xq_p = _fox_proj(xp, g1, wqkv, wf, bf, wx)
            c, ct = _fox_cumsum(lf.reshape(b, s, LANES))
            ct = ct[:, :FOX_HEADS].reshape(b, PAIRS, 2, s)
            mix_p = _flash([q], [k], v, b, s, c, ct)
            outs["fk_p"] = k.reshape(1, b, s, FOX_HEADS, FOX_HEAD_DIM)
            outs["fv_p"] = v.reshape(1, b, s, FOX_HEADS, FOX_HEAD_DIM)
            outs["fl_p"] = lf[:, :FOX_HEADS].reshape(1, b, s, FOX_HEADS)

            q, k, v, lf, xq_s = _fox_proj(xs, g1, wqkv, wf, bf, wx)
            outs["fk_s"] = k.reshape(1, bd, t_new, FOX_HEADS, FOX_HEAD_DIM)
            outs["fv_s"] = v.reshape(1, bd, t_new, FOX_HEADS, FOX_HEAD_DIM)
            outs["fl_s"] = lf[:, :FOX_HEADS].reshape(1, bd, t_new, FOX_HEADS)
            qblk = _block_diag_rows(q.reshape(bd, t_new, FOX_WIDTH), FOX_HEAD_DIM)
            lft = jnp.transpose(lf.reshape(bd, t_new, LANES)[:, :, :HEAD_PAD], (0, 2, 1))
            lft = jnp.pad(lft, ((0, 0), (0, 0), (0, LANES - t_new)))
            cache_kt = jnp.transpose(cache_fox_k, (0, 1, 3, 4, 2))
            cache_vt = jnp.transpose(cache_fox_v, (0, 1, 3, 4, 2))
            cache_lf = jnp.pad(jnp.transpose(cache_fox_logf[j], (0, 2, 1)),
                               ((0, 0), (0, HEAD_PAD - FOX_HEADS), (0, 0)))
            mix_s = _fox_decode(page_table, qblk, k.reshape(bd, t_new, FOX_WIDTH),
                                v.reshape(bd, t_new, FOX_WIDTH), lft,
                                cache_kt[j:j + 1], cache_vt[j:j + 1], cache_lf)
            mix_s = mix_s.reshape(bd * t_new, FOX_WIDTH).astype(BF16)
        else:
            w_in = mla_w_in[j]
            o1 = MLA_Q_LORA
            o2 = o1 + MLA_KV_LORA
            o3 = o2 + MLA_ROPE
            wkr = w_in[:, o2:o3]
            lane_pad = ((0, 0), (0, LANES - MLA_ROPE))
            wqb = mla_w_q_b[j]
            wkvb = mla_w_kv_b[j]
            weights = [
                w_in[:, :o1].astype(BF16),
                w_in[:, o1:o2].astype(BF16),
                jnp.pad(wkr, lane_pad).astype(BF16),
                jnp.pad(_rot_cols(wkr), lane_pad).astype(BF16),
                w_in[:, o3:].astype(BF16),
                mla_q_norm[j].reshape(1, MLA_Q_LORA),
                wqb[:, :, :MLA_NOPE].reshape(MLA_Q_LORA, MLA_HEADS * MLA_NOPE).astype(BF16),
                _pair_rope_cols(wqb[:, :, MLA_NOPE:]).astype(BF16),
                _pair_rope_cols(_rot_cols(wqb[:, :, MLA_NOPE:])).astype(BF16),
                mla_kv_norm[j].reshape(1, MLA_KV_LORA),
                wkvb[:, :, :MLA_NOPE].reshape(MLA_KV_LORA, MLA_HEADS * MLA_NOPE).astype(BF16),
                wkvb[:, :, MLA_NOPE:].reshape(MLA_KV_LORA, MLA_HEADS * MLA_V).astype(BF16),
            ]
            g1 = gains[layer, 1]
            cos_p, sin_p = _rope_tables(jnp.arange(s, dtype=jnp.int32))
            pos_s = n_pages * PAGE_SIZE + jnp.arange(t_new, dtype=jnp.int32)
            cos_s, sin_s = _rope_tables(jnp.tile(pos_s, bd))

            qn, qr, c, kr, xq_p, kn, v = _mla_proj(xp, g1, weights, cos_p, sin_p)
            kr2 = jnp.concatenate(
                [kr, kr, jnp.zeros((b * s, LANES - 2 * MLA_ROPE), F32)], axis=1).astype(BF16)
            kr2 = jnp.tile(kr2, (1, PAIRS))
            mix_p = _flash([qn, qr], [kn, kr2], v, b, s)
            outs["mc_p"] = c.reshape(1, b, s, MLA_KV_LORA)
            outs["mr_p"] = kr.reshape(1, b, s, MLA_ROPE)

            qn, qr, c, kr, xq_s, _, _ = _mla_proj(xs, g1, weights, cos_s, sin_s)
            outs["mc_s"] = c.reshape(1, bd, t_new, MLA_KV_LORA)
            outs["mr_s"] = kr.reshape(1, bd, t_new, MLA_ROPE)
            qnblk = _block_diag_rows(qn.reshape(bd, t_new, MLA_HEADS * MLA_NOPE), MLA_NOPE)
            qr4 = qr.reshape(bd, t_new, PAIRS, LANES)[..., :2 * MLA_ROPE]
            qr4 = qr4.reshape(bd, t_new, MLA_HEADS, MLA_ROPE)
            qr4 = jnp.pad(qr4, ((0, 0), (0, 0), (0, HEAD_PAD - MLA_HEADS), (0, LANES - MLA_ROPE)))
            qrblk = qr4.reshape(bd, t_new * HEAD_PAD, LANES)
            kr_new = jnp.pad(kr.reshape(bd, t_new, MLA_ROPE).astype(BF16),
                             ((0, 0), (0, PAGE_SIZE - t_new), (0, LANES - MLA_ROPE)))
            wuk_t = jnp.transpose(wkvb[:, :, :MLA_NOPE], (1, 2, 0)).reshape(
                MLA_HEADS * MLA_NOPE, MLA_KV_LORA).astype(BF16)
            cache_krt = jnp.transpose(cache_mla_krope, (0, 1, 3, 2))
            mix_s = _mla_decode(page_table, qnblk, qrblk, c.reshape(bd, t_new, MLA_KV_LORA),
                                kr_new, wuk_t, weights[11], cache_mla_ckv[j:j + 1],
                                cache_krt[j:j + 1])
            mix_s = mix_s.reshape(bd * t_new, MLA_HEADS * MLA_V).astype(BF16)

        cross_p = _cross(xq_p.reshape(b, s, X_WIDTH), mk, mv).reshape(b * s, X_WIDTH)
        cross_s = _cross(xq_s.reshape(bd, t_new, X_WIDTH), mk_s, mv_s).reshape(bd * t_new, X_WIDTH)
        xp = _merge(xp, mix_p, cross_p, w_out_b[layer])
        xs = _merge(xs, mix_s, cross_s, w_out_b[layer])
        xp = _ffn_half(xp, gains[layer, 2], w_up, w_down, layer, 1)
        xs = _ffn_half(xs, gains[layer, 2], w_up, w_down, layer, 1)

    fin = final_norm.reshape(1, d)
    y_prompt = _final_norm(xp, fin).reshape(b, s, d)
    y_sample = _final_norm(xs, fin).reshape(bd, t_new, d)
    return (y_prompt, y_sample,
            outs["fk_p"], outs["fv_p"], outs["fl_p"],
            outs["fk_s"], outs["fv_s"], outs["fl_s"],
            outs["mc_p"], outs["mr_p"], outs["mc_s"], outs["mr_s"],
            jnp.stack(outs["mk"]), jnp.stack(outs["mv"]))
```

```python
import functools

import numpy as np
import jax
import jax.numpy as jnp
from jax import lax
from jax.experimental import pallas as pl
from jax.experimental.pallas import tpu as pltpu

D_MODEL = 1024
PAGE_SIZE = 128
X_HEADS = 4
X_HEAD_DIM = 64
X_WIDTH = X_HEADS * X_HEAD_DIM
FOX_HEADS = 12
FOX_HEAD_DIM = 64
FOX_WIDTH = FOX_HEADS * FOX_HEAD_DIM
MLA_HEADS = 12
MLA_NOPE = 64
MLA_ROPE = 32
MLA_V = 64
MLA_Q_LORA = 384
MLA_KV_LORA = 256
FFN_HIDDEN = 2816
ROPE_THETA = 10000.0
RMS_EPS = 1e-6

LANES = 128
HEAD_PAD = 16
PAIRS = FOX_HEADS // 2
VMEM_LIMIT = 56 * 1024 * 1024

F32 = jnp.float32
BF16 = jnp.bfloat16
NT_DIMS = (((1,), (1,)), ((), ()))
NEG_INF = float("-inf")
LOG2E = 1.4426950408889634


def _params(*sem):
    return pltpu.CompilerParams(dimension_semantics=sem, vmem_limit_bytes=VMEM_LIMIT)


def _rms(x, g):
    ms = jnp.mean(x * x, axis=-1, keepdims=True)
    return x * lax.rsqrt(ms + RMS_EPS) * g


def _dot(a, b):
    return jnp.dot(a, b, preferred_element_type=F32)


def _dot_nt(a, b):
    return lax.dot_general(a, b, NT_DIMS, preferred_element_type=F32)


def _dot_exact(a, b):
    return jnp.dot(a, b, preferred_element_type=F32, precision=lax.Precision.HIGHEST)


def _row_tile(t, pref):
    return pref if t % pref == 0 else t


def _ffn_kernel(x_ref, g_ref, wg_ref, wu_ref, wd_ref, o_ref, xn_ref, acc_ref):
    j = pl.program_id(1)

    @pl.when(j == 0)
    def _():
        xn_ref[...] = _rms(x_ref[...], g_ref[...]).astype(BF16)
        acc_ref[...] = jnp.zeros_like(acc_ref)

    xn = xn_ref[...]
    gate = _dot(xn, wg_ref[...])
    up = _dot(xn, wu_ref[...])
    h = (gate / (1.0 + jnp.exp(-gate)) * up).astype(BF16)
    acc_ref[...] += _dot(h, wd_ref[...])

    @pl.when(j == pl.num_programs(1) - 1)
    def _():
        o_ref[...] = x_ref[...] + 0.5 * acc_ref[...]


def _ffn_half(x, g, w_up, w_down, layer, idx):
    t = x.shape[0]
    tm = _row_tile(t, 512)
    th = FFN_HIDDEN // 2
    nj = FFN_HIDDEN // th
    return pl.pallas_call(
        _ffn_kernel,
        grid=(t // tm, nj),
        in_specs=[
            pl.BlockSpec((tm, D_MODEL), lambda m, j: (m, 0)),
            pl.BlockSpec((1, D_MODEL), lambda m, j: (0, 0)),
            pl.BlockSpec((None, None, D_MODEL, th), lambda m, j: (layer, idx, 0, j)),
            pl.BlockSpec((None, None, D_MODEL, th), lambda m, j: (layer, idx, 0, j + nj)),
            pl.BlockSpec((None, None, th, D_MODEL), lambda m, j: (layer, idx, j, 0)),
        ],
        out_specs=pl.BlockSpec((tm, D_MODEL), lambda m, j: (m, 0)),
        out_shape=jax.ShapeDtypeStruct((t, D_MODEL), F32),
        scratch_shapes=[pltpu.VMEM((tm, D_MODEL), BF16), pltpu.VMEM((tm, D_MODEL), F32)],
        compiler_params=_params("parallel", "arbitrary"),
        name="ffn_half",
    )(x, g, w_up, w_up, w_down)


def _log_sigmoid(f):
    return jnp.minimum(f, 0.0) - jnp.log(1.0 + jnp.exp(-jnp.abs(f)))


def _fox_proj_kernel(x_ref, g_ref, wqkv_ref, wf_ref, bf_ref, wx_ref,
                     q_ref, k_ref, v_ref, lf_ref, xq_ref, *, qscale):
    hn = _rms(x_ref[...], g_ref[...]).astype(BF16)
    qkv = _dot(hn, wqkv_ref[...])
    q_ref[...] = (qkv[:, :FOX_WIDTH] * qscale).astype(BF16)
    k_ref[...] = qkv[:, FOX_WIDTH:2 * FOX_WIDTH]
    v_ref[...] = qkv[:, 2 * FOX_WIDTH:]
    lf_ref[...] = _log_sigmoid(_dot(hn, wf_ref[...]) + bf_ref[...])
    xq_ref[...] = (_dot(hn, wx_ref[...]) * (X_HEAD_DIM ** -0.5)).astype(BF16)


def _fox_proj_t_kernel(x_ref, g_ref, wqk_ref, wkvt_ref, wf_ref, bf_ref, wx_ref,
                       q_ref, k_ref, kt_ref, vt_ref, lf_ref, xq_ref, *, qscale):
    hn = _rms(x_ref[...], g_ref[...]).astype(BF16)
    qk = _dot(hn, wqk_ref[...])
    q_ref[...] = (qk[:, :FOX_WIDTH] * qscale).astype(BF16)
    k_ref[...] = qk[:, FOX_WIDTH:].astype(BF16)
    kvt = _dot_nt(wkvt_ref[...], hn)
    kt_ref[0] = kvt[:FOX_WIDTH, :]
    vt_ref[0] = kvt[FOX_WIDTH:, :]
    lf_ref[...] = _log_sigmoid(_dot(hn, wf_ref[...]) + bf_ref[...])
    xq_ref[...] = (_dot(hn, wx_ref[...]) * (X_HEAD_DIM ** -0.5)).astype(BF16)


def _fox_proj(x, g, wqkv, wf, bf, wx, qscale):
    t = x.shape[0]
    tm = _row_tile(t, 512)
    row = lambda n: pl.BlockSpec((tm, n), lambda m: (m, 0))
    full = lambda a: pl.BlockSpec(a.shape, lambda m: (0,) * a.ndim)
    return pl.pallas_call(
        functools.partial(_fox_proj_kernel, qscale=qscale),
        grid=(t // tm,),
        in_specs=[row(D_MODEL), full(g), full(wqkv), full(wf), full(bf), full(wx)],
        out_specs=[row(FOX_WIDTH), row(FOX_WIDTH), row(FOX_WIDTH), row(LANES), row(X_WIDTH)],
        out_shape=[
            jax.ShapeDtypeStruct((t, FOX_WIDTH), BF16),
            jax.ShapeDtypeStruct((t, FOX_WIDTH), F32),
            jax.ShapeDtypeStruct((t, FOX_WIDTH), F32),
            jax.ShapeDtypeStruct((t, LANES), F32),
            jax.ShapeDtypeStruct((t, X_WIDTH), BF16),
        ],
        compiler_params=_params("parallel"),
        name="fox_proj",
    )(x, g, wqkv, wf, bf, wx)


def _fox_proj_t(x, g, wqk, wkvt, wf, bf, wx, qscale, b, s):
    t = x.shape[0]
    tm = _row_tile(s, 512)
    ns = s // tm
    row = lambda n: pl.BlockSpec((tm, n), lambda m: (m, 0))
    full = lambda a: pl.BlockSpec(a.shape, lambda m: (0,) * a.ndim)
    tspec = pl.BlockSpec((1, FOX_WIDTH, tm), lambda m: (m // ns, 0, m % ns))
    return pl.pallas_call(
        functools.partial(_fox_proj_t_kernel, qscale=qscale),
        grid=(t // tm,),
        in_specs=[row(D_MODEL), full(g), full(wqk), full(wkvt), full(wf), full(bf), full(wx)],
        out_specs=[row(FOX_WIDTH), row(FOX_WIDTH), tspec, tspec, row(LANES), row(X_WIDTH)],
        out_shape=[
            jax.ShapeDtypeStruct((t, FOX_WIDTH), BF16),
            jax.ShapeDtypeStruct((t, FOX_WIDTH), BF16),
            jax.ShapeDtypeStruct((b, FOX_WIDTH, s), F32),
            jax.ShapeDtypeStruct((b, FOX_WIDTH, s), F32),
            jax.ShapeDtypeStruct((t, LANES), F32),
            jax.ShapeDtypeStruct((t, X_WIDTH), BF16),
        ],
        compiler_params=_params("parallel"),
        name="fox_proj_t",
    )(x, g, wqk, wkvt, wf, bf, wx)


def _cumsum_kernel(lf_ref, tri_ref, c_ref, carry_ref):
    @pl.when(pl.program_id(1) == 0)
    def _():
        carry_ref[...] = jnp.zeros_like(carry_ref)

    c = _dot_exact(tri_ref[...], lf_ref[0]) + carry_ref[...]
    c_ref[0] = c * LOG2E
    carry_ref[...] = c[-1:, :]


def _fox_cumsum(lf):
    b, s, _ = lf.shape
    ts = _row_tile(s, 256)
    tri = jnp.asarray(np.tril(np.ones((ts, ts), np.float32)))
    return pl.pallas_call(
        _cumsum_kernel,
        grid=(b, s // ts),
        in_specs=[
            pl.BlockSpec((1, ts, LANES), lambda i, j: (i, j, 0)),
            pl.BlockSpec((ts, ts), lambda i, j: (0, 0)),
        ],
        out_specs=pl.BlockSpec((1, ts, LANES), lambda i, j: (i, j, 0)),
        out_shape=jax.ShapeDtypeStruct((b, s, LANES), F32),
        scratch_shapes=[pltpu.VMEM((1, LANES), F32)],
        compiler_params=_params("parallel", "arbitrary"),
        name="fox_cumsum",
    )(lf, tri)


BIAS_TERMS = 3


def _split3(x):
    hi = x.astype(BF16)
    r1 = x - hi.astype(F32)
    mid = r1.astype(BF16)
    lo = (r1 - mid.astype(F32)).astype(BF16)
    return hi, mid, lo


def _place(parts, src_lane, dst_lane0, sign):
    r = lax.broadcasted_iota(jnp.int32, (LANES, LANES), 0)
    c = lax.broadcasted_iota(jnp.int32, (LANES, LANES), 1)
    out = None
    for i, part in enumerate(parts):
        sel = jnp.where((r == src_lane) & (c == dst_lane0 + i), sign, 0.0).astype(BF16)
        term = _dot(part, sel)
        out = term if out is None else out + term
    return out


def _ones_lanes(shape, lo, n):
    lane = lax.broadcasted_iota(jnp.int32, shape, 1)
    return jnp.where((lane >= lo) & (lane < lo + n), 1.0, 0.0)


def _group8(x, op):
    parts = [x[r * 8:(r + 1) * 8, :] for r in range(x.shape[0] // 8)]
    while len(parts) > 1:
        parts = [op(parts[i], parts[i + 1]) for i in range(0, len(parts), 2)]
    return parts[0]


def _flash_kernel(*refs, tile, n_q, n_k, use_c):
    q_refs = refs[:n_q]
    k_refs = refs[n_q:n_q + n_k]
    vt_ref = refs[n_q + n_k]
    pos = n_q + n_k + 1
    if use_c:
        c_ref = refs[pos]
        pos += 1
    o_ref, kb_ref, vtb_ref, s_ref, p_ref, acc_ref, st_ref = refs[pos:pos + 7]
    pair = pl.program_id(1)
    qi = pl.program_id(2)
    tq = tk = tile
    chunk = 64
    nb = BIAS_TERMS

    @pl.when(qi == 0)
    def _():
        for n, k_ref in enumerate(k_refs):
            kb_ref[:, n * LANES:(n + 1) * LANES] = k_ref[...].astype(BF16)
        seq = vt_ref.shape[2]
        vtb_ref[:, :seq] = vt_ref[0].astype(BF16)
        vtb_ref[:, seq:] = jnp.zeros((LANES, tk), BF16)
        p_ref[1] = jnp.zeros(p_ref.shape[1:], BF16)
        if use_c:
            parts = _split3(c_ref[0])
            feat = (_place(parts, 2 * pair, 0, -1.0) + _place(parts, 2 * pair + 1, nb, -1.0)
                    + _ones_lanes((1, LANES), 2 * nb, nb))
            kb_ref[:, n_k * LANES:] = feat.astype(BF16)

    q = jnp.concatenate([r[...] for r in q_refs], axis=1) if n_q > 1 else q_refs[0][...]
    lane = lax.broadcasted_iota(jnp.int32, (1, q.shape[1]), 1)
    if n_q == 1:
        own = (lane < FOX_HEAD_DIM, lane >= FOX_HEAD_DIM)
    else:
        own = ((lane < MLA_NOPE) | ((lane >= LANES) & (lane < LANES + MLA_ROPE)),
               ((lane >= MLA_NOPE) & (lane < LANES))
               | ((lane >= LANES + MLA_ROPE) & (lane < LANES + 2 * MLA_ROPE)))
    zero = jnp.zeros_like(q)
    qh = [jnp.where(own[h], q, zero) for h in range(2)]
    if use_c:
        q0 = pl.multiple_of(qi * tq, tq)
        parts = _split3(c_ref[0, pl.ds(q0, tq), :])
        for h in range(2):
            feat = _place(parts, 2 * pair + h, 2 * nb, 1.0) + _ones_lanes((1, LANES), h * nb, nb)
            qh[h] = jnp.concatenate([qh[h], feat.astype(BF16)], axis=1)

    krow = lax.broadcasted_iota(jnp.int32, (chunk, tq), 0)
    qcol = lax.broadcasted_iota(jnp.int32, (chunk, tq), 1)

    def keys_of(blk):
        return pl.ds(blk * tk if isinstance(blk, int) else pl.multiple_of(blk * tk, tk), tk)

    def scores(slot, blk):
        kblk = kb_ref[keys_of(blk), :]
        for h in range(2):
            s_ref[slot, h] = _dot_nt(kblk, qh[h])

    def values(slot, blk, stats):
        vtblk = vtb_ref[:, keys_of(blk)]
        for h in range(2):
            acc_ref[h] = acc_ref[h] * stats[h][2] + _dot(vtblk, p_ref[slot, h])

    def softmax(slot, stats, diagonal):
        out = []
        for h in range(2):
            m, l, _ = stats[h]
            mx = None
            for c in range(tk // chunk):
                rows = slice(c * chunk, (c + 1) * chunk)
                blk = s_ref[slot, h, rows, :]
                if diagonal:
                    blk = jnp.where(krow + c * chunk > qcol, NEG_INF, blk)
                    s_ref[slot, h, rows, :] = blk
                cm = _group8(blk, jnp.maximum)
                mx = cm if mx is None else jnp.maximum(mx, cm)
            m_new = jnp.maximum(m, jnp.max(mx, axis=0, keepdims=True))
            alpha = jnp.exp2(m - m_new)
            ls = None
            for c in range(tk // chunk):
                rows = slice(c * chunk, (c + 1) * chunk)
                p = jnp.exp2(s_ref[slot, h, rows, :] - m_new)
                ps = _group8(p, jnp.add)
                ls = ps if ls is None else ls + ps
                p_ref[slot, h, rows, :] = p.astype(BF16)
            l = l * alpha + jnp.sum(ls, axis=0, keepdims=True)
            out.append((m_new, l, alpha))
        return tuple(out)

    def save(stats):
        for h in range(2):
            for n in range(3):
                st_ref[3 * h + n] = stats[h][n]

    def load():
        return tuple(tuple(st_ref[3 * h + n] for n in range(3)) for h in range(2))

    zero_blk = vt_ref.shape[2] // tk

    def prev(j):
        return jnp.where(j >= 0, j, zero_blk)

    scores(0, 0)
    acc_ref[...] = jnp.zeros_like(acc_ref)
    init = tuple((jnp.full((1, tq), NEG_INF, F32), jnp.zeros((1, tq), F32),
                  jnp.ones((1, tq), F32)) for _ in range(2))

    def two_stages(t, stats):
        j = 2 * t
        scores(1, j + 1)
        values(1, prev(j - 1), stats)
        stats = softmax(0, stats, False)
        scores(0, j + 2)
        values(0, j, stats)
        return softmax(1, stats, False)

    save(lax.fori_loop(0, qi // 2, two_stages, init))

    @pl.when(qi % 2 == 0)
    def _():
        stats = load()
        values(1, prev(qi - 1), stats)
        stats = softmax(0, stats, True)
        values(0, qi, stats)
        save(stats)

    @pl.when(qi % 2 == 1)
    def _():
        stats = load()
        scores(1, qi)
        values(1, prev(qi - 2), stats)
        stats = softmax(0, stats, False)
        values(0, qi - 1, stats)
        stats = softmax(1, stats, True)
        values(1, qi, stats)
        save(stats)

    row = lax.broadcasted_iota(jnp.int32, (LANES, 1), 0)
    o_t = jnp.where(row < FOX_HEAD_DIM, acc_ref[0] / st_ref[1], acc_ref[1] / st_ref[4])
    o_ref[...] = o_t.T.astype(o_ref.dtype)


FLASH_TILE = 512


def _flash(qs, ks, vt, b, s, c=None):
    tq = _row_tile(s, FLASH_TILE)
    nq = s // tq
    use_c = c is not None
    kd = LANES * (len(ks) + (1 if use_c else 0))
    qspec = pl.BlockSpec((tq, LANES), lambda i, p, j: (i * nq + j, p))
    kspec = pl.BlockSpec((s, LANES), lambda i, p, j: (i, p))
    in_specs = [qspec] * len(qs) + [kspec] * len(ks)
    in_specs.append(pl.BlockSpec((1, LANES, s), lambda i, p, j: (i, p, 0)))
    args = list(qs) + list(ks) + [vt]
    if use_c:
        in_specs.append(pl.BlockSpec((1, s, LANES), lambda i, p, j: (i, 0, 0)))
        args.append(c)
    kern = functools.partial(_flash_kernel, tile=tq, n_q=len(qs), n_k=len(ks), use_c=use_c)
    return pl.pallas_call(
        kern,
        grid=(b, PAIRS, nq),
        in_specs=in_specs,
        out_specs=qspec,
        out_shape=jax.ShapeDtypeStruct((b * s, PAIRS * LANES), BF16),
        scratch_shapes=[pltpu.VMEM((s, kd), BF16), pltpu.VMEM((LANES, s + tq), BF16),
                        pltpu.VMEM((2, 2, tq, tq), F32), pltpu.VMEM((2, 2, tq, tq), BF16),
                        pltpu.VMEM((2, LANES, tq), F32), pltpu.VMEM((6, 1, tq), F32)],
        compiler_params=_params("parallel", "parallel", "arbitrary"),
        name="flash_mla" if len(qs) > 1 else "flash_fox",
    )(*args)


def _cross_kernel(q_ref, k_ref, v_ref, o_ref):
    q = q_ref[0].astype(BF16)
    kb = k_ref[0].astype(BF16)
    vb = v_ref[0].astype(BF16)
    lane = lax.broadcasted_iota(jnp.int32, (1, LANES), 1)
    first = lane < X_HEAD_DIM
    zero = jnp.zeros_like(q)
    outs = []
    for qh in (jnp.where(first, q, zero), jnp.where(first, zero, q)):
        s = _dot_nt(qh, kb)
        m = jnp.max(s, axis=1, keepdims=True)
        p = jnp.exp(s - m)
        l = jnp.sum(p, axis=1, keepdims=True)
        outs.append(_dot(p.astype(BF16), vb) / l)
    o_ref[0] = jnp.where(first, outs[0], outs[1]).astype(o_ref.dtype)


def _cross(xq, mk, mv):
    b, t, _ = xq.shape
    m = mk.shape[1]
    tq = _row_tile(t, 512)
    qspec = pl.BlockSpec((1, tq, LANES), lambda i, p, j: (i, j, p))
    kspec = pl.BlockSpec((1, m, LANES), lambda i, p, j: (i, 0, p))
    return pl.pallas_call(
        _cross_kernel,
        grid=(b, X_WIDTH // LANES, t // tq),
        in_specs=[qspec, kspec, kspec],
        out_specs=qspec,
        out_shape=jax.ShapeDtypeStruct((b, t, X_WIDTH), BF16),
        compiler_params=_params("parallel", "parallel", "arbitrary"),
        name="cross_attn",
    )(xq, mk, mv)


def _merge_kernel(x_ref, mix_ref, cross_ref, w_ref, o_ref):
    nm = mix_ref.shape[1]
    o_ref[...] = (x_ref[...] + _dot(mix_ref[...], w_ref[:nm, :])
                  + _dot(cross_ref[...], w_ref[nm:, :]))


def _merge(x, mix, cross, w):
    t = x.shape[0]
    tm = _row_tile(t, 512)
    row = lambda n: pl.BlockSpec((tm, n), lambda m: (m, 0))
    return pl.pallas_call(
        _merge_kernel,
        grid=(t // tm,),
        in_specs=[row(D_MODEL), row(mix.shape[1]), row(cross.shape[1]),
                  pl.BlockSpec(w.shape, lambda m: (0, 0))],
        out_specs=row(D_MODEL),
        out_shape=jax.ShapeDtypeStruct((t, D_MODEL), F32),
        compiler_params=_params("parallel"),
        name="out_proj",
    )(x, mix, cross, w)


def _norm_proj_kernel(x_ref, g_ref, w_ref, o_ref):
    o_ref[...] = _dot(_rms(x_ref[...], g_ref[...]).astype(BF16), w_ref[...])


def _norm_proj(x, g, w):
    t = x.shape[0]
    n = w.shape[1]
    tm = _row_tile(t, 512)
    return pl.pallas_call(
        _norm_proj_kernel,
        grid=(t // tm,),
        in_specs=[pl.BlockSpec((tm, D_MODEL), lambda m: (m, 0)),
                  pl.BlockSpec((1, D_MODEL), lambda m: (0, 0)),
                  pl.BlockSpec(w.shape, lambda m: (0, 0))],
        out_specs=pl.BlockSpec((tm, n), lambda m: (m, 0)),
        out_shape=jax.ShapeDtypeStruct((t, n), F32),
        compiler_params=_params("parallel"),
        name="norm_proj",
    )(x, g, w)


def _final_norm_kernel(x_ref, g_ref, o_ref):
    o_ref[...] = _rms(x_ref[...], g_ref[...])


def _final_norm(x, g):
    t = x.shape[0]
    tm = _row_tile(t, 512)
    return pl.pallas_call(
        _final_norm_kernel,
        grid=(t // tm,),
        in_specs=[pl.BlockSpec((tm, D_MODEL), lambda m: (m, 0)),
                  pl.BlockSpec((1, D_MODEL), lambda m: (0, 0))],
        out_specs=pl.BlockSpec((tm, D_MODEL), lambda m: (m, 0)),
        out_shape=jax.ShapeDtypeStruct((t, D_MODEL), F32),
        compiler_params=_params("parallel"),
        name="final_norm",
    )(x, g)


def _mla_proj_kernel(*refs, qscale, with_kv):
    (x_ref, g_ref, wqa_ref, wkva_ref, wkr_ref, wkrr_ref, wx_ref,
     qg_ref, wqn_ref, wqr_ref, wqrr_ref, kvg_ref) = refs[:12]
    pos = 12
    if with_kv:
        wuk_ref, wuvt_ref = refs[pos:pos + 2]
        pos += 2
    cos_ref, sin_ref = refs[pos:pos + 2]
    qn_ref, qr_ref, c_ref, kr_ref, xq_ref = refs[pos + 2:pos + 7]
    hn = _rms(x_ref[...], g_ref[...]).astype(BF16)
    cos = cos_ref[...]
    sin = sin_ref[...]
    qa = _rms(_dot(hn, wqa_ref[...]), qg_ref[...]).astype(BF16)
    qn_ref[...] = (_dot(qa, wqn_ref[...]) * qscale).astype(BF16)
    qr = _dot(qa, wqr_ref[...])
    qrr = _dot(qa, wqrr_ref[...])
    for p in range(PAIRS):
        sl = slice(p * LANES, (p + 1) * LANES)
        qr_ref[:, sl] = ((qr[:, sl] * cos + qrr[:, sl] * sin) * qscale).astype(BF16)
    c = _rms(_dot(hn, wkva_ref[...]), kvg_ref[...])
    c_ref[...] = c
    if with_kv:
        kn_ref, vt_ref = refs[pos + 7:pos + 9]
        cb = c.astype(BF16)
        kn_ref[...] = _dot(cb, wuk_ref[...]).astype(BF16)
        vt_ref[0] = _dot_nt(wuvt_ref[...], cb).astype(BF16)
    kr = _dot(hn, wkr_ref[...]) * cos + _dot(hn, wkrr_ref[...]) * sin
    kr_ref[...] = kr[:, :MLA_ROPE]
    xq_ref[...] = (_dot(hn, wx_ref[...]) * (X_HEAD_DIM ** -0.5)).astype(BF16)


def _mla_proj(x, g, w, cos, sin, qscale, kv_w=None, b=None, s=None):
    t = x.shape[0]
    with_kv = kv_w is not None
    tm = _row_tile(s if with_kv else t, 512)
    ntab = cos.shape[0] // tm
    row = lambda n: pl.BlockSpec((tm, n), lambda m: (m, 0))
    full = lambda a: pl.BlockSpec(a.shape, lambda m: (0,) * a.ndim)
    tab = pl.BlockSpec((tm, LANES), lambda m: (m % ntab, 0))
    wide = PAIRS * LANES
    w = list(w) + (list(kv_w) if with_kv else [])
    out_specs = [row(wide), row(wide), row(MLA_KV_LORA), row(MLA_ROPE), row(X_WIDTH)]
    out_shape = [
        jax.ShapeDtypeStruct((t, wide), BF16),
        jax.ShapeDtypeStruct((t, wide), BF16),
        jax.ShapeDtypeStruct((t, MLA_KV_LORA), F32),
        jax.ShapeDtypeStruct((t, MLA_ROPE), F32),
        jax.ShapeDtypeStruct((t, X_WIDTH), BF16),
    ]
    if with_kv:
        ns = s // tm
        out_specs += [row(wide), pl.BlockSpec((1, wide, tm), lambda m: (m // ns, 0, m % ns))]
        out_shape += [jax.ShapeDtypeStruct((t, wide), BF16),
                      jax.ShapeDtypeStruct((b, wide, s), BF16)]
    return pl.pallas_call(
        functools.partial(_mla_proj_kernel, qscale=qscale, with_kv=with_kv),
        grid=(t // tm,),
        in_specs=[row(D_MODEL), full(g)] + [full(a) for a in w] + [tab, tab],
        out_specs=out_specs,
        out_shape=out_shape,
        compiler_params=_params("parallel"),
        name="mla_proj_kv" if with_kv else "mla_proj",
    )(x, g, *w, cos, sin)


FOX_PAGES_PER_STEP = 16
MLA_PAGES_PER_STEP = 32


def _pages_per_step(n_pages, pref):
    while n_pages % pref:
        pref //= 2
    return pref


PAGE_GROUPS = 4


def _online_step(state, s_parts, v_parts, v_is_transposed):
    m, l, acc = state
    n = len(s_parts)
    per = n // PAGE_GROUPS if n % PAGE_GROUPS == 0 else n
    partials = []
    for g0 in range(0, n, per):
        s = jnp.concatenate(s_parts[g0:g0 + per], axis=1) if per > 1 else s_parts[g0]
        mg = jnp.max(s, axis=1, keepdims=True)
        p = jnp.exp(s - mg)
        lg = jnp.sum(p, axis=1, keepdims=True)
        ag = None
        for r in range(per):
            pb = p[:, r * LANES:(r + 1) * LANES].astype(BF16)
            vb = v_parts[g0 + r]
            t = _dot_nt(pb, vb) if v_is_transposed else _dot(pb, vb)
            ag = t if ag is None else ag + t
        partials.append((mg, lg, ag))
    m_new = m
    for mg, _, _ in partials:
        m_new = jnp.maximum(m_new, mg)
    alpha = jnp.exp(m - m_new)
    l = l * alpha
    acc = acc * alpha
    for mg, lg, ag in partials:
        w = jnp.exp(mg - m_new)
        l = l + lg * w
        acc = acc + ag * w
    return m_new, l, acc


def _head_diag(acc, t_new, width):
    n = acc.shape[1]
    rowh = lax.broadcasted_iota(jnp.int32, (HEAD_PAD, n), 0)
    laneh = lax.broadcasted_iota(jnp.int32, (HEAD_PAD, n), 1) // width
    orow = lax.broadcasted_iota(jnp.int32, (t_new, n), 0)
    out = jnp.zeros((t_new, n), F32)
    for i in range(t_new):
        blk = acc[i * HEAD_PAD:(i + 1) * HEAD_PAD, :]
        r = jnp.sum(jnp.where(rowh == laneh, blk, 0.0), axis=0, keepdims=True)
        out = jnp.where(orow == i, r, out)
    return out


def _fox_dec_kernel(pt_ref, qb_ref, kn_ref, vn_ref, lft_ref, sufm_ref, *rest, pb, t_new):
    kt_refs = rest[:pb]
    vt_refs = rest[pb:2 * pb]
    lf_refs = rest[2 * pb:3 * pb]
    o_ref, m_ref, l_ref, acc_ref, run_ref, pad_ref = rest[3 * pb:]
    step = pl.program_id(1)
    rows = t_new * HEAD_PAD
    qb = qb_ref[0]
    lft = lft_ref[0]
    lane = lax.broadcasted_iota(jnp.int32, (HEAD_PAD, LANES), 1)
    cn_cols = []
    run = jnp.zeros((HEAD_PAD, 1), F32)
    for i in range(t_new):
        run = run + lft[:, i:i + 1]
        cn_cols.append(run)
    cn_col = jnp.concatenate(cn_cols, axis=0)

    @pl.when(step == 0)
    def _():
        cn_lanes = jnp.zeros((HEAD_PAD, LANES), F32)
        for i in range(t_new):
            cn_lanes = jnp.where(lane == i, cn_cols[i], cn_lanes)
        pad_ref[...] = jnp.zeros_like(pad_ref)
        pad_ref[0:t_new, :] = kn_ref[0]
        kpad = pad_ref[...].astype(BF16)
        pad_ref[0:t_new, :] = vn_ref[0]
        vpad = pad_ref[...].astype(BF16)
        s = _dot_nt(qb, kpad)
        parts = []
        for i in range(t_new):
            blk = s[i * HEAD_PAD:(i + 1) * HEAD_PAD, :] + (cn_cols[i] - cn_lanes)
            parts.append(jnp.where(lane <= i, blk, NEG_INF))
        s = jnp.concatenate(parts, axis=0)
        m = jnp.max(s, axis=1, keepdims=True)
        p = jnp.exp(s - m)
        m_ref[...] = m
        l_ref[...] = jnp.sum(p, axis=1, keepdims=True)
        acc_ref[...] = _dot(p.astype(BF16), vpad)
        run_ref[...] = jnp.zeros_like(run_ref)

    run = run_ref[...]
    s_parts, v_parts = [], []
    lf_all = jnp.concatenate([lf_refs[r][0] for r in range(pb)], axis=0)
    suf_all = _dot_exact(lf_all, sufm_ref[...])
    tot_all = jnp.sum(lf_all, axis=1, keepdims=True)
    for r in range(pb):
        heads = slice(r * HEAD_PAD, (r + 1) * HEAD_PAD)
        suf = suf_all[heads, :] + run
        run = run + tot_all[heads, :]
        ktb = kt_refs[r][0, 0].reshape(FOX_WIDTH, PAGE_SIZE).astype(BF16)
        s = _dot(qb, ktb)
        bias = jnp.concatenate([suf] * t_new, axis=0) + cn_col
        s_parts.append(s + bias)
        v_parts.append(vt_refs[r][0, 0].reshape(FOX_WIDTH, PAGE_SIZE).astype(BF16))
    run_ref[...] = run
    m, l, acc = _online_step((m_ref[...], l_ref[...], acc_ref[...]), s_parts, v_parts, True)
    m_ref[...] = m
    l_ref[...] = l
    acc_ref[...] = acc

    @pl.when(step == pl.num_programs(1) - 1)
    def _():
        o_ref[0] = _head_diag(acc / l, t_new, FOX_HEAD_DIM)


def _fox_decode(page_table, qblk, k_new, v_new, lft_new, cache_kt, cache_vt, cache_lf):
    bd, n_pages = page_table.shape
    t_new = k_new.shape[1]
    rows = t_new * HEAD_PAD
    pb = _pages_per_step(n_pages, FOX_PAGES_PER_STEP)
    sufm = jnp.asarray(np.tril(np.ones((PAGE_SIZE, PAGE_SIZE), np.float32), -1))

    def page(r):
        return lambda b, s, pt: pt[b, n_pages - 1 - (s * pb + r)]

    kv_specs = [pl.BlockSpec((1, 1, FOX_HEADS, FOX_HEAD_DIM, PAGE_SIZE),
                             (lambda b, s, pt, f=page(r): (0, f(b, s, pt), 0, 0, 0)))
                for r in range(pb)]
    lf_specs = [pl.BlockSpec((1, HEAD_PAD, PAGE_SIZE),
                             (lambda b, s, pt, f=page(r): (f(b, s, pt), 0, 0)))
                for r in range(pb)]
    per_b = lambda shape: pl.BlockSpec((1,) + shape, lambda b, s, pt: (b, 0, 0))
    grid_spec = pltpu.PrefetchScalarGridSpec(
        num_scalar_prefetch=1,
        grid=(bd, n_pages // pb),
        in_specs=[per_b((rows, FOX_WIDTH)), per_b((t_new, FOX_WIDTH)), per_b((t_new, FOX_WIDTH)),
                  per_b((HEAD_PAD, LANES)),
                  pl.BlockSpec((PAGE_SIZE, PAGE_SIZE), lambda b, s, pt: (0, 0))]
        + kv_specs + kv_specs + lf_specs,
        out_specs=per_b((t_new, FOX_WIDTH)),
        scratch_shapes=[pltpu.VMEM((rows, 1), F32), pltpu.VMEM((rows, 1), F32),
                        pltpu.VMEM((rows, FOX_WIDTH), F32), pltpu.VMEM((HEAD_PAD, 1), F32),
                        pltpu.VMEM((PAGE_SIZE, FOX_WIDTH), F32)],
    )
    return pl.pallas_call(
        functools.partial(_fox_dec_kernel, pb=pb, t_new=t_new),
        grid_spec=grid_spec,
        out_shape=jax.ShapeDtypeStruct((bd, t_new, FOX_WIDTH), F32),
        compiler_params=_params("parallel", "arbitrary"),
        name="fox_decode",
    )(page_table, qblk, k_new, v_new, lft_new, sufm,
      *([cache_kt] * pb), *([cache_vt] * pb), *([cache_lf] * pb))


def _mla_dec_kernel(pt_ref, qn_ref, qr_ref, cn_ref, krn_ref, wuk_ref, wuv_ref, *rest, pb, t_new):
    c_refs = rest[:pb]
    kr_refs = rest[pb:2 * pb]
    o_ref, m_ref, l_ref, acc_ref, ql_ref, cpad_ref, krt_ref = rest[2 * pb:]
    step = pl.program_id(1)
    lane = lax.broadcasted_iota(jnp.int32, (HEAD_PAD, LANES), 1)
    qr = qr_ref[0]

    @pl.when(step == 0)
    def _():
        ql = _dot(qn_ref[0], wuk_ref[...]).astype(BF16)
        ql_ref[...] = ql
        cpad_ref[...] = jnp.zeros_like(cpad_ref)
        cpad_ref[0:t_new, :] = cn_ref[0]
        cpad = cpad_ref[...].astype(BF16)
        s = _dot_nt(ql, cpad) + _dot_nt(qr, krn_ref[0])
        parts = [jnp.where(lane <= i, s[i * HEAD_PAD:(i + 1) * HEAD_PAD, :], NEG_INF)
                 for i in range(t_new)]
        s = jnp.concatenate(parts, axis=0)
        m = jnp.max(s, axis=1, keepdims=True)
        p = jnp.exp(s - m)
        m_ref[...] = m
        l_ref[...] = jnp.sum(p, axis=1, keepdims=True)
        acc_ref[...] = _dot(p.astype(BF16), cpad)
        krt_ref[...] = jnp.zeros_like(krt_ref)

    ql = ql_ref[...]
    s_parts, v_parts = [], []
    for r in range(pb):
        cb = c_refs[r][0, 0].astype(BF16)
        krt_ref[0:MLA_ROPE, :] = kr_refs[r][0, 0].astype(BF16)
        s_parts.append(_dot_nt(ql, cb) + _dot(qr, krt_ref[...]))
        v_parts.append(cb)
    m, l, acc = _online_step((m_ref[...], l_ref[...], acc_ref[...]), s_parts, v_parts, False)
    m_ref[...] = m
    l_ref[...] = l
    acc_ref[...] = acc

    @pl.when(step == pl.num_programs(1) - 1)
    def _():
        full = _dot((acc / l).astype(BF16), wuv_ref[...])
        o_ref[0] = _head_diag(full, t_new, MLA_V)


def _mla_decode(page_table, qnblk, qrblk, c_new, kr_new, wuk_t, wuv, cache_c, cache_krt):
    bd, n_pages = page_table.shape
    t_new = c_new.shape[1]
    rows = t_new * HEAD_PAD
    pb = _pages_per_step(n_pages, MLA_PAGES_PER_STEP)

    def page(r):
        return lambda b, s, pt: pt[b, s * pb + r]

    c_specs = [pl.BlockSpec((1, 1, PAGE_SIZE, MLA_KV_LORA),
                            (lambda b, s, pt, f=page(r): (0, f(b, s, pt), 0, 0)))
               for r in range(pb)]
    kr_specs = [pl.BlockSpec((1, 1, MLA_ROPE, PAGE_SIZE),
                             (lambda b, s, pt, f=page(r): (0, f(b, s, pt), 0, 0)))
                for r in range(pb)]
    per_b = lambda shape: pl.BlockSpec((1,) + shape, lambda b, s, pt: (b, 0, 0))
    full = lambda a: pl.BlockSpec(a.shape, lambda b, s, pt: (0,) * a.ndim)
    wide = MLA_HEADS * MLA_V
    grid_spec = pltpu.PrefetchScalarGridSpec(
        num_scalar_prefetch=1,
        grid=(bd, n_pages // pb),
        in_specs=[per_b((rows, MLA_HEADS * MLA_NOPE)), per_b((rows, LANES)),
                  per_b((t_new, MLA_KV_LORA)), per_b((PAGE_SIZE, LANES)),
                  full(wuk_t), full(wuv)] + c_specs + kr_specs,
        out_specs=per_b((t_new, wide)),
        scratch_shapes=[pltpu.VMEM((rows, 1), F32), pltpu.VMEM((rows, 1), F32),
                        pltpu.VMEM((rows, MLA_KV_LORA), F32),
                        pltpu.VMEM((rows, MLA_KV_LORA), BF16),
                        pltpu.VMEM((PAGE_SIZE, MLA_KV_LORA), F32),
                        pltpu.VMEM((LANES, PAGE_SIZE), BF16)],
    )
    return pl.pallas_call(
        functools.partial(_mla_dec_kernel, pb=pb, t_new=t_new),
        grid_spec=grid_spec,
        out_shape=jax.ShapeDtypeStruct((bd, t_new, wide), F32),
        compiler_params=_params("parallel", "arbitrary"),
        name="mla_decode",
    )(page_table, qnblk, qrblk, c_new, kr_new, wuk_t, wuv,
      *([cache_c] * pb), *([cache_krt] * pb))


def _rope_tables(pos):
    half = MLA_ROPE // 2
    inv = ROPE_THETA ** (-jnp.arange(half, dtype=F32) / half)
    ang = pos.astype(F32)[:, None] * inv
    reps = (LANES // 2) // half
    pad = jnp.zeros((pos.shape[0], LANES // 2), F32)
    cos = jnp.concatenate([jnp.tile(jnp.cos(ang), (1, reps)), pad], axis=1)
    sin = jnp.concatenate([jnp.tile(jnp.sin(ang), (1, reps)), pad], axis=1)
    return cos, sin


def _rot_cols(w):
    half = MLA_ROPE // 2
    return jnp.concatenate([-w[..., half:], w[..., :half]], axis=-1)


def _pair_rope_cols(w):
    r = w.shape[0]
    w = w.reshape(r, PAIRS, 2 * MLA_ROPE)
    return jnp.pad(w, ((0, 0), (0, 0), (0, LANES - 2 * MLA_ROPE))).reshape(r, PAIRS * LANES)


def _block_diag_rows(q, width):
    bd, t, n = q.shape
    rowh = np.arange(t * HEAD_PAD) % HEAD_PAD
    mask = jnp.asarray(rowh[:, None] == (np.arange(n) // width)[None, :])
    rep = jnp.repeat(q, HEAD_PAD, axis=1)
    return jnp.where(mask[None], rep, jnp.zeros_like(rep))


def kernel(x_prompt, x_sample, cache_fox_k, cache_fox_v, cache_fox_logf, cache_mla_ckv,
           cache_mla_krope, cache_mem_k, cache_mem_v, page_table, mem_prompt, norm_gains,
           ffn_w_up, ffn_w_down, fox_w_in, fox_b_f, mla_w_in, mla_q_norm, mla_w_q_b,
           mla_kv_norm, mla_w_kv_b, mem_norm, w_mem_kv, w_out, final_norm):
    b, s, d = x_prompt.shape
    bd, t_new, _ = x_sample.shape
    n_pages = page_table.shape[1]
    mem_len = mem_prompt.shape[1]
    depth = norm_gains.shape[0]
    xp = x_prompt.reshape(b * s, d)
    xs = x_sample.reshape(bd * t_new, d)

    w_up = ffn_w_up.astype(BF16)
    w_down = ffn_w_down.astype(BF16)
    w_out_b = w_out.astype(BF16)
    w_mem_b = w_mem_kv.astype(BF16)
    gains = norm_gains.reshape(depth, 3, 1, d)
    mem2d = mem_prompt.reshape(b * mem_len, d)

    outs = {}
    for layer in range(depth):
        j = layer // 2
        xp = _ffn_half(xp, gains[layer, 0], w_up, w_down, layer, 0)
        xs = _ffn_half(xs, gains[layer, 0], w_up, w_down, layer, 0)

        mkv = _norm_proj(mem2d, mem_norm[layer].reshape(1, d), w_mem_b[layer])
        mk = mkv[:, :X_WIDTH].reshape(b, mem_len, X_WIDTH)
        mv = mkv[:, X_WIDTH:].reshape(b, mem_len, X_WIDTH)
        outs.setdefault("mk", []).append(mk.reshape(b, mem_len, X_HEADS, X_HEAD_DIM))
        outs.setdefault("mv", []).append(mv.reshape(b, mem_len, X_HEADS, X_HEAD_DIM))
        mk_s = cache_mem_k[layer].reshape(bd, mem_len, X_WIDTH)
        mv_s = cache_mem_v[layer].reshape(bd, mem_len, X_WIDTH)

        if layer % 2 == 0:
            w_in = fox_w_in[j]
            wqkv = w_in[:, :3 * FOX_WIDTH].astype(BF16)
            wf = jnp.pad(w_in[:, 3 * FOX_WIDTH:3 * FOX_WIDTH + FOX_HEADS],
                         ((0, 0), (0, LANES - FOX_HEADS))).astype(BF16)
            bf = jnp.pad(fox_b_f[j], (0, LANES - FOX_HEADS)).reshape(1, LANES)
            wx = w_in[:, 3 * FOX_WIDTH + FOX_HEADS:].astype(BF16)
            g1 = gains[layer, 1]

            wqk = w_in[:, :2 * FOX_WIDTH].astype(BF16)
            wkvt = jnp.transpose(w_in[:, FOX_WIDTH:3 * FOX_WIDTH]).astype(BF16)
            q, kb, kt, vt, lf, xq_p = _fox_proj_t(
                xp, g1, wqk, wkvt, wf, bf, wx, FOX_HEAD_DIM ** -0.5 * LOG2E, b, s)
            c2 = _fox_cumsum(lf.reshape(b, s, LANES))
            mix_p = _flash([q], [kb], vt, b, s, c2)
            heads_t = (0, 1, 4, 2, 3)
            outs["fk_p"] = jnp.transpose(kt.reshape(1, b, FOX_HEADS, FOX_HEAD_DIM, s), heads_t)
            outs["fv_p"] = jnp.transpose(vt.reshape(1, b, FOX_HEADS, FOX_HEAD_DIM, s), heads_t)
            outs["fl_p"] = lf[:, :FOX_HEADS].reshape(1, b, s, FOX_HEADS)

            q, k, v, lf, xq_s = _fox_proj(xs, g1, wqkv, wf, bf, wx, FOX_HEAD_DIM ** -0.5)
            outs["fk_s"] = k.reshape(1, bd, t_new, FOX_HEADS, FOX_HEAD_DIM)
            outs["fv_s"] = v.reshape(1, bd, t_new, FOX_HEADS, FOX_HEAD_DIM)
            outs["fl_s"] = lf[:, :FOX_HEADS].reshape(1, bd, t_new, FOX_HEADS)
            qblk = _block_diag_rows(q.reshape(bd, t_new, FOX_WIDTH), FOX_HEAD_DIM)
            lft = jnp.transpose(lf.reshape(bd, t_new, LANES)[:, :, :HEAD_PAD], (0, 2, 1))
            lft = jnp.pad(lft, ((0, 0), (0, 0), (0, LANES - t_new)))
            cache_kt = jnp.transpose(cache_fox_k, (0, 1, 3, 4, 2))
            cache_vt = jnp.transpose(cache_fox_v, (0, 1, 3, 4, 2))
            cache_lf = jnp.pad(jnp.transpose(cache_fox_logf[j], (0, 2, 1)),
                               ((0, 0), (0, HEAD_PAD - FOX_HEADS), (0, 0)))
            mix_s = _fox_decode(page_table, qblk, k.reshape(bd, t_new, FOX_WIDTH),
                                v.reshape(bd, t_new, FOX_WIDTH), lft,
                                cache_kt[j:j + 1], cache_vt[j:j + 1], cache_lf)
            mix_s = mix_s.reshape(bd * t_new, FOX_WIDTH).astype(BF16)
        else:
            w_in = mla_w_in[j]
            o1 = MLA_Q_LORA
            o2 = o1 + MLA_KV_LORA
            o3 = o2 + MLA_ROPE
            wkr = w_in[:, o2:o3]
            lane_pad = ((0, 0), (0, LANES - MLA_ROPE))
            wqb = mla_w_q_b[j]
            wkvb = mla_w_kv_b[j]
            weights = [
                w_in[:, :o1].astype(BF16),
                w_in[:, o1:o2].astype(BF16),
                jnp.pad(wkr, lane_pad).astype(BF16),
                jnp.pad(_rot_cols(wkr), lane_pad).astype(BF16),
                w_in[:, o3:].astype(BF16),
                mla_q_norm[j].reshape(1, MLA_Q_LORA),
                wqb[:, :, :MLA_NOPE].reshape(MLA_Q_LORA, MLA_HEADS * MLA_NOPE).astype(BF16),
                _pair_rope_cols(wqb[:, :, MLA_NOPE:]).astype(BF16),
                _pair_rope_cols(_rot_cols(wqb[:, :, MLA_NOPE:])).astype(BF16),
                mla_kv_norm[j].reshape(1, MLA_KV_LORA),
            ]
            wuk = wkvb[:, :, :MLA_NOPE].reshape(MLA_KV_LORA, MLA_HEADS * MLA_NOPE).astype(BF16)
            wuv = wkvb[:, :, MLA_NOPE:].reshape(MLA_KV_LORA, MLA_HEADS * MLA_V).astype(BF16)
            wuv_t = jnp.transpose(wkvb[:, :, MLA_NOPE:], (1, 2, 0)).reshape(
                MLA_HEADS * MLA_V, MLA_KV_LORA).astype(BF16)
            mla_scale = (MLA_NOPE + MLA_ROPE) ** -0.5
            g1 = gains[layer, 1]
            cos_p, sin_p = _rope_tables(jnp.arange(s, dtype=jnp.int32))
            pos_s = n_pages * PAGE_SIZE + jnp.arange(t_new, dtype=jnp.int32)
            cos_s, sin_s = _rope_tables(jnp.tile(pos_s, bd))

            qn, qr, c, kr, xq_p, kn, vt = _mla_proj(
                xp, g1, weights, cos_p, sin_p, mla_scale * LOG2E, (wuk, wuv_t), b, s)
            kr2 = jnp.concatenate(
                [kr, kr, jnp.zeros((b * s, LANES - 2 * MLA_ROPE), F32)], axis=1).astype(BF16)
            kr2 = jnp.tile(kr2, (1, PAIRS))
            mix_p = _flash([qn, qr], [kn, kr2], vt, b, s)
            outs["mc_p"] = c.reshape(1, b, s, MLA_KV_LORA)
            outs["mr_p"] = kr.reshape(1, b, s, MLA_ROPE)

            qn, qr, c, kr, xq_s = _mla_proj(xs, g1, weights, cos_s, sin_s, mla_scale)
            outs["mc_s"] = c.reshape(1, bd, t_new, MLA_KV_LORA)
            outs["mr_s"] = kr.reshape(1, bd, t_new, MLA_ROPE)
            qnblk = _block_diag_rows(qn.reshape(bd, t_new, MLA_HEADS * MLA_NOPE), MLA_NOPE)
            qr4 = qr.reshape(bd, t_new, PAIRS, LANES)[..., :2 * MLA_ROPE]
            qr4 = qr4.reshape(bd, t_new, MLA_HEADS, MLA_ROPE)
            qr4 = jnp.pad(qr4, ((0, 0), (0, 0), (0, HEAD_PAD - MLA_HEADS), (0, LANES - MLA_ROPE)))
            qrblk = qr4.reshape(bd, t_new * HEAD_PAD, LANES)
            kr_new = jnp.pad(kr.reshape(bd, t_new, MLA_ROPE).astype(BF16),
                             ((0, 0), (0, PAGE_SIZE - t_new), (0, LANES - MLA_ROPE)))
            wuk_t = jnp.transpose(wkvb[:, :, :MLA_NOPE], (1, 2, 0)).reshape(
                MLA_HEADS * MLA_NOPE, MLA_KV_LORA).astype(BF16)
            cache_krt = jnp.transpose(cache_mla_krope, (0, 1, 3, 2))
            mix_s = _mla_decode(page_table, qnblk, qrblk, c.reshape(bd, t_new, MLA_KV_LORA),
                                kr_new, wuk_t, wuv, cache_mla_ckv[j:j + 1],
                                cache_krt[j:j + 1])
            mix_s = mix_s.reshape(bd * t_new, MLA_HEADS * MLA_V).astype(BF16)

        cross_p = _cross(xq_p.reshape(b, s, X_WIDTH), mk, mv).reshape(b * s, X_WIDTH)
        cross_s = _cross(xq_s.reshape(bd, t_new, X_WIDTH), mk_s, mv_s).reshape(bd * t_new, X_WIDTH)
        xp = _merge(xp, mix_p, cross_p, w_out_b[layer])
        xs = _merge(xs, mix_s, cross_s, w_out_b[layer])
        xp = _ffn_half(xp, gains[layer, 2], w_up, w_down, layer, 1)
        xs = _ffn_half(xs, gains[layer, 2], w_up, w_down, layer, 1)

    fin = final_norm.reshape(1, d)
    y_prompt = _final_norm(xp, fin).reshape(b, s, d)
    y_sample = _final_norm(xs, fin).reshape(bd, t_new, d)
    return (y_prompt, y_sample,
            outs["fk_p"], outs["fv_p"], outs["fl_p"],
            outs["fk_s"], outs["fv_s"], outs["fl_s"],
            outs["mc_p"], outs["mr_p"], outs["mc_s"], outs["mr_s"],
            jnp.stack(outs["mk"]), jnp.stack(outs["mv"]))
```

```python
import functools

import numpy as np
import jax
import jax.numpy as jnp
from jax import lax
from jax.experimental import pallas as pl
from jax.experimental.pallas import tpu as pltpu

D_MODEL = 1024
PAGE_SIZE = 128
X_HEADS = 4
X_HEAD_DIM = 64
X_WIDTH = X_HEADS * X_HEAD_DIM
FOX_HEADS = 12
FOX_HEAD_DIM = 64
FOX_WIDTH = FOX_HEADS * FOX_HEAD_DIM
MLA_HEADS = 12
MLA_NOPE = 64
MLA_ROPE = 32
MLA_V = 64
MLA_Q_LORA = 384
MLA_KV_LORA = 256
FFN_HIDDEN = 2816
ROPE_THETA = 10000.0
RMS_EPS = 1e-6

LANES = 128
HEAD_PAD = 16
PAIRS = FOX_HEADS // 2
VMEM_LIMIT = 56 * 1024 * 1024

F32 = jnp.float32
BF16 = jnp.bfloat16
NT_DIMS = (((1,), (1,)), ((), ()))
NEG_INF = float("-inf")
LOG2E = 1.4426950408889634


def _params(*sem):
    return pltpu.CompilerParams(dimension_semantics=sem, vmem_limit_bytes=VMEM_LIMIT)


def _rms(x, g):
    ms = jnp.mean(x * x, axis=-1, keepdims=True)
    return x * lax.rsqrt(ms + RMS_EPS) * g


def _dot(a, b):
    return jnp.dot(a, b, preferred_element_type=F32)


def _dot_nt(a, b):
    return lax.dot_general(a, b, NT_DIMS, preferred_element_type=F32)


def _dot_exact(a, b):
    return jnp.dot(a, b, preferred_element_type=F32, precision=lax.Precision.HIGHEST)


def _row_tile(t, pref):
    return pref if t % pref == 0 else t


def _ffn_kernel(*refs, merge, final):
    x_ref = refs[0]
    pos = 1
    if merge:
        mix_ref, cross_ref, wo_ref = refs[1:4]
        pos = 4
    g_ref, wg_ref, wu_ref, wd_ref = refs[pos:pos + 4]
    pos += 4
    if final:
        gf_ref = refs[pos]
        pos += 1
    o_ref, xn_ref, acc_ref = refs[pos:pos + 3]
    res_ref = refs[pos + 3] if merge else x_ref
    j = pl.program_id(1)

    @pl.when(j == 0)
    def _():
        x = x_ref[...]
        if merge:
            nm = mix_ref.shape[1]
            x = x + _dot(mix_ref[...], wo_ref[:nm, :]) + _dot(cross_ref[...], wo_ref[nm:, :])
            res_ref[...] = x
        xn_ref[...] = _rms(x, g_ref[...]).astype(BF16)
        acc_ref[...] = jnp.zeros_like(acc_ref)

    xn = xn_ref[...]
    gate = _dot(xn, wg_ref[...])
    up = _dot(xn, wu_ref[...])
    h = (gate / (1.0 + jnp.exp(-gate)) * up).astype(BF16)
    acc_ref[...] += _dot(h, wd_ref[...])

    @pl.when(j == pl.num_programs(1) - 1)
    def _():
        y = res_ref[...] + 0.5 * acc_ref[...]
        o_ref[...] = _rms(y, gf_ref[...]) if final else y


def _ffn_half(x, g, w_up, w_down, layer, idx, merge=None, final_g=None):
    t = x.shape[0]
    tm = _row_tile(t, 512)
    th = FFN_HIDDEN // 2
    nj = FFN_HIDDEN // th
    row = lambda n: pl.BlockSpec((tm, n), lambda m, j: (m, 0))
    full = lambda a: pl.BlockSpec(a.shape, lambda m, j: (0,) * a.ndim)
    in_specs = [row(D_MODEL)]
    args = [x]
    scratch = [pltpu.VMEM((tm, D_MODEL), BF16), pltpu.VMEM((tm, D_MODEL), F32)]
    if merge is not None:
        mix, cross, w_out = merge
        in_specs += [row(mix.shape[1]), row(cross.shape[1]), full(w_out)]
        args += [mix, cross, w_out]
        scratch.append(pltpu.VMEM((tm, D_MODEL), F32))
    in_specs += [
        full(g),
        pl.BlockSpec((None, None, D_MODEL, th), lambda m, j: (layer, idx, 0, j)),
        pl.BlockSpec((None, None, D_MODEL, th), lambda m, j: (layer, idx, 0, j + nj)),
        pl.BlockSpec((None, None, th, D_MODEL), lambda m, j: (layer, idx, j, 0)),
    ]
    args += [g, w_up, w_up, w_down]
    if final_g is not None:
        in_specs.append(full(final_g))
        args.append(final_g)
    return pl.pallas_call(
        functools.partial(_ffn_kernel, merge=merge is not None, final=final_g is not None),
        grid=(t // tm, nj),
        in_specs=in_specs,
        out_specs=row(D_MODEL),
        out_shape=jax.ShapeDtypeStruct((t, D_MODEL), F32),
        scratch_shapes=scratch,
        compiler_params=_params("parallel", "arbitrary"),
        name="ffn_half",
    )(*args)


def _log_sigmoid(f):
    return jnp.minimum(f, 0.0) - jnp.log(1.0 + jnp.exp(-jnp.abs(f)))


def _fox_proj_kernel(x_ref, g_ref, wqkv_ref, wf_ref, bf_ref, wx_ref,
                     q_ref, k_ref, v_ref, lf_ref, xq_ref, *, qscale):
    hn = _rms(x_ref[...], g_ref[...]).astype(BF16)
    qkv = _dot(hn, wqkv_ref[...])
    q_ref[...] = (qkv[:, :FOX_WIDTH] * qscale).astype(BF16)
    k_ref[...] = qkv[:, FOX_WIDTH:2 * FOX_WIDTH]
    v_ref[...] = qkv[:, 2 * FOX_WIDTH:]
    lf_ref[...] = _log_sigmoid(_dot(hn, wf_ref[...]) + bf_ref[...])
    xq_ref[...] = (_dot(hn, wx_ref[...]) * (X_HEAD_DIM ** -0.5)).astype(BF16)


def _fox_proj_t_kernel(x_ref, g_ref, wqk_ref, wkvt_ref, wf_ref, bf_ref, wx_ref, tri_ref,
                       q_ref, k_ref, kt_ref, vt_ref, lf_ref, c_ref, xq_ref, carry_ref,
                       *, qscale, tiles_per_seq):
    hn = _rms(x_ref[...], g_ref[...]).astype(BF16)
    qk = _dot(hn, wqk_ref[...])
    q_ref[...] = (qk[:, :FOX_WIDTH] * qscale).astype(BF16)
    k_ref[...] = qk[:, FOX_WIDTH:].astype(BF16)
    kvt = _dot_nt(wkvt_ref[...], hn)
    kt_ref[0] = kvt[:FOX_WIDTH, :]
    vt_ref[0] = kvt[FOX_WIDTH:, :]
    lf = _log_sigmoid(_dot(hn, wf_ref[...]) + bf_ref[...])
    lf_ref[...] = lf

    @pl.when(pl.program_id(0) % tiles_per_seq == 0)
    def _():
        carry_ref[...] = jnp.zeros_like(carry_ref)

    c = _dot_exact(tri_ref[...], lf) + carry_ref[...]
    c_ref[...] = c * LOG2E
    carry_ref[...] = c[-1:, :]
    xq_ref[...] = (_dot(hn, wx_ref[...]) * (X_HEAD_DIM ** -0.5)).astype(BF16)


def _fox_proj(x, g, wqkv, wf, bf, wx, qscale):
    t = x.shape[0]
    tm = _row_tile(t, 512)
    row = lambda n: pl.BlockSpec((tm, n), lambda m: (m, 0))
    full = lambda a: pl.BlockSpec(a.shape, lambda m: (0,) * a.ndim)
    return pl.pallas_call(
        functools.partial(_fox_proj_kernel, qscale=qscale),
        grid=(t // tm,),
        in_specs=[row(D_MODEL), full(g), full(wqkv), full(wf), full(bf), full(wx)],
        out_specs=[row(FOX_WIDTH), row(FOX_WIDTH), row(FOX_WIDTH), row(LANES), row(X_WIDTH)],
        out_shape=[
            jax.ShapeDtypeStruct((t, FOX_WIDTH), BF16),
            jax.ShapeDtypeStruct((t, FOX_WIDTH), F32),
            jax.ShapeDtypeStruct((t, FOX_WIDTH), F32),
            jax.ShapeDtypeStruct((t, LANES), F32),
            jax.ShapeDtypeStruct((t, X_WIDTH), BF16),
        ],
        compiler_params=_params("parallel"),
        name="fox_proj",
    )(x, g, wqkv, wf, bf, wx)


def _fox_proj_t(x, g, wqk, wkvt, wf, bf, wx, qscale, b, s):
    t = x.shape[0]
    tm = _row_tile(s, 512)
    ns = s // tm
    tri = jnp.asarray(np.tril(np.ones((tm, tm), np.float32)))
    row = lambda n: pl.BlockSpec((tm, n), lambda m: (m, 0))
    full = lambda a: pl.BlockSpec(a.shape, lambda m: (0,) * a.ndim)
    tspec = pl.BlockSpec((1, FOX_WIDTH, tm), lambda m: (m // ns, 0, m % ns))
    return pl.pallas_call(
        functools.partial(_fox_proj_t_kernel, qscale=qscale, tiles_per_seq=ns),
        grid=(t // tm,),
        in_specs=[row(D_MODEL), full(g), full(wqk), full(wkvt), full(wf), full(bf), full(wx),
                  full(tri)],
        out_specs=[row(FOX_WIDTH), row(FOX_WIDTH), tspec, tspec, row(LANES), row(LANES),
                   row(X_WIDTH)],
        out_shape=[
            jax.ShapeDtypeStruct((t, FOX_WIDTH), BF16),
            jax.ShapeDtypeStruct((t, FOX_WIDTH), BF16),
            jax.ShapeDtypeStruct((b, FOX_WIDTH, s), F32),
            jax.ShapeDtypeStruct((b, FOX_WIDTH, s), F32),
            jax.ShapeDtypeStruct((t, LANES), F32),
            jax.ShapeDtypeStruct((t, LANES), F32),
            jax.ShapeDtypeStruct((t, X_WIDTH), BF16),
        ],
        scratch_shapes=[pltpu.VMEM((1, LANES), F32)],
        compiler_params=_params("arbitrary"),
        name="fox_proj_t",
    )(x, g, wqk, wkvt, wf, bf, wx, tri)


BIAS_TERMS = 3


def _split3(x):
    hi = x.astype(BF16)
    r1 = x - hi.astype(F32)
    mid = r1.astype(BF16)
    lo = (r1 - mid.astype(F32)).astype(BF16)
    return hi, mid, lo


def _place(parts, src_lane, dst_lane0, sign):
    r = lax.broadcasted_iota(jnp.int32, (LANES, LANES), 0)
    c = lax.broadcasted_iota(jnp.int32, (LANES, LANES), 1)
    out = None
    for i, part in enumerate(parts):
        sel = jnp.where((r == src_lane) & (c == dst_lane0 + i), sign, 0.0).astype(BF16)
        term = _dot(part, sel)
        out = term if out is None else out + term
    return out


def _ones_lanes(shape, lo, n):
    lane = lax.broadcasted_iota(jnp.int32, shape, 1)
    return jnp.where((lane >= lo) & (lane < lo + n), 1.0, 0.0)


def _group8(x, op):
    parts = [x[r * 8:(r + 1) * 8, :] for r in range(x.shape[0] // 8)]
    while len(parts) > 1:
        parts = [op(parts[i], parts[i + 1]) for i in range(0, len(parts), 2)]
    return parts[0]


def _flash_kernel(*refs, tile, n_q, n_k, use_c):
    q_refs = refs[:n_q]
    k_refs = refs[n_q:n_q + n_k]
    vt_ref = refs[n_q + n_k]
    pos = n_q + n_k + 1
    if use_c:
        c_ref = refs[pos]
        pos += 1
    o_ref, kb_ref, vtb_ref, s_ref, p_ref, acc_ref = refs[pos:pos + 6]
    pair = pl.program_id(1)
    seq = vt_ref.shape[2]
    n_tiles = seq // tile
    chunk = 64
    nb = BIAS_TERMS

    for n, k_ref in enumerate(k_refs):
        kb_ref[:, n * LANES:(n + 1) * LANES] = k_ref[...].astype(BF16)
    vtb_ref[...] = vt_ref[0].astype(BF16)
    if use_c:
        c_parts = _split3(c_ref[0])
        feat = (_place(c_parts, 2 * pair, 0, -1.0) + _place(c_parts, 2 * pair + 1, nb, -1.0)
                + _ones_lanes((1, LANES), 2 * nb, nb))
        kb_ref[:, n_k * LANES:] = feat.astype(BF16)

    kd_q = n_q * LANES
    lane = lax.broadcasted_iota(jnp.int32, (1, kd_q), 1)
    if n_q == 1:
        own = (lane < FOX_HEAD_DIM, lane >= FOX_HEAD_DIM)
    else:
        own = ((lane < MLA_NOPE) | ((lane >= LANES) & (lane < LANES + MLA_ROPE)),
               ((lane >= MLA_NOPE) & (lane < LANES))
               | ((lane >= LANES + MLA_ROPE) & (lane < LANES + 2 * MLA_ROPE)))

    def q_operands(i):
        rows = slice(i * tile, (i + 1) * tile)
        q = (jnp.concatenate([r[rows, :] for r in q_refs], axis=1) if n_q > 1
             else q_refs[0][rows, :])
        zero = jnp.zeros_like(q)
        qh = [jnp.where(own[h], q, zero) for h in range(2)]
        if use_c:
            parts = [p[rows, :] for p in c_parts]
            for h in range(2):
                feat = (_place(parts, 2 * pair + h, 2 * nb, 1.0)
                        + _ones_lanes((1, LANES), h * nb, nb))
                qh[h] = jnp.concatenate([qh[h], feat.astype(BF16)], axis=1)
        return qh

    krow = lax.broadcasted_iota(jnp.int32, (chunk, tile), 0)
    qcol = lax.broadcasted_iota(jnp.int32, (chunk, tile), 1)

    def scores(slot, qh, j):
        kblk = kb_ref[j * tile:(j + 1) * tile, :]
        for h in range(2):
            s_ref[slot, h] = _dot_nt(kblk, qh[h])

    def values(slot, par, j, alphas):
        vtblk = vtb_ref[:, j * tile:(j + 1) * tile]
        for h in range(2):
            pv = _dot(vtblk, p_ref[slot, h])
            acc_ref[par, h] = pv if alphas is None else acc_ref[par, h] * alphas[h] + pv

    def softmax(slot, stats, diagonal):
        out = []
        alphas = []
        for h in range(2):
            mx = None
            for c in range(tile // chunk):
                rows = slice(c * chunk, (c + 1) * chunk)
                blk = s_ref[slot, h, rows, :]
                if diagonal:
                    blk = jnp.where(krow + c * chunk > qcol, NEG_INF, blk)
                    s_ref[slot, h, rows, :] = blk
                cm = _group8(blk, jnp.maximum)
                mx = cm if mx is None else jnp.maximum(mx, cm)
            m_new = jnp.max(mx, axis=0, keepdims=True)
            if stats is not None:
                m_old = stats[h][0]
                m_new = jnp.maximum(m_old, m_new)
                alphas.append(jnp.exp2(m_old - m_new))
            ls = None
            for c in range(tile // chunk):
                rows = slice(c * chunk, (c + 1) * chunk)
                p = jnp.exp2(s_ref[slot, h, rows, :] - m_new)
                ps = _group8(p, jnp.add)
                ls = ps if ls is None else ls + ps
                p_ref[slot, h, rows, :] = p.astype(BF16)
            l = jnp.sum(ls, axis=0, keepdims=True)
            if stats is not None:
                l = stats[h][1] * alphas[-1] + l
            out.append((m_new, l))
        return out, (None if stats is None else alphas)

    def finish(i, stats):
        par = i % 2
        row = lax.broadcasted_iota(jnp.int32, (LANES, 1), 0)
        o_t = jnp.where(row < FOX_HEAD_DIM, acc_ref[par, 0] / stats[0][1],
                        acc_ref[par, 1] / stats[1][1])
        o_ref[i * tile:(i + 1) * tile, :] = o_t.T.astype(o_ref.dtype)

    blocks = [(i, j) for i in range(n_tiles) for j in range(i + 1)]
    qh_of = {0: q_operands(0)}
    scores(0, qh_of[0], 0)
    stats = None
    pending = None
    for n, (i, j) in enumerate(blocks):
        if n + 1 < len(blocks):
            i2, j2 = blocks[n + 1]
            if i2 not in qh_of:
                qh_of[i2] = q_operands(i2)
            scores((n + 1) % 2, qh_of[i2], j2)
        if pending is not None:
            slot_p, i_p, j_p, alphas_p, stats_p = pending
            values(slot_p, i_p % 2, j_p, alphas_p)
            if j_p == i_p:
                finish(i_p, stats_p)
        stats, alphas = softmax(n % 2, None if j == 0 else stats, j == i)
        pending = (n % 2, i, j, alphas, stats)
    slot_p, i_p, j_p, alphas_p, stats_p = pending
    values(slot_p, i_p % 2, j_p, alphas_p)
    finish(i_p, stats_p)


FLASH_TILE = 512


def _flash(qs, ks, vt, b, s, c=None):
    tile = _row_tile(s, FLASH_TILE)
    use_c = c is not None
    kd = LANES * (len(ks) + (1 if use_c else 0))
    rows = pl.BlockSpec((s, LANES), lambda i, p: (i, p))
    in_specs = [rows] * (len(qs) + len(ks))
    in_specs.append(pl.BlockSpec((1, LANES, s), lambda i, p: (i, p, 0)))
    args = list(qs) + list(ks) + [vt]
    if use_c:
        in_specs.append(pl.BlockSpec((1, s, LANES), lambda i, p: (i, 0, 0)))
        args.append(c)
    kern = functools.partial(_flash_kernel, tile=tile, n_q=len(qs), n_k=len(ks), use_c=use_c)
    return pl.pallas_call(
        kern,
        grid=(b, PAIRS),
        in_specs=in_specs,
        out_specs=rows,
        out_shape=jax.ShapeDtypeStruct((b * s, PAIRS * LANES), BF16),
        scratch_shapes=[pltpu.VMEM((s, kd), BF16), pltpu.VMEM((LANES, s), BF16),
                        pltpu.VMEM((2, 2, tile, tile), F32),
                        pltpu.VMEM((2, 2, tile, tile), BF16),
                        pltpu.VMEM((2, 2, LANES, tile), F32)],
        compiler_params=_params("parallel", "parallel"),
        name="flash_mla" if len(qs) > 1 else "flash_fox",
    )(*args)


def _cross_kernel(q_ref, k_ref, v_ref, o_ref):
    q = q_ref[0].astype(BF16)
    kb = k_ref[0].astype(BF16)
    vb = v_ref[0].astype(BF16)
    lane = lax.broadcasted_iota(jnp.int32, (1, LANES), 1)
    first = lane < X_HEAD_DIM
    zero = jnp.zeros_like(q)
    outs = []
    for qh in (jnp.where(first, q, zero), jnp.where(first, zero, q)):
        s = _dot_nt(qh, kb)
        m = jnp.max(s, axis=1, keepdims=True)
        p = jnp.exp(s - m)
        l = jnp.sum(p, axis=1, keepdims=True)
        outs.append(_dot(p.astype(BF16), vb) / l)
    o_ref[0] = jnp.where(first, outs[0], outs[1]).astype(o_ref.dtype)


def _cross(xq, mk, mv):
    b, t, _ = xq.shape
    m = mk.shape[1]
    tq = _row_tile(t, 512)
    qspec = pl.BlockSpec((1, tq, LANES), lambda i, p, j: (i, j, p))
    kspec = pl.BlockSpec((1, m, LANES), lambda i, p, j: (i, 0, p))
    return pl.pallas_call(
        _cross_kernel,
        grid=(b, X_WIDTH // LANES, t // tq),
        in_specs=[qspec, kspec, kspec],
        out_specs=qspec,
        out_shape=jax.ShapeDtypeStruct((b, t, X_WIDTH), BF16),
        compiler_params=_params("parallel", "parallel", "arbitrary"),
        name="cross_attn",
    )(xq, mk, mv)


def _norm_proj_kernel(x_ref, g_ref, w_ref, o_ref):
    o_ref[...] = _dot(_rms(x_ref[...], g_ref[...]).astype(BF16), w_ref[...])


def _norm_proj(x, g, w):
    t = x.shape[0]
    n = w.shape[1]
    tm = _row_tile(t, 512)
    return pl.pallas_call(
        _norm_proj_kernel,
        grid=(t // tm,),
        in_specs=[pl.BlockSpec((tm, D_MODEL), lambda m: (m, 0)),
                  pl.BlockSpec((1, D_MODEL), lambda m: (0, 0)),
                  pl.BlockSpec(w.shape, lambda m: (0, 0))],
        out_specs=pl.BlockSpec((tm, n), lambda m: (m, 0)),
        out_shape=jax.ShapeDtypeStruct((t, n), F32),
        compiler_params=_params("parallel"),
        name="norm_proj",
    )(x, g, w)


def _mla_proj_kernel(*refs, qscale, with_kv):
    (x_ref, g_ref, wqa_ref, wkva_ref, wkr_ref, wkrr_ref, wx_ref,
     qg_ref, wqn_ref, wqr_ref, wqrr_ref, kvg_ref) = refs[:12]
    pos = 12
    if with_kv:
        wuk_ref, wuvt_ref = refs[pos:pos + 2]
        pos += 2
    cos_ref, sin_ref = refs[pos:pos + 2]
    qn_ref, qr_ref, c_ref, kr_ref, xq_ref = refs[pos + 2:pos + 7]
    hn = _rms(x_ref[...], g_ref[...]).astype(BF16)
    cos = cos_ref[...]
    sin = sin_ref[...]
    qa = _rms(_dot(hn, wqa_ref[...]), qg_ref[...]).astype(BF16)
    qn_ref[...] = (_dot(qa, wqn_ref[...]) * qscale).astype(BF16)
    qr = _dot(qa, wqr_ref[...])
    qrr = _dot(qa, wqrr_ref[...])
    for p in range(PAIRS):
        sl = slice(p * LANES, (p + 1) * LANES)
        qr_ref[:, sl] = ((qr[:, sl] * cos + qrr[:, sl] * sin) * qscale).astype(BF16)
    c = _rms(_dot(hn, wkva_ref[...]), kvg_ref[...])
    c_ref[...] = c
    if with_kv:
        kn_ref, vt_ref = refs[pos + 7:pos + 9]
        cb = c.astype(BF16)
        kn_ref[...] = _dot(cb, wuk_ref[...]).astype(BF16)
        vt_ref[0] = _dot_nt(wuvt_ref[...], cb).astype(BF16)
    kr = _dot(hn, wkr_ref[...]) * cos + _dot(hn, wkrr_ref[...]) * sin
    kr_ref[...] = kr[:, :MLA_ROPE]
    xq_ref[...] = (_dot(hn, wx_ref[...]) * (X_HEAD_DIM ** -0.5)).astype(BF16)


def _mla_proj(x, g, w, cos, sin, qscale, kv_w=None, b=None, s=None):
    t = x.shape[0]
    with_kv = kv_w is not None
    tm = _row_tile(s if with_kv else t, 512)
    ntab = cos.shape[0] // tm
    row = lambda n: pl.BlockSpec((tm, n), lambda m: (m, 0))
    full = lambda a: pl.BlockSpec(a.shape, lambda m: (0,) * a.ndim)
    tab = pl.BlockSpec((tm, LANES), lambda m: (m % ntab, 0))
    wide = PAIRS * LANES
    w = list(w) + (list(kv_w) if with_kv else [])
    out_specs = [row(wide), row(wide), row(MLA_KV_LORA), row(MLA_ROPE), row(X_WIDTH)]
    out_shape = [
        jax.ShapeDtypeStruct((t, wide), BF16),
        jax.ShapeDtypeStruct((t, wide), BF16),
        jax.ShapeDtypeStruct((t, MLA_KV_LORA), F32),
        jax.ShapeDtypeStruct((t, MLA_ROPE), F32),
        jax.ShapeDtypeStruct((t, X_WIDTH), BF16),
    ]
    if with_kv:
        ns = s // tm
        out_specs += [row(wide), pl.BlockSpec((1, wide, tm), lambda m: (m // ns, 0, m % ns))]
        out_shape += [jax.ShapeDtypeStruct((t, wide), BF16),
                      jax.ShapeDtypeStruct((b, wide, s), BF16)]
    return pl.pallas_call(
        functools.partial(_mla_proj_kernel, qscale=qscale, with_kv=with_kv),
        grid=(t // tm,),
        in_specs=[row(D_MODEL), full(g)] + [full(a) for a in w] + [tab, tab],
        out_specs=out_specs,
        out_shape=out_shape,
        compiler_params=_params("parallel"),
        name="mla_proj_kv" if with_kv else "mla_proj",
    )(x, g, *w, cos, sin)


FOX_PAGES_PER_STEP = 16
MLA_PAGES_PER_STEP = 32
PAGE_GROUPS = 4


def _pages_per_step(n_pages, pref):
    while n_pages % pref:
        pref //= 2
    return pref


def _online_step(state, s_parts, v_parts, v_is_transposed):
    m, l, acc = state
    n = len(s_parts)
    per = n // PAGE_GROUPS if n % PAGE_GROUPS == 0 else n
    partials = []
    for g0 in range(0, n, per):
        s = jnp.concatenate(s_parts[g0:g0 + per], axis=1) if per > 1 else s_parts[g0]
        mg = jnp.max(s, axis=1, keepdims=True)
        p = jnp.exp(s - mg)
        lg = jnp.sum(p, axis=1, keepdims=True)
        ag = None
        for r in range(per):
            pb = p[:, r * LANES:(r + 1) * LANES].astype(BF16)
            vb = v_parts[g0 + r]
            t = _dot_nt(pb, vb) if v_is_transposed else _dot(pb, vb)
            ag = t if ag is None else ag + t
        partials.append((mg, lg, ag))
    m_new = m
    for mg, _, _ in partials:
        m_new = jnp.maximum(m_new, mg)
    alpha = jnp.exp(m - m_new)
    l = l * alpha
    acc = acc * alpha
    for mg, lg, ag in partials:
        w = jnp.exp(mg - m_new)
        l = l + lg * w
        acc = acc + ag * w
    return m_new, l, acc


def _head_diag(acc, t_new, width):
    n = acc.shape[1]
    rowh = lax.broadcasted_iota(jnp.int32, (HEAD_PAD, n), 0)
    laneh = lax.broadcasted_iota(jnp.int32, (HEAD_PAD, n), 1) // width
    orow = lax.broadcasted_iota(jnp.int32, (t_new, n), 0)
    out = jnp.zeros((t_new, n), F32)
    for i in range(t_new):
        blk = acc[i * HEAD_PAD:(i + 1) * HEAD_PAD, :]
        r = jnp.sum(jnp.where(rowh == laneh, blk, 0.0), axis=0, keepdims=True)
        out = jnp.where(orow == i, r, out)
    return out


def _fox_dec_kernel(pt_ref, qb_ref, kn_ref, vn_ref, lft_ref, sufm_ref, *rest, pb, t_new):
    kt_refs = rest[:pb]
    vt_refs = rest[pb:2 * pb]
    lf_refs = rest[2 * pb:3 * pb]
    o_ref, m_ref, l_ref, acc_ref, run_ref, pad_ref = rest[3 * pb:]
    step = pl.program_id(1)
    qb = qb_ref[0]
    lft = lft_ref[0]
    lane = lax.broadcasted_iota(jnp.int32, (HEAD_PAD, LANES), 1)
    cn_cols = []
    run = jnp.zeros((HEAD_PAD, 1), F32)
    for i in range(t_new):
        run = run + lft[:, i:i + 1]
        cn_cols.append(run)
    cn_col = jnp.concatenate(cn_cols, axis=0)

    @pl.when(step == 0)
    def _():
        cn_lanes = jnp.zeros((HEAD_PAD, LANES), F32)
        for i in range(t_new):
            cn_lanes = jnp.where(lane == i, cn_cols[i], cn_lanes)
        pad_ref[...] = jnp.zeros_like(pad_ref)
        pad_ref[0:t_new, :] = kn_ref[0]
        kpad = pad_ref[...].astype(BF16)
        pad_ref[0:t_new, :] = vn_ref[0]
        vpad = pad_ref[...].astype(BF16)
        s = _dot_nt(qb, kpad)
        parts = []
        for i in range(t_new):
            blk = s[i * HEAD_PAD:(i + 1) * HEAD_PAD, :] + (cn_cols[i] - cn_lanes)
            parts.append(jnp.where(lane <= i, blk, NEG_INF))
        s = jnp.concatenate(parts, axis=0)
        m = jnp.max(s, axis=1, keepdims=True)
        p = jnp.exp(s - m)
        m_ref[...] = m
        l_ref[...] = jnp.sum(p, axis=1, keepdims=True)
        acc_ref[...] = _dot(p.astype(BF16), vpad)
        run_ref[...] = jnp.zeros_like(run_ref)

    run = run_ref[...]
    s_parts, v_parts = [], []
    lf_all = jnp.concatenate([lf_refs[r][0] for r in range(pb)], axis=0)
    suf_all = _dot_exact(lf_all, sufm_ref[...])
    tot_all = jnp.sum(lf_all, axis=1, keepdims=True)
    for r in range(pb):
        heads = slice(r * HEAD_PAD, (r + 1) * HEAD_PAD)
        suf = suf_all[heads, :] + run
        run = run + tot_all[heads, :]
        ktb = kt_refs[r][0, 0].reshape(FOX_WIDTH, PAGE_SIZE).astype(BF16)
        s = _dot(qb, ktb)
        bias = jnp.concatenate([suf] * t_new, axis=0) + cn_col
        s_parts.append(s + bias)
        v_parts.append(vt_refs[r][0, 0].reshape(FOX_WIDTH, PAGE_SIZE).astype(BF16))
    run_ref[...] = run
    m, l, acc = _online_step((m_ref[...], l_ref[...], acc_ref[...]), s_parts, v_parts, True)
    m_ref[...] = m
    l_ref[...] = l
    acc_ref[...] = acc

    @pl.when(step == pl.num_programs(1) - 1)
    def _():
        o_ref[0] = _head_diag(acc / l, t_new, FOX_HEAD_DIM)


def _fox_decode(page_table, qblk, k_new, v_new, lft_new, cache_kt, cache_vt, cache_lf):
    bd, n_pages = page_table.shape
    t_new = k_new.shape[1]
    rows = t_new * HEAD_PAD
    pb = _pages_per_step(n_pages, FOX_PAGES_PER_STEP)
    sufm = jnp.asarray(np.tril(np.ones((PAGE_SIZE, PAGE_SIZE), np.float32), -1))

    def page(r):
        return lambda b, s, pt: pt[b, n_pages - 1 - (s * pb + r)]

    kv_specs = [pl.BlockSpec((1, 1, FOX_HEADS, FOX_HEAD_DIM, PAGE_SIZE),
                             (lambda b, s, pt, f=page(r): (0, f(b, s, pt), 0, 0, 0)))
                for r in range(pb)]
    lf_specs = [pl.BlockSpec((1, HEAD_PAD, PAGE_SIZE),
                             (lambda b, s, pt, f=page(r): (f(b, s, pt), 0, 0)))
                for r in range(pb)]
    per_b = lambda shape: pl.BlockSpec((1,) + shape, lambda b, s, pt: (b, 0, 0))
    grid_spec = pltpu.PrefetchScalarGridSpec(
        num_scalar_prefetch=1,
        grid=(bd, n_pages // pb),
        in_specs=[per_b((rows, FOX_WIDTH)), per_b((t_new, FOX_WIDTH)), per_b((t_new, FOX_WIDTH)),
                  per_b((HEAD_PAD, LANES)),
                  pl.BlockSpec((PAGE_SIZE, PAGE_SIZE), lambda b, s, pt: (0, 0))]
        + kv_specs + kv_specs + lf_specs,
        out_specs=per_b((t_new, FOX_WIDTH)),
        scratch_shapes=[pltpu.VMEM((rows, 1), F32), pltpu.VMEM((rows, 1), F32),
                        pltpu.VMEM((rows, FOX_WIDTH), F32), pltpu.VMEM((HEAD_PAD, 1), F32),
                        pltpu.VMEM((PAGE_SIZE, FOX_WIDTH), F32)],
    )
    return pl.pallas_call(
        functools.partial(_fox_dec_kernel, pb=pb, t_new=t_new),
        grid_spec=grid_spec,
        out_shape=jax.ShapeDtypeStruct((bd, t_new, FOX_WIDTH), F32),
        compiler_params=_params("parallel", "arbitrary"),
        name="fox_decode",
    )(page_table, qblk, k_new, v_new, lft_new, sufm,
      *([cache_kt] * pb), *([cache_vt] * pb), *([cache_lf] * pb))


def _mla_dec_kernel(pt_ref, qn_ref, qr_ref, cn_ref, krn_ref, wuk_ref, wuv_ref, *rest, pb, t_new):
    c_refs = rest[:pb]
    kr_refs = rest[pb:2 * pb]
    o_ref, m_ref, l_ref, acc_ref, ql_ref, cpad_ref, krt_ref = rest[2 * pb:]
    step = pl.program_id(1)
    lane = lax.broadcasted_iota(jnp.int32, (HEAD_PAD, LANES), 1)
    qr = qr_ref[0]

    @pl.when(step == 0)
    def _():
        ql = _dot(qn_ref[0], wuk_ref[...]).astype(BF16)
        ql_ref[...] = ql
        cpad_ref[...] = jnp.zeros_like(cpad_ref)
        cpad_ref[0:t_new, :] = cn_ref[0]
        cpad = cpad_ref[...].astype(BF16)
        s = _dot_nt(ql, cpad) + _dot_nt(qr, krn_ref[0])
        parts = [jnp.where(lane <= i, s[i * HEAD_PAD:(i + 1) * HEAD_PAD, :], NEG_INF)
                 for i in range(t_new)]
        s = jnp.concatenate(parts, axis=0)
        m = jnp.max(s, axis=1, keepdims=True)
        p = jnp.exp(s - m)
        m_ref[...] = m
        l_ref[...] = jnp.sum(p, axis=1, keepdims=True)
        acc_ref[...] = _dot(p.astype(BF16), cpad)
        krt_ref[...] = jnp.zeros_like(krt_ref)

    ql = ql_ref[...]
    s_parts, v_parts = [], []
    for r in range(pb):
        cb = c_refs[r][0, 0].astype(BF16)
        krt_ref[0:MLA_ROPE, :] = kr_refs[r][0, 0].astype(BF16)
        s_parts.append(_dot_nt(ql, cb) + _dot(qr, krt_ref[...]))
        v_parts.append(cb)
    m, l, acc = _online_step((m_ref[...], l_ref[...], acc_ref[...]), s_parts, v_parts, False)
    m_ref[...] = m
    l_ref[...] = l
    acc_ref[...] = acc

    @pl.when(step == pl.num_programs(1) - 1)
    def _():
        full = _dot((acc / l).astype(BF16), wuv_ref[...])
        o_ref[0] = _head_diag(full, t_new, MLA_V)


def _mla_decode(page_table, qnblk, qrblk, c_new, kr_new, wuk_t, wuv, cache_c, cache_krt):
    bd, n_pages = page_table.shape
    t_new = c_new.shape[1]
    rows = t_new * HEAD_PAD
    pb = _pages_per_step(n_pages, MLA_PAGES_PER_STEP)

    def page(r):
        return lambda b, s, pt: pt[b, s * pb + r]

    c_specs = [pl.BlockSpec((1, 1, PAGE_SIZE, MLA_KV_LORA),
                            (lambda b, s, pt, f=page(r): (0, f(b, s, pt), 0, 0)))
               for r in range(pb)]
    kr_specs = [pl.BlockSpec((1, 1, MLA_ROPE, PAGE_SIZE),
                             (lambda b, s, pt, f=page(r): (0, f(b, s, pt), 0, 0)))
                for r in range(pb)]
    per_b = lambda shape: pl.BlockSpec((1,) + shape, lambda b, s, pt: (b, 0, 0))
    full = lambda a: pl.BlockSpec(a.shape, lambda b, s, pt: (0,) * a.ndim)
    wide = MLA_HEADS * MLA_V
    grid_spec = pltpu.PrefetchScalarGridSpec(
        num_scalar_prefetch=1,
        grid=(bd, n_pages // pb),
        in_specs=[per_b((rows, MLA_HEADS * MLA_NOPE)), per_b((rows, LANES)),
                  per_b((t_new, MLA_KV_LORA)), per_b((PAGE_SIZE, LANES)),
                  full(wuk_t), full(wuv)] + c_specs + kr_specs,
        out_specs=per_b((t_new, wide)),
        scratch_shapes=[pltpu.VMEM((rows, 1), F32), pltpu.VMEM((rows, 1), F32),
                        pltpu.VMEM((rows, MLA_KV_LORA), F32),
                        pltpu.VMEM((rows, MLA_KV_LORA), BF16),
                        pltpu.VMEM((PAGE_SIZE, MLA_KV_LORA), F32),
                        pltpu.VMEM((LANES, PAGE_SIZE), BF16)],
    )
    return pl.pallas_call(
        functools.partial(_mla_dec_kernel, pb=pb, t_new=t_new),
        grid_spec=grid_spec,
        out_shape=jax.ShapeDtypeStruct((bd, t_new, wide), F32),
        compiler_params=_params("parallel", "arbitrary"),
        name="mla_decode",
    )(page_table, qnblk, qrblk, c_new, kr_new, wuk_t, wuv,
      *([cache_c] * pb), *([cache_krt] * pb))


def _rope_tables(pos):
    half = MLA_ROPE // 2
    inv = ROPE_THETA ** (-jnp.arange(half, dtype=F32) / half)
    ang = pos.astype(F32)[:, None] * inv
    reps = (LANES // 2) // half
    pad = jnp.zeros((pos.shape[0], LANES // 2), F32)
    cos = jnp.concatenate([jnp.tile(jnp.cos(ang), (1, reps)), pad], axis=1)
    sin = jnp.concatenate([jnp.tile(jnp.sin(ang), (1, reps)), pad], axis=1)
    return cos, sin


def _rot_cols(w):
    half = MLA_ROPE // 2
    return jnp.concatenate([-w[..., half:], w[..., :half]], axis=-1)


def _pair_rope_cols(w):
    r = w.shape[0]
    w = w.reshape(r, PAIRS, 2 * MLA_ROPE)
    return jnp.pad(w, ((0, 0), (0, 0), (0, LANES - 2 * MLA_ROPE))).reshape(r, PAIRS * LANES)


def _block_diag_rows(q, width):
    bd, t, n = q.shape
    rowh = np.arange(t * HEAD_PAD) % HEAD_PAD
    mask = jnp.asarray(rowh[:, None] == (np.arange(n) // width)[None, :])
    rep = jnp.repeat(q, HEAD_PAD, axis=1)
    return jnp.where(mask[None], rep, jnp.zeros_like(rep))


def kernel(x_prompt, x_sample, cache_fox_k, cache_fox_v, cache_fox_logf, cache_mla_ckv,
           cache_mla_krope, cache_mem_k, cache_mem_v, page_table, mem_prompt, norm_gains,
           ffn_w_up, ffn_w_down, fox_w_in, fox_b_f, mla_w_in, mla_q_norm, mla_w_q_b,
           mla_kv_norm, mla_w_kv_b, mem_norm, w_mem_kv, w_out, final_norm):
    b, s, d = x_prompt.shape
    bd, t_new, _ = x_sample.shape
    n_pages = page_table.shape[1]
    mem_len = mem_prompt.shape[1]
    depth = norm_gains.shape[0]
    xp = x_prompt.reshape(b * s, d)
    xs = x_sample.reshape(bd * t_new, d)

    w_up = ffn_w_up.astype(BF16)
    w_down = ffn_w_down.astype(BF16)
    w_out_b = w_out.astype(BF16)
    w_mem_b = w_mem_kv.astype(BF16)
    gains = norm_gains.reshape(depth, 3, 1, d)
    mem2d = mem_prompt.reshape(b * mem_len, d)

    outs = {}
    for layer in range(depth):
        j = layer // 2
        xp = _ffn_half(xp, gains[layer, 0], w_up, w_down, layer, 0)
        xs = _ffn_half(xs, gains[layer, 0], w_up, w_down, layer, 0)

        mkv = _norm_proj(mem2d, mem_norm[layer].reshape(1, d), w_mem_b[layer])
        mk = mkv[:, :X_WIDTH].reshape(b, mem_len, X_WIDTH)
        mv = mkv[:, X_WIDTH:].reshape(b, mem_len, X_WIDTH)
        outs.setdefault("mk", []).append(mk.reshape(b, mem_len, X_HEADS, X_HEAD_DIM))
        outs.setdefault("mv", []).append(mv.reshape(b, mem_len, X_HEADS, X_HEAD_DIM))
        mk_s = cache_mem_k[layer].reshape(bd, mem_len, X_WIDTH)
        mv_s = cache_mem_v[layer].reshape(bd, mem_len, X_WIDTH)

        if layer % 2 == 0:
            w_in = fox_w_in[j]
            wqkv = w_in[:, :3 * FOX_WIDTH].astype(BF16)
            wf = jnp.pad(w_in[:, 3 * FOX_WIDTH:3 * FOX_WIDTH + FOX_HEADS],
                         ((0, 0), (0, LANES - FOX_HEADS))).astype(BF16)
            bf = jnp.pad(fox_b_f[j], (0, LANES - FOX_HEADS)).reshape(1, LANES)
            wx = w_in[:, 3 * FOX_WIDTH + FOX_HEADS:].astype(BF16)
            g1 = gains[layer, 1]

            wqk = w_in[:, :2 * FOX_WIDTH].astype(BF16)
            wkvt = jnp.transpose(w_in[:, FOX_WIDTH:3 * FOX_WIDTH]).astype(BF16)
            q, kb, kt, vt, lf, c2, xq_p = _fox_proj_t(
                xp, g1, wqk, wkvt, wf, bf, wx, FOX_HEAD_DIM ** -0.5 * LOG2E, b, s)
            mix_p = _flash([q], [kb], vt, b, s, c2.reshape(b, s, LANES))
            heads_t = (0, 1, 4, 2, 3)
            outs["fk_p"] = jnp.transpose(kt.reshape(1, b, FOX_HEADS, FOX_HEAD_DIM, s), heads_t)
            outs["fv_p"] = jnp.transpose(vt.reshape(1, b, FOX_HEADS, FOX_HEAD_DIM, s), heads_t)
            outs["fl_p"] = lf[:, :FOX_HEADS].reshape(1, b, s, FOX_HEADS)

            q, k, v, lf, xq_s = _fox_proj(xs, g1, wqkv, wf, bf, wx, FOX_HEAD_DIM ** -0.5)
            outs["fk_s"] = k.reshape(1, bd, t_new, FOX_HEADS, FOX_HEAD_DIM)
            outs["fv_s"] = v.reshape(1, bd, t_new, FOX_HEADS, FOX_HEAD_DIM)
            outs["fl_s"] = lf[:, :FOX_HEADS].reshape(1, bd, t_new, FOX_HEADS)
            qblk = _block_diag_rows(q.reshape(bd, t_new, FOX_WIDTH), FOX_HEAD_DIM)
            lft = jnp.transpose(lf.reshape(bd, t_new, LANES)[:, :, :HEAD_PAD], (0, 2, 1))
            lft = jnp.pad(lft, ((0, 0), (0, 0), (0, LANES - t_new)))
            cache_kt = jnp.transpose(cache_fox_k, (0, 1, 3, 4, 2))
            cache_vt = jnp.transpose(cache_fox_v, (0, 1, 3, 4, 2))
            cache_lf = jnp.pad(jnp.transpose(cache_fox_logf[j], (0, 2, 1)),
                               ((0, 0), (0, HEAD_PAD - FOX_HEADS), (0, 0)))
            mix_s = _fox_decode(page_table, qblk, k.reshape(bd, t_new, FOX_WIDTH),
                                v.reshape(bd, t_new, FOX_WIDTH), lft,
                                cache_kt[j:j + 1], cache_vt[j:j + 1], cache_lf)
            mix_s = mix_s.reshape(bd * t_new, FOX_WIDTH).astype(BF16)
        else:
            w_in = mla_w_in[j]
            o1 = MLA_Q_LORA
            o2 = o1 + MLA_KV_LORA
            o3 = o2 + MLA_ROPE
            wkr = w_in[:, o2:o3]
            lane_pad = ((0, 0), (0, LANES - MLA_ROPE))
            wqb = mla_w_q_b[j]
            wkvb = mla_w_kv_b[j]
            weights = [
                w_in[:, :o1].astype(BF16),
                w_in[:, o1:o2].astype(BF16),
                jnp.pad(wkr, lane_pad).astype(BF16),
                jnp.pad(_rot_cols(wkr), lane_pad).astype(BF16),
                w_in[:, o3:].astype(BF16),
                mla_q_norm[j].reshape(1, MLA_Q_LORA),
                wqb[:, :, :MLA_NOPE].reshape(MLA_Q_LORA, MLA_HEADS * MLA_NOPE).astype(BF16),
                _pair_rope_cols(wqb[:, :, MLA_NOPE:]).astype(BF16),
                _pair_rope_cols(_rot_cols(wqb[:, :, MLA_NOPE:])).astype(BF16),
                mla_kv_norm[j].reshape(1, MLA_KV_LORA),
            ]
            wuk = wkvb[:, :, :MLA_NOPE].reshape(MLA_KV_LORA, MLA_HEADS * MLA_NOPE).astype(BF16)
            wuv = wkvb[:, :, MLA_NOPE:].reshape(MLA_KV_LORA, MLA_HEADS * MLA_V).astype(BF16)
            wuv_t = jnp.transpose(wkvb[:, :, MLA_NOPE:], (1, 2, 0)).reshape(
                MLA_HEADS * MLA_V, MLA_KV_LORA).astype(BF16)
            mla_scale = (MLA_NOPE + MLA_ROPE) ** -0.5
            g1 = gains[layer, 1]
            cos_p, sin_p = _rope_tables(jnp.arange(s, dtype=jnp.int32))
            pos_s = n_pages * PAGE_SIZE + jnp.arange(t_new, dtype=jnp.int32)
            cos_s, sin_s = _rope_tables(jnp.tile(pos_s, bd))

            qn, qr, c, kr, xq_p, kn, vt = _mla_proj(
                xp, g1, weights, cos_p, sin_p, mla_scale * LOG2E, (wuk, wuv_t), b, s)
            kr2 = jnp.concatenate(
                [kr, kr, jnp.zeros((b * s, LANES - 2 * MLA_ROPE), F32)], axis=1).astype(BF16)
            kr2 = jnp.tile(kr2, (1, PAIRS))
            mix_p = _flash([qn, qr], [kn, kr2], vt, b, s)
            outs["mc_p"] = c.reshape(1, b, s, MLA_KV_LORA)
            outs["mr_p"] = kr.reshape(1, b, s, MLA_ROPE)

            qn, qr, c, kr, xq_s = _mla_proj(xs, g1, weights, cos_s, sin_s, mla_scale)
            outs["mc_s"] = c.reshape(1, bd, t_new, MLA_KV_LORA)
            outs["mr_s"] = kr.reshape(1, bd, t_new, MLA_ROPE)
            qnblk = _block_diag_rows(qn.reshape(bd, t_new, MLA_HEADS * MLA_NOPE), MLA_NOPE)
            qr4 = qr.reshape(bd, t_new, PAIRS, LANES)[..., :2 * MLA_ROPE]
            qr4 = qr4.reshape(bd, t_new, MLA_HEADS, MLA_ROPE)
            qr4 = jnp.pad(qr4, ((0, 0), (0, 0), (0, HEAD_PAD - MLA_HEADS), (0, LANES - MLA_ROPE)))
            qrblk = qr4.reshape(bd, t_new * HEAD_PAD, LANES)
            kr_new = jnp.pad(kr.reshape(bd, t_new, MLA_ROPE).astype(BF16),
                             ((0, 0), (0, PAGE_SIZE - t_new), (0, LANES - MLA_ROPE)))
            wuk_t = jnp.transpose(wkvb[:, :, :MLA_NOPE], (1, 2, 0)).reshape(
                MLA_HEADS * MLA_NOPE, MLA_KV_LORA).astype(BF16)
            cache_krt = jnp.transpose(cache_mla_krope, (0, 1, 3, 2))
            mix_s = _mla_decode(page_table, qnblk, qrblk, c.reshape(bd, t_new, MLA_KV_LORA),
                                kr_new, wuk_t, wuv, cache_mla_ckv[j:j + 1],
                                cache_krt[j:j + 1])
            mix_s = mix_s.reshape(bd * t_new, MLA_HEADS * MLA_V).astype(BF16)

        cross_p = _cross(xq_p.reshape(b, s, X_WIDTH), mk, mv).reshape(b * s, X_WIDTH)
        cross_s = _cross(xq_s.reshape(bd, t_new, X_WIDTH), mk_s, mv_s).reshape(bd * t_new, X_WIDTH)
        fin = final_norm.reshape(1, d) if layer == depth - 1 else None
        xp = _ffn_half(xp, gains[layer, 2], w_up, w_down, layer, 1,
                       (mix_p, cross_p, w_out_b[layer]), fin)
        xs = _ffn_half(xs, gains[layer, 2], w_up, w_down, layer, 1,
                       (mix_s, cross_s, w_out_b[layer]), fin)

    y_prompt = xp.reshape(b, s, d)
    y_sample = xs.reshape(bd, t_new, d)
    return (y_prompt, y_sample,
            outs["fk_p"], outs["fv_p"], outs["fl_p"],
            outs["fk_s"], outs["fv_s"], outs["fl_s"],
            outs["mc_p"], outs["mr_p"], outs["mc_s"], outs["mr_s"],
            jnp.stack(outs["mk"]), jnp.stack(outs["mv"]))
```

```python
import functools

import numpy as np
import jax
import jax.numpy as jnp
from jax import lax
from jax.experimental import pallas as pl
from jax.experimental.pallas import tpu as pltpu

D_MODEL = 1024
PAGE_SIZE = 128
X_HEADS = 4
X_HEAD_DIM = 64
X_WIDTH = X_HEADS * X_HEAD_DIM
FOX_HEADS = 12
FOX_HEAD_DIM = 64
FOX_WIDTH = FOX_HEADS * FOX_HEAD_DIM
MLA_HEADS = 12
MLA_NOPE = 64
MLA_ROPE = 32
MLA_V = 64
MLA_Q_LORA = 384
MLA_KV_LORA = 256
FFN_HIDDEN = 2816
ROPE_THETA = 10000.0
RMS_EPS = 1e-6

LANES = 128
HEAD_PAD = 16
PAIRS = FOX_HEADS // 2
VMEM_LIMIT = 56 * 1024 * 1024

F32 = jnp.float32
BF16 = jnp.bfloat16
NT_DIMS = (((1,), (1,)), ((), ()))
NEG_INF = float("-inf")
LOG2E = 1.4426950408889634


def _params(*sem):
    return pltpu.CompilerParams(dimension_semantics=sem, vmem_limit_bytes=VMEM_LIMIT)


def _rms(x, g):
    ms = jnp.mean(x * x, axis=-1, keepdims=True)
    return x * lax.rsqrt(ms + RMS_EPS) * g


def _dot(a, b):
    return jnp.dot(a, b, preferred_element_type=F32)


def _dot_nt(a, b):
    return lax.dot_general(a, b, NT_DIMS, preferred_element_type=F32)


def _dot_exact(a, b):
    return jnp.dot(a, b, preferred_element_type=F32, precision=lax.Precision.HIGHEST)


def _row_tile(t, pref):
    return pref if t % pref == 0 else t


def _ffn_kernel(*refs, merge, final):
    x_ref = refs[0]
    pos = 1
    if merge:
        mix_ref, cross_ref, wo_ref = refs[1:4]
        pos = 4
    g_ref, wg_ref, wu_ref, wd_ref = refs[pos:pos + 4]
    pos += 4
    if final:
        gf_ref = refs[pos]
        pos += 1
    o_ref, xn_ref, acc_ref = refs[pos:pos + 3]
    res_ref = refs[pos + 3] if merge else x_ref
    j = pl.program_id(1)

    @pl.when(j == 0)
    def _():
        x = x_ref[...]
        if merge:
            nm = mix_ref.shape[1]
            x = x + _dot(mix_ref[...], wo_ref[:nm, :]) + _dot(cross_ref[...], wo_ref[nm:, :])
            res_ref[...] = x
        xn_ref[...] = _rms(x, g_ref[...]).astype(BF16)
        acc_ref[...] = jnp.zeros_like(acc_ref)

    xn = xn_ref[...]
    gate = _dot(xn, wg_ref[...])
    up = _dot(xn, wu_ref[...])
    h = (gate / (1.0 + jnp.exp(-gate)) * up).astype(BF16)
    acc_ref[...] += _dot(h, wd_ref[...])

    @pl.when(j == pl.num_programs(1) - 1)
    def _():
        y = res_ref[...] + 0.5 * acc_ref[...]
        o_ref[...] = _rms(y, gf_ref[...]) if final else y


def _ffn_half(x, g, w_up, w_down, layer, idx, merge=None, final_g=None):
    t = x.shape[0]
    tm = _row_tile(t, 512)
    th = FFN_HIDDEN // 2
    nj = FFN_HIDDEN // th
    row = lambda n: pl.BlockSpec((tm, n), lambda m, j: (m, 0))
    full = lambda a: pl.BlockSpec(a.shape, lambda m, j: (0,) * a.ndim)
    in_specs = [row(D_MODEL)]
    args = [x]
    scratch = [pltpu.VMEM((tm, D_MODEL), BF16), pltpu.VMEM((tm, D_MODEL), F32)]
    if merge is not None:
        mix, cross, w_out = merge
        in_specs += [row(mix.shape[1]), row(cross.shape[1]), full(w_out)]
        args += [mix, cross, w_out]
        scratch.append(pltpu.VMEM((tm, D_MODEL), F32))
    in_specs += [
        full(g),
        pl.BlockSpec((None, None, D_MODEL, th), lambda m, j: (layer, idx, 0, j)),
        pl.BlockSpec((None, None, D_MODEL, th), lambda m, j: (layer, idx, 0, j + nj)),
        pl.BlockSpec((None, None, th, D_MODEL), lambda m, j: (layer, idx, j, 0)),
    ]
    args += [g, w_up, w_up, w_down]
    if final_g is not None:
        in_specs.append(full(final_g))
        args.append(final_g)
    return pl.pallas_call(
        functools.partial(_ffn_kernel, merge=merge is not None, final=final_g is not None),
        grid=(t // tm, nj),
        in_specs=in_specs,
        out_specs=row(D_MODEL),
        out_shape=jax.ShapeDtypeStruct((t, D_MODEL), F32),
        scratch_shapes=scratch,
        compiler_params=_params("parallel", "arbitrary"),
        name="ffn_half",
    )(*args)


def _log_sigmoid(f):
    return jnp.minimum(f, 0.0) - jnp.log(1.0 + jnp.exp(-jnp.abs(f)))


def _fox_proj_kernel(x_ref, g_ref, wqkv_ref, wf_ref, bf_ref, wx_ref,
                     q_ref, k_ref, v_ref, lf_ref, xq_ref, *, qscale):
    hn = _rms(x_ref[...], g_ref[...]).astype(BF16)
    qkv = _dot(hn, wqkv_ref[...])
    q_ref[...] = (qkv[:, :FOX_WIDTH] * qscale).astype(BF16)
    k_ref[...] = qkv[:, FOX_WIDTH:2 * FOX_WIDTH]
    v_ref[...] = qkv[:, 2 * FOX_WIDTH:]
    lf_ref[...] = _log_sigmoid(_dot(hn, wf_ref[...]) + bf_ref[...])
    xq_ref[...] = (_dot(hn, wx_ref[...]) * (X_HEAD_DIM ** -0.5)).astype(BF16)


BIAS_TERMS = 3
TERM_STRIDE = 16


def _split3(x):
    hi = x.astype(BF16)
    r1 = x - hi.astype(F32)
    mid = r1.astype(BF16)
    lo = (r1 - mid.astype(F32)).astype(BF16)
    return hi, mid, lo


def _fox_proj_t_kernel(x_ref, g_ref, wqk_ref, wkvt_ref, wf_ref, bf_ref, wx_ref, tri_ref,
                       q_ref, k_ref, kt_ref, vt_ref, lf_ref, cx_ref, xq_ref, carry_ref,
                       *, qscale, tiles_per_seq):
    hn = _rms(x_ref[...], g_ref[...]).astype(BF16)
    qk = _dot(hn, wqk_ref[...])
    q_ref[...] = (qk[:, :FOX_WIDTH] * qscale).astype(BF16)
    k_ref[...] = qk[:, FOX_WIDTH:].astype(BF16)
    kvt = _dot_nt(wkvt_ref[...], hn)
    kt_ref[0] = kvt[:FOX_WIDTH, :]
    vt_ref[0] = kvt[FOX_WIDTH:, :]
    lf = _log_sigmoid(_dot(hn, wf_ref[...]) + bf_ref[...])
    lf_ref[...] = lf

    @pl.when(pl.program_id(0) % tiles_per_seq == 0)
    def _():
        carry_ref[...] = jnp.zeros_like(carry_ref)

    tri = tri_ref[...]
    span = tri.shape[0]
    carry = carry_ref[...]
    pieces = []
    for r0 in range(0, lf.shape[0], span):
        c = carry
        for term in _split3(lf[r0:r0 + span, :]):
            c = c + _dot(tri, term)
        pieces.append(c)
        carry = c[-1:, :]
    carry_ref[...] = carry
    c2 = jnp.concatenate(pieces, axis=0) * LOG2E
    row = lax.broadcasted_iota(jnp.int32, (LANES, LANES), 0)
    col = lax.broadcasted_iota(jnp.int32, (LANES, LANES), 1)
    packed = None
    for i, term in enumerate(_split3(c2)):
        sel = jnp.where((col == row + TERM_STRIDE * i) & (row < TERM_STRIDE), 1.0, 0.0)
        t = _dot(term, sel.astype(BF16))
        packed = t if packed is None else packed + t
    cx_ref[...] = packed.astype(BF16)
    xq_ref[...] = (_dot(hn, wx_ref[...]) * (X_HEAD_DIM ** -0.5)).astype(BF16)


def _fox_proj(x, g, wqkv, wf, bf, wx, qscale):
    t = x.shape[0]
    tm = _row_tile(t, 512)
    row = lambda n: pl.BlockSpec((tm, n), lambda m: (m, 0))
    full = lambda a: pl.BlockSpec(a.shape, lambda m: (0,) * a.ndim)
    return pl.pallas_call(
        functools.partial(_fox_proj_kernel, qscale=qscale),
        grid=(t // tm,),
        in_specs=[row(D_MODEL), full(g), full(wqkv), full(wf), full(bf), full(wx)],
        out_specs=[row(FOX_WIDTH), row(FOX_WIDTH), row(FOX_WIDTH), row(LANES), row(X_WIDTH)],
        out_shape=[
            jax.ShapeDtypeStruct((t, FOX_WIDTH), BF16),
            jax.ShapeDtypeStruct((t, FOX_WIDTH), F32),
            jax.ShapeDtypeStruct((t, FOX_WIDTH), F32),
            jax.ShapeDtypeStruct((t, LANES), F32),
            jax.ShapeDtypeStruct((t, X_WIDTH), BF16),
        ],
        compiler_params=_params("parallel"),
        name="fox_proj",
    )(x, g, wqkv, wf, bf, wx)


def _fox_proj_t(x, g, wqk, wkvt, wf, bf, wx, qscale, b, s):
    t = x.shape[0]
    tm = _row_tile(s, 512)
    ns = s // tm
    span = _row_tile(tm, 256)
    tri = jnp.asarray(np.tril(np.ones((span, span), np.float32)), dtype=BF16)
    row = lambda n: pl.BlockSpec((tm, n), lambda m: (m, 0))
    full = lambda a: pl.BlockSpec(a.shape, lambda m: (0,) * a.ndim)
    tspec = pl.BlockSpec((1, FOX_WIDTH, tm), lambda m: (m // ns, 0, m % ns))
    return pl.pallas_call(
        functools.partial(_fox_proj_t_kernel, qscale=qscale, tiles_per_seq=ns),
        grid=(t // tm,),
        in_specs=[row(D_MODEL), full(g), full(wqk), full(wkvt), full(wf), full(bf), full(wx),
                  full(tri)],
        out_specs=[row(FOX_WIDTH), row(FOX_WIDTH), tspec, tspec, row(LANES), row(LANES),
                   row(X_WIDTH)],
        out_shape=[
            jax.ShapeDtypeStruct((t, FOX_WIDTH), BF16),
            jax.ShapeDtypeStruct((t, FOX_WIDTH), BF16),
            jax.ShapeDtypeStruct((b, FOX_WIDTH, s), F32),
            jax.ShapeDtypeStruct((b, FOX_WIDTH, s), F32),
            jax.ShapeDtypeStruct((t, LANES), F32),
            jax.ShapeDtypeStruct((t, LANES), BF16),
            jax.ShapeDtypeStruct((t, X_WIDTH), BF16),
        ],
        scratch_shapes=[pltpu.VMEM((1, LANES), F32)],
        compiler_params=_params("arbitrary"),
        name="fox_proj_t",
    )(x, g, wqk, wkvt, wf, bf, wx, tri)


def _bias_select(moves, sign):
    r = lax.broadcasted_iota(jnp.int32, (LANES, LANES), 0)
    c = lax.broadcasted_iota(jnp.int32, (LANES, LANES), 1)
    sel = jnp.zeros((LANES, LANES), F32)
    for head, dst in moves:
        for i in range(BIAS_TERMS):
            sel = jnp.where((r == TERM_STRIDE * i + head) & (c == dst + i), sign, sel)
    return sel.astype(BF16)


def _ones_lanes(shape, lo, n):
    lane = lax.broadcasted_iota(jnp.int32, shape, 1)
    return jnp.where((lane >= lo) & (lane < lo + n), 1.0, 0.0)


def _group8(x, op):
    parts = [x[r * 8:(r + 1) * 8, :] for r in range(x.shape[0] // 8)]
    while len(parts) > 1:
        parts = [op(parts[i], parts[i + 1]) for i in range(0, len(parts), 2)]
    return parts[0]


def _flash_kernel(*refs, tile, n_q, n_k, use_c):
    q_refs = refs[:n_q]
    k_refs = refs[n_q:n_q + n_k]
    vt_ref = refs[n_q + n_k]
    pos = n_q + n_k + 1
    if use_c:
        c_ref = refs[pos]
        pos += 1
    o_ref, kb_ref, vtb_ref, s_ref, p_ref, acc_ref = refs[pos:pos + 6]
    pair = pl.program_id(1)
    seq = vt_ref.shape[2]
    n_tiles = seq // tile
    chunk = 64
    nb = BIAS_TERMS

    for n, k_ref in enumerate(k_refs):
        kb_ref[:, n * LANES:(n + 1) * LANES] = k_ref[...].astype(BF16)
    vtb_ref[...] = vt_ref[0].astype(BF16)
    if use_c:
        feat = (_dot(c_ref[0], _bias_select([(2 * pair, 0), (2 * pair + 1, nb)], -1.0))
                + _ones_lanes((1, LANES), 2 * nb, nb))
        kb_ref[:, n_k * LANES:] = feat.astype(BF16)

    kd_q = n_q * LANES
    lane = lax.broadcasted_iota(jnp.int32, (1, kd_q), 1)
    if n_q == 1:
        own = (lane < FOX_HEAD_DIM, lane >= FOX_HEAD_DIM)
    else:
        own = ((lane < MLA_NOPE) | ((lane >= LANES) & (lane < LANES + MLA_ROPE)),
               ((lane >= MLA_NOPE) & (lane < LANES))
               | ((lane >= LANES + MLA_ROPE) & (lane < LANES + 2 * MLA_ROPE)))

    def q_operands(i):
        rows = slice(i * tile, (i + 1) * tile)
        q = (jnp.concatenate([r[rows, :] for r in q_refs], axis=1) if n_q > 1
             else q_refs[0][rows, :])
        zero = jnp.zeros_like(q)
        qh = [jnp.where(own[h], q, zero) for h in range(2)]
        if use_c:
            cx = c_ref[0, rows, :]
            for h in range(2):
                feat = (_dot(cx, _bias_select([(2 * pair + h, 2 * nb)], 1.0))
                        + _ones_lanes((1, LANES), h * nb, nb))
                qh[h] = jnp.concatenate([qh[h], feat.astype(BF16)], axis=1)
        return qh

    krow = lax.broadcasted_iota(jnp.int32, (chunk, tile), 0)
    qcol = lax.broadcasted_iota(jnp.int32, (chunk, tile), 1)

    def scores(slot, qh, j):
        kblk = kb_ref[j * tile:(j + 1) * tile, :]
        for h in range(2):
            s_ref[slot, h] = _dot_nt(kblk, qh[h])

    def values(slot, par, j, alphas):
        vtblk = vtb_ref[:, j * tile:(j + 1) * tile]
        for h in range(2):
            pv = _dot(vtblk, p_ref[slot, h])
            acc_ref[par, h] = pv if alphas is None else acc_ref[par, h] * alphas[h] + pv

    def softmax(slot, stats, diagonal):
        out = []
        alphas = []
        for h in range(2):
            mx = None
            for c in range(tile // chunk):
                rows = slice(c * chunk, (c + 1) * chunk)
                blk = s_ref[slot, h, rows, :]
                if diagonal:
                    blk = jnp.where(krow + c * chunk > qcol, NEG_INF, blk)
                    s_ref[slot, h, rows, :] = blk
                cm = _group8(blk, jnp.maximum)
                mx = cm if mx is None else jnp.maximum(mx, cm)
            m_new = jnp.max(mx, axis=0, keepdims=True)
            if stats is not None:
                m_old = stats[h][0]
                m_new = jnp.maximum(m_old, m_new)
                alphas.append(jnp.exp2(m_old - m_new))
            ls = None
            for c in range(tile // chunk):
                rows = slice(c * chunk, (c + 1) * chunk)
                p = jnp.exp2(s_ref[slot, h, rows, :] - m_new)
                ps = _group8(p, jnp.add)
                ls = ps if ls is None else ls + ps
                p_ref[slot, h, rows, :] = p.astype(BF16)
            l = jnp.sum(ls, axis=0, keepdims=True)
            if stats is not None:
                l = stats[h][1] * alphas[-1] + l
            out.append((m_new, l))
        return out, (None if stats is None else alphas)

    def finish(i, stats):
        par = i % 2
        row = lax.broadcasted_iota(jnp.int32, (LANES, 1), 0)
        o_t = jnp.where(row < FOX_HEAD_DIM, acc_ref[par, 0] / stats[0][1],
                        acc_ref[par, 1] / stats[1][1])
        o_ref[i * tile:(i + 1) * tile, :] = o_t.T.astype(o_ref.dtype)

    blocks = [(i, j) for i in range(n_tiles) for j in range(i + 1)]
    qh_of = {0: q_operands(0)}
    scores(0, qh_of[0], 0)
    stats = None
    pending = None
    for n, (i, j) in enumerate(blocks):
        if n + 1 < len(blocks):
            i2, j2 = blocks[n + 1]
            if i2 not in qh_of:
                qh_of[i2] = q_operands(i2)
            scores((n + 1) % 2, qh_of[i2], j2)
        if pending is not None:
            slot_p, i_p, j_p, alphas_p, stats_p = pending
            values(slot_p, i_p % 2, j_p, alphas_p)
            if j_p == i_p:
                finish(i_p, stats_p)
        stats, alphas = softmax(n % 2, None if j == 0 else stats, j == i)
        pending = (n % 2, i, j, alphas, stats)
    slot_p, i_p, j_p, alphas_p, stats_p = pending
    values(slot_p, i_p % 2, j_p, alphas_p)
    finish(i_p, stats_p)


FLASH_TILE = 512


def _flash(qs, ks, vt, b, s, c=None):
    tile = _row_tile(s, FLASH_TILE)
    use_c = c is not None
    kd = LANES * (len(ks) + (1 if use_c else 0))
    rows = pl.BlockSpec((s, LANES), lambda i, p: (i, p))
    in_specs = [rows] * (len(qs) + len(ks))
    in_specs.append(pl.BlockSpec((1, LANES, s), lambda i, p: (i, p, 0)))
    args = list(qs) + list(ks) + [vt]
    if use_c:
        in_specs.append(pl.BlockSpec((1, s, LANES), lambda i, p: (i, 0, 0)))
        args.append(c)
    kern = functools.partial(_flash_kernel, tile=tile, n_q=len(qs), n_k=len(ks), use_c=use_c)
    return pl.pallas_call(
        kern,
        grid=(b, PAIRS),
        in_specs=in_specs,
        out_specs=rows,
        out_shape=jax.ShapeDtypeStruct((b * s, PAIRS * LANES), BF16),
        scratch_shapes=[pltpu.VMEM((s, kd), BF16), pltpu.VMEM((LANES, s), BF16),
                        pltpu.VMEM((2, 2, tile, tile), F32),
                        pltpu.VMEM((2, 2, tile, tile), BF16),
                        pltpu.VMEM((2, 2, LANES, tile), F32)],
        compiler_params=_params("parallel", "parallel"),
        name="flash_mla" if len(qs) > 1 else "flash_fox",
    )(*args)


def _cross_kernel(q_ref, k_ref, v_ref, o_ref):
    q = q_ref[0].astype(BF16)
    kb = k_ref[0].astype(BF16)
    vb = v_ref[0].astype(BF16)
    lane = lax.broadcasted_iota(jnp.int32, (1, LANES), 1)
    first = lane < X_HEAD_DIM
    zero = jnp.zeros_like(q)
    outs = []
    for qh in (jnp.where(first, q, zero), jnp.where(first, zero, q)):
        s = _dot_nt(qh, kb)
        m = jnp.max(s, axis=1, keepdims=True)
        p = jnp.exp(s - m)
        l = jnp.sum(p, axis=1, keepdims=True)
        outs.append(_dot(p.astype(BF16), vb) / l)
    o_ref[0] = jnp.where(first, outs[0], outs[1]).astype(o_ref.dtype)


def _cross(xq, mk, mv):
    b, t, _ = xq.shape
    m = mk.shape[1]
    tq = _row_tile(t, 1024)
    qspec = pl.BlockSpec((1, tq, LANES), lambda i, p, j: (i, j, p))
    kspec = pl.BlockSpec((1, m, LANES), lambda i, p, j: (i, 0, p))
    return pl.pallas_call(
        _cross_kernel,
        grid=(b, X_WIDTH // LANES, t // tq),
        in_specs=[qspec, kspec, kspec],
        out_specs=qspec,
        out_shape=jax.ShapeDtypeStruct((b, t, X_WIDTH), BF16),
        compiler_params=_params("parallel", "parallel", "arbitrary"),
        name="cross_attn",
    )(xq, mk, mv)


def _cross_t_kernel(q_ref, kt_ref, vt_ref, o_ref):
    lane = lax.broadcasted_iota(jnp.int32, (1, LANES), 1)
    first = lane < X_HEAD_DIM
    for g in range(q_ref.shape[0]):
        q = q_ref[g].astype(BF16)
        halves = []
        for p in range(X_WIDTH // LANES):
            cols = slice(p * LANES, (p + 1) * LANES)
            qp = q[:, cols]
            ktb = kt_ref[g, cols, :].astype(BF16)
            vtb = vt_ref[g, cols, :].astype(BF16)
            zero = jnp.zeros_like(qp)
            outs = []
            for qh in (jnp.where(first, qp, zero), jnp.where(first, zero, qp)):
                s = _dot(qh, ktb)
                mx = jnp.max(s, axis=1, keepdims=True)
                pr = jnp.exp(s - mx)
                l = jnp.sum(pr, axis=1, keepdims=True)
                outs.append(_dot_nt(pr.astype(BF16), vtb) / l)
            halves.append(jnp.where(first, outs[0], outs[1]))
        o_ref[g] = jnp.concatenate(halves, axis=1).astype(o_ref.dtype)


CROSS_T_BATCHES = 8


def _cross_t(xq, mkt, mvt, layer):
    b, t, _ = xq.shape
    m = mkt.shape[3]
    g = CROSS_T_BATCHES if b % CROSS_T_BATCHES == 0 else 1
    qspec = pl.BlockSpec((g, t, X_WIDTH), lambda i: (i, 0, 0))
    kspec = pl.BlockSpec((None, g, X_WIDTH, m), lambda i: (layer, i, 0, 0))
    return pl.pallas_call(
        _cross_t_kernel,
        grid=(b // g,),
        in_specs=[qspec, kspec, kspec],
        out_specs=qspec,
        out_shape=jax.ShapeDtypeStruct((b, t, X_WIDTH), BF16),
        compiler_params=_params("parallel"),
        name="cross_attn_t",
    )(xq, mkt, mvt)


def _norm_proj_kernel(x_ref, g_ref, w_ref, o_ref):
    o_ref[...] = _dot(_rms(x_ref[...], g_ref[...]).astype(BF16), w_ref[...])


def _norm_proj(x, g, w):
    t = x.shape[0]
    n = w.shape[1]
    tm = _row_tile(t, 512)
    return pl.pallas_call(
        _norm_proj_kernel,
        grid=(t // tm,),
        in_specs=[pl.BlockSpec((tm, D_MODEL), lambda m: (m, 0)),
                  pl.BlockSpec((1, D_MODEL), lambda m: (0, 0)),
                  pl.BlockSpec(w.shape, lambda m: (0, 0))],
        out_specs=pl.BlockSpec((tm, n), lambda m: (m, 0)),
        out_shape=jax.ShapeDtypeStruct((t, n), F32),
        compiler_params=_params("parallel"),
        name="norm_proj",
    )(x, g, w)


def _mla_proj_kernel(*refs, qscale, with_kv):
    (x_ref, g_ref, wqa_ref, wkva_ref, wkr_ref, wkrr_ref, wx_ref,
     qg_ref, wqn_ref, wqr_ref, wqrr_ref, kvg_ref) = refs[:12]
    pos = 12
    if with_kv:
        wuk_ref, wuvt_ref = refs[pos:pos + 2]
        pos += 2
    cos_ref, sin_ref = refs[pos:pos + 2]
    qn_ref, qr_ref, c_ref, kr_ref, xq_ref = refs[pos + 2:pos + 7]
    hn = _rms(x_ref[...], g_ref[...]).astype(BF16)
    cos = cos_ref[...]
    sin = sin_ref[...]
    qa = _rms(_dot(hn, wqa_ref[...]), qg_ref[...]).astype(BF16)
    qn_ref[...] = (_dot(qa, wqn_ref[...]) * qscale).astype(BF16)
    qr = _dot(qa, wqr_ref[...])
    qrr = _dot(qa, wqrr_ref[...])
    for p in range(PAIRS):
        sl = slice(p * LANES, (p + 1) * LANES)
        qr_ref[:, sl] = ((qr[:, sl] * cos + qrr[:, sl] * sin) * qscale).astype(BF16)
    c = _rms(_dot(hn, wkva_ref[...]), kvg_ref[...])
    c_ref[...] = c
    if with_kv:
        kn_ref, vt_ref = refs[pos + 7:pos + 9]
        cb = c.astype(BF16)
        kn_ref[...] = _dot(cb, wuk_ref[...]).astype(BF16)
        vt_ref[0] = _dot_nt(wuvt_ref[...], cb).astype(BF16)
    kr = _dot(hn, wkr_ref[...]) * cos + _dot(hn, wkrr_ref[...]) * sin
    kr_ref[...] = kr[:, :MLA_ROPE]
    xq_ref[...] = (_dot(hn, wx_ref[...]) * (X_HEAD_DIM ** -0.5)).astype(BF16)


def _mla_proj(x, g, w, cos, sin, qscale, kv_w=None, b=None, s=None):
    t = x.shape[0]
    with_kv = kv_w is not None
    tm = _row_tile(s if with_kv else t, 512)
    ntab = cos.shape[0] // tm
    row = lambda n: pl.BlockSpec((tm, n), lambda m: (m, 0))
    full = lambda a: pl.BlockSpec(a.shape, lambda m: (0,) * a.ndim)
    tab = pl.BlockSpec((tm, LANES), lambda m: (m % ntab, 0))
    wide = PAIRS * LANES
    w = list(w) + (list(kv_w) if with_kv else [])
    out_specs = [row(wide), row(wide), row(MLA_KV_LORA), row(MLA_ROPE), row(X_WIDTH)]
    out_shape = [
        jax.ShapeDtypeStruct((t, wide), BF16),
        jax.ShapeDtypeStruct((t, wide), BF16),
        jax.ShapeDtypeStruct((t, MLA_KV_LORA), F32),
        jax.ShapeDtypeStruct((t, MLA_ROPE), F32),
        jax.ShapeDtypeStruct((t, X_WIDTH), BF16),
    ]
    if with_kv:
        ns = s // tm
        out_specs += [row(wide), pl.BlockSpec((1, wide, tm), lambda m: (m // ns, 0, m % ns))]
        out_shape += [jax.ShapeDtypeStruct((t, wide), BF16),
                      jax.ShapeDtypeStruct((b, wide, s), BF16)]
    return pl.pallas_call(
        functools.partial(_mla_proj_kernel, qscale=qscale, with_kv=with_kv),
        grid=(t // tm,),
        in_specs=[row(D_MODEL), full(g)] + [full(a) for a in w] + [tab, tab],
        out_specs=out_specs,
        out_shape=out_shape,
        compiler_params=_params("parallel"),
        name="mla_proj_kv" if with_kv else "mla_proj",
    )(x, g, *w, cos, sin)


FOX_PAGES_PER_STEP = 16
MLA_PAGES_PER_STEP = 32
PAGE_GROUPS = 4


def _pages_per_step(n_pages, pref):
    while n_pages % pref:
        pref //= 2
    return pref


def _online_step(state, s_parts, v_parts, v_is_transposed):
    m, l, acc = state
    n = len(s_parts)
    per = n // PAGE_GROUPS if n % PAGE_GROUPS == 0 else n
    partials = []
    for g0 in range(0, n, per):
        s = jnp.concatenate(s_parts[g0:g0 + per], axis=1) if per > 1 else s_parts[g0]
        mg = jnp.max(s, axis=1, keepdims=True)
        p = jnp.exp(s - mg)
        lg = jnp.sum(p, axis=1, keepdims=True)
        ag = None
        for r in range(per):
            pb = p[:, r * LANES:(r + 1) * LANES].astype(BF16)
            vb = v_parts[g0 + r]
            t = _dot_nt(pb, vb) if v_is_transposed else _dot(pb, vb)
            ag = t if ag is None else ag + t
        partials.append((mg, lg, ag))
    m_new = m
    for mg, _, _ in partials:
        m_new = jnp.maximum(m_new, mg)
    alpha = jnp.exp(m - m_new)
    l = l * alpha
    acc = acc * alpha
    for mg, lg, ag in partials:
        w = jnp.exp(mg - m_new)
        l = l + lg * w
        acc = acc + ag * w
    return m_new, l, acc


def _head_diag(acc, t_new, width):
    n = acc.shape[1]
    rowh = lax.broadcasted_iota(jnp.int32, (HEAD_PAD, n), 0)
    laneh = lax.broadcasted_iota(jnp.int32, (HEAD_PAD, n), 1) // width
    orow = lax.broadcasted_iota(jnp.int32, (t_new, n), 0)
    out = jnp.zeros((t_new, n), F32)
    for i in range(t_new):
        blk = acc[i * HEAD_PAD:(i + 1) * HEAD_PAD, :]
        r = jnp.sum(jnp.where(rowh == laneh, blk, 0.0), axis=0, keepdims=True)
        out = jnp.where(orow == i, r, out)
    return out


def _fox_dec_kernel(pt_ref, qb_ref, kn_ref, vn_ref, lft_ref, sufm_ref, *rest, pb, t_new):
    kt_refs = rest[:pb]
    vt_refs = rest[pb:2 * pb]
    lf_refs = rest[2 * pb:3 * pb]
    o_ref, m_ref, l_ref, acc_ref, run_ref, pad_ref = rest[3 * pb:]
    step = pl.program_id(1)
    qb = qb_ref[0]
    lft = lft_ref[0]
    lane = lax.broadcasted_iota(jnp.int32, (HEAD_PAD, LANES), 1)
    cn_cols = []
    run = jnp.zeros((HEAD_PAD, 1), F32)
    for i in range(t_new):
        run = run + lft[:, i:i + 1]
        cn_cols.append(run)
    cn_col = jnp.concatenate(cn_cols, axis=0)

    @pl.when(step == 0)
    def _():
        cn_lanes = jnp.zeros((HEAD_PAD, LANES), F32)
        for i in range(t_new):
            cn_lanes = jnp.where(lane == i, cn_cols[i], cn_lanes)
        pad_ref[...] = jnp.zeros_like(pad_ref)
        pad_ref[0:t_new, :] = kn_ref[0]
        kpad = pad_ref[...].astype(BF16)
        pad_ref[0:t_new, :] = vn_ref[0]
        vpad = pad_ref[...].astype(BF16)
        s = _dot_nt(qb, kpad)
        parts = []
        for i in range(t_new):
            blk = s[i * HEAD_PAD:(i + 1) * HEAD_PAD, :] + (cn_cols[i] - cn_lanes)
            parts.append(jnp.where(lane <= i, blk, NEG_INF))
        s = jnp.concatenate(parts, axis=0)
        m = jnp.max(s, axis=1, keepdims=True)
        p = jnp.exp(s - m)
        m_ref[...] = m
        l_ref[...] = jnp.sum(p, axis=1, keepdims=True)
        acc_ref[...] = _dot(p.astype(BF16), vpad)
        run_ref[...] = jnp.zeros_like(run_ref)

    run = run_ref[...]
    s_parts, v_parts = [], []
    lf_all = jnp.concatenate([lf_refs[r][0] for r in range(pb)], axis=0)
    suf_all = _dot_exact(lf_all, sufm_ref[...])
    tot_all = jnp.sum(lf_all, axis=1, keepdims=True)
    for r in range(pb):
        heads = slice(r * HEAD_PAD, (r + 1) * HEAD_PAD)
        suf = suf_all[heads, :] + run
        run = run + tot_all[heads, :]
        ktb = kt_refs[r][0, 0].reshape(FOX_WIDTH, PAGE_SIZE).astype(BF16)
        s = _dot(qb, ktb)
        bias = jnp.concatenate([suf] * t_new, axis=0) + cn_col
        s_parts.append(s + bias)
        v_parts.append(vt_refs[r][0, 0].reshape(FOX_WIDTH, PAGE_SIZE).astype(BF16))
    run_ref[...] = run
    m, l, acc = _online_step((m_ref[...], l_ref[...], acc_ref[...]), s_parts, v_parts, True)
    m_ref[...] = m
    l_ref[...] = l
    acc_ref[...] = acc

    @pl.when(step == pl.num_programs(1) - 1)
    def _():
        o_ref[0] = _head_diag(acc / l, t_new, FOX_HEAD_DIM)


def _fox_decode(page_table, qblk, k_new, v_new, lft_new, cache_kt, cache_vt, cache_lf):
    bd, n_pages = page_table.shape
    t_new = k_new.shape[1]
    rows = t_new * HEAD_PAD
    pb = _pages_per_step(n_pages, FOX_PAGES_PER_STEP)
    sufm = jnp.asarray(np.tril(np.ones((PAGE_SIZE, PAGE_SIZE), np.float32), -1))

    def page(r):
        return lambda b, s, pt: pt[b, n_pages - 1 - (s * pb + r)]

    kv_specs = [pl.BlockSpec((1, 1, FOX_HEADS, FOX_HEAD_DIM, PAGE_SIZE),
                             (lambda b, s, pt, f=page(r): (0, f(b, s, pt), 0, 0, 0)))
                for r in range(pb)]
    lf_specs = [pl.BlockSpec((1, HEAD_PAD, PAGE_SIZE),
                             (lambda b, s, pt, f=page(r): (f(b, s, pt), 0, 0)))
                for r in range(pb)]
    per_b = lambda shape: pl.BlockSpec((1,) + shape, lambda b, s, pt: (b, 0, 0))
    grid_spec = pltpu.PrefetchScalarGridSpec(
        num_scalar_prefetch=1,
        grid=(bd, n_pages // pb),
        in_specs=[per_b((rows, FOX_WIDTH)), per_b((t_new, FOX_WIDTH)), per_b((t_new, FOX_WIDTH)),
                  per_b((HEAD_PAD, LANES)),
                  pl.BlockSpec((PAGE_SIZE, PAGE_SIZE), lambda b, s, pt: (0, 0))]
        + kv_specs + kv_specs + lf_specs,
        out_specs=per_b((t_new, FOX_WIDTH)),
        scratch_shapes=[pltpu.VMEM((rows, 1), F32), pltpu.VMEM((rows, 1), F32),
                        pltpu.VMEM((rows, FOX_WIDTH), F32), pltpu.VMEM((HEAD_PAD, 1), F32),
                        pltpu.VMEM((PAGE_SIZE, FOX_WIDTH), F32)],
    )
    return pl.pallas_call(
        functools.partial(_fox_dec_kernel, pb=pb, t_new=t_new),
        grid_spec=grid_spec,
        out_shape=jax.ShapeDtypeStruct((bd, t_new, FOX_WIDTH), F32),
        compiler_params=_params("parallel", "arbitrary"),
        name="fox_decode",
    )(page_table, qblk, k_new, v_new, lft_new, sufm,
      *([cache_kt] * pb), *([cache_vt] * pb), *([cache_lf] * pb))


def _mla_dec_kernel(pt_ref, qn_ref, qr_ref, cn_ref, krn_ref, wuk_ref, wuv_ref, *rest, pb, t_new):
    c_refs = rest[:pb]
    kr_refs = rest[pb:2 * pb]
    o_ref, m_ref, l_ref, acc_ref, ql_ref, cpad_ref, krt_ref = rest[2 * pb:]
    step = pl.program_id(1)
    lane = lax.broadcasted_iota(jnp.int32, (HEAD_PAD, LANES), 1)
    qr = qr_ref[0]

    @pl.when(step == 0)
    def _():
        ql = _dot(qn_ref[0], wuk_ref[...]).astype(BF16)
        ql_ref[...] = ql
        cpad_ref[...] = jnp.zeros_like(cpad_ref)
        cpad_ref[0:t_new, :] = cn_ref[0]
        cpad = cpad_ref[...].astype(BF16)
        s = _dot_nt(ql, cpad) + _dot_nt(qr, krn_ref[0])
        parts = [jnp.where(lane <= i, s[i * HEAD_PAD:(i + 1) * HEAD_PAD, :], NEG_INF)
                 for i in range(t_new)]
        s = jnp.concatenate(parts, axis=0)
        m = jnp.max(s, axis=1, keepdims=True)
        p = jnp.exp(s - m)
        m_ref[...] = m
        l_ref[...] = jnp.sum(p, axis=1, keepdims=True)
        acc_ref[...] = _dot(p.astype(BF16), cpad)
        krt_ref[...] = jnp.zeros_like(krt_ref)

    ql = ql_ref[...]
    s_parts, v_parts = [], []
    for r in range(pb):
        cb = c_refs[r][0, 0].astype(BF16)
        krt_ref[0:MLA_ROPE, :] = kr_refs[r][0, 0].astype(BF16)
        s_parts.append(_dot_nt(ql, cb) + _dot(qr, krt_ref[...]))
        v_parts.append(cb)
    m, l, acc = _online_step((m_ref[...], l_ref[...], acc_ref[...]), s_parts, v_parts, False)
    m_ref[...] = m
    l_ref[...] = l
    acc_ref[...] = acc

    @pl.when(step == pl.num_programs(1) - 1)
    def _():
        full = _dot((acc / l).astype(BF16), wuv_ref[...])
        o_ref[0] = _head_diag(full, t_new, MLA_V)


def _mla_decode(page_table, qnblk, qrblk, c_new, kr_new, wuk_t, wuv, cache_c, cache_krt):
    bd, n_pages = page_table.shape
    t_new = c_new.shape[1]
    rows = t_new * HEAD_PAD
    pb = _pages_per_step(n_pages, MLA_PAGES_PER_STEP)

    def page(r):
        return lambda b, s, pt: pt[b, s * pb + r]

    c_specs = [pl.BlockSpec((1, 1, PAGE_SIZE, MLA_KV_LORA),
                            (lambda b, s, pt, f=page(r): (0, f(b, s, pt), 0, 0)))
               for r in range(pb)]
    kr_specs = [pl.BlockSpec((1, 1, MLA_ROPE, PAGE_SIZE),
                             (lambda b, s, pt, f=page(r): (0, f(b, s, pt), 0, 0)))
                for r in range(pb)]
    per_b = lambda shape: pl.BlockSpec((1,) + shape, lambda b, s, pt: (b, 0, 0))
    full = lambda a: pl.BlockSpec(a.shape, lambda b, s, pt: (0,) * a.ndim)
    wide = MLA_HEADS * MLA_V
    grid_spec = pltpu.PrefetchScalarGridSpec(
        num_scalar_prefetch=1,
        grid=(bd, n_pages // pb),
        in_specs=[per_b((rows, MLA_HEADS * MLA_NOPE)), per_b((rows, LANES)),
                  per_b((t_new, MLA_KV_LORA)), per_b((PAGE_SIZE, LANES)),
                  full(wuk_t), full(wuv)] + c_specs + kr_specs,
        out_specs=per_b((t_new, wide)),
        scratch_shapes=[pltpu.VMEM((rows, 1), F32), pltpu.VMEM((rows, 1), F32),
                        pltpu.VMEM((rows, MLA_KV_LORA), F32),
                        pltpu.VMEM((rows, MLA_KV_LORA), BF16),
                        pltpu.VMEM((PAGE_SIZE, MLA_KV_LORA), F32),
                        pltpu.VMEM((LANES, PAGE_SIZE), BF16)],
    )
    return pl.pallas_call(
        functools.partial(_mla_dec_kernel, pb=pb, t_new=t_new),
        grid_spec=grid_spec,
        out_shape=jax.ShapeDtypeStruct((bd, t_new, wide), F32),
        compiler_params=_params("parallel", "arbitrary"),
        name="mla_decode",
    )(page_table, qnblk, qrblk, c_new, kr_new, wuk_t, wuv,
      *([cache_c] * pb), *([cache_krt] * pb))


def _rope_tables(pos):
    half = MLA_ROPE // 2
    inv = ROPE_THETA ** (-jnp.arange(half, dtype=F32) / half)
    ang = pos.astype(F32)[:, None] * inv
    reps = (LANES // 2) // half
    pad = jnp.zeros((pos.shape[0], LANES // 2), F32)
    cos = jnp.concatenate([jnp.tile(jnp.cos(ang), (1, reps)), pad], axis=1)
    sin = jnp.concatenate([jnp.tile(jnp.sin(ang), (1, reps)), pad], axis=1)
    return cos, sin


def _rot_cols(w):
    half = MLA_ROPE // 2
    return jnp.concatenate([-w[..., half:], w[..., :half]], axis=-1)


def _pair_rope_cols(w):
    r = w.shape[0]
    w = w.reshape(r, PAIRS, 2 * MLA_ROPE)
    return jnp.pad(w, ((0, 0), (0, 0), (0, LANES - 2 * MLA_ROPE))).reshape(r, PAIRS * LANES)


def _block_diag_rows(q, width):
    bd, t, n = q.shape
    rowh = np.arange(t * HEAD_PAD) % HEAD_PAD
    mask = jnp.asarray(rowh[:, None] == (np.arange(n) // width)[None, :])
    rep = jnp.repeat(q, HEAD_PAD, axis=1)
    return jnp.where(mask[None], rep, jnp.zeros_like(rep))


def kernel(x_prompt, x_sample, cache_fox_k, cache_fox_v, cache_fox_logf, cache_mla_ckv,
           cache_mla_krope, cache_mem_k, cache_mem_v, page_table, mem_prompt, norm_gains,
           ffn_w_up, ffn_w_down, fox_w_in, fox_b_f, mla_w_in, mla_q_norm, mla_w_q_b,
           mla_kv_norm, mla_w_kv_b, mem_norm, w_mem_kv, w_out, final_norm):
    b, s, d = x_prompt.shape
    bd, t_new, _ = x_sample.shape
    n_pages = page_table.shape[1]
    mem_len = mem_prompt.shape[1]
    depth = norm_gains.shape[0]
    xp = x_prompt.reshape(b * s, d)
    xs = x_sample.reshape(bd * t_new, d)

    w_up = ffn_w_up.astype(BF16)
    w_down = ffn_w_down.astype(BF16)
    w_out_b = w_out.astype(BF16)
    w_mem_b = w_mem_kv.astype(BF16)
    gains = norm_gains.reshape(depth, 3, 1, d)
    mem2d = mem_prompt.reshape(b * mem_len, d)
    mkt_s = jnp.transpose(cache_mem_k, (0, 1, 3, 4, 2)).reshape(depth, bd, X_WIDTH, mem_len)
    mvt_s = jnp.transpose(cache_mem_v, (0, 1, 3, 4, 2)).reshape(depth, bd, X_WIDTH, mem_len)

    outs = {}
    for layer in range(depth):
        j = layer // 2
        xp = _ffn_half(xp, gains[layer, 0], w_up, w_down, layer, 0)
        xs = _ffn_half(xs, gains[layer, 0], w_up, w_down, layer, 0)

        mkv = _norm_proj(mem2d, mem_norm[layer].reshape(1, d), w_mem_b[layer])
        mk = mkv[:, :X_WIDTH].reshape(b, mem_len, X_WIDTH)
        mv = mkv[:, X_WIDTH:].reshape(b, mem_len, X_WIDTH)
        outs.setdefault("mk", []).append(mk.reshape(b, mem_len, X_HEADS, X_HEAD_DIM))
        outs.setdefault("mv", []).append(mv.reshape(b, mem_len, X_HEADS, X_HEAD_DIM))

        if layer % 2 == 0:
            w_in = fox_w_in[j]
            wqkv = w_in[:, :3 * FOX_WIDTH].astype(BF16)
            wf = jnp.pad(w_in[:, 3 * FOX_WIDTH:3 * FOX_WIDTH + FOX_HEADS],
                         ((0, 0), (0, LANES - FOX_HEADS))).astype(BF16)
            bf = jnp.pad(fox_b_f[j], (0, LANES - FOX_HEADS)).reshape(1, LANES)
            wx = w_in[:, 3 * FOX_WIDTH + FOX_HEADS:].astype(BF16)
            g1 = gains[layer, 1]

            wqk = w_in[:, :2 * FOX_WIDTH].astype(BF16)
            wkvt = jnp.transpose(w_in[:, FOX_WIDTH:3 * FOX_WIDTH]).astype(BF16)
            q, kb, kt, vt, lf, cx, xq_p = _fox_proj_t(
                xp, g1, wqk, wkvt, wf, bf, wx, FOX_HEAD_DIM ** -0.5 * LOG2E, b, s)
            mix_p = _flash([q], [kb], vt, b, s, cx.reshape(b, s, LANES))
            heads_t = (0, 1, 4, 2, 3)
            outs["fk_p"] = jnp.transpose(kt.reshape(1, b, FOX_HEADS, FOX_HEAD_DIM, s), heads_t)
            outs["fv_p"] = jnp.transpose(vt.reshape(1, b, FOX_HEADS, FOX_HEAD_DIM, s), heads_t)
            outs["fl_p"] = lf[:, :FOX_HEADS].reshape(1, b, s, FOX_HEADS)

            q, k, v, lf, xq_s = _fox_proj(xs, g1, wqkv, wf, bf, wx, FOX_HEAD_DIM ** -0.5)
            outs["fk_s"] = k.reshape(1, bd, t_new, FOX_HEADS, FOX_HEAD_DIM)
            outs["fv_s"] = v.reshape(1, bd, t_new, FOX_HEADS, FOX_HEAD_DIM)
            outs["fl_s"] = lf[:, :FOX_HEADS].reshape(1, bd, t_new, FOX_HEADS)
            qblk = _block_diag_rows(q.reshape(bd, t_new, FOX_WIDTH), FOX_HEAD_DIM)
            lft = jnp.transpose(lf.reshape(bd, t_new, LANES)[:, :, :HEAD_PAD], (0, 2, 1))
            lft = jnp.pad(lft, ((0, 0), (0, 0), (0, LANES - t_new)))
            cache_kt = jnp.transpose(cache_fox_k, (0, 1, 3, 4, 2))
            cache_vt = jnp.transpose(cache_fox_v, (0, 1, 3, 4, 2))
            cache_lf = jnp.pad(jnp.transpose(cache_fox_logf[j], (0, 2, 1)),
                               ((0, 0), (0, HEAD_PAD - FOX_HEADS), (0, 0)))
            mix_s = _fox_decode(page_table, qblk, k.reshape(bd, t_new, FOX_WIDTH),
                                v.reshape(bd, t_new, FOX_WIDTH), lft,
                                cache_kt[j:j + 1], cache_vt[j:j + 1], cache_lf)
            mix_s = mix_s.reshape(bd * t_new, FOX_WIDTH).astype(BF16)
        else:
            w_in = mla_w_in[j]
            o1 = MLA_Q_LORA
            o2 = o1 + MLA_KV_LORA
            o3 = o2 + MLA_ROPE
            wkr = w_in[:, o2:o3]
            lane_pad = ((0, 0), (0, LANES - MLA_ROPE))
            wqb = mla_w_q_b[j]
            wkvb = mla_w_kv_b[j]
            weights = [
                w_in[:, :o1].astype(BF16),
                w_in[:, o1:o2].astype(BF16),
                jnp.pad(wkr, lane_pad).astype(BF16),
                jnp.pad(_rot_cols(wkr), lane_pad).astype(BF16),
                w_in[:, o3:].astype(BF16),
                mla_q_norm[j].reshape(1, MLA_Q_LORA),
                wqb[:, :, :MLA_NOPE].reshape(MLA_Q_LORA, MLA_HEADS * MLA_NOPE).astype(BF16),
                _pair_rope_cols(wqb[:, :, MLA_NOPE:]).astype(BF16),
                _pair_rope_cols(_rot_cols(wqb[:, :, MLA_NOPE:])).astype(BF16),
                mla_kv_norm[j].reshape(1, MLA_KV_LORA),
            ]
            wuk = wkvb[:, :, :MLA_NOPE].reshape(MLA_KV_LORA, MLA_HEADS * MLA_NOPE).astype(BF16)
            wuv = wkvb[:, :, MLA_NOPE:].reshape(MLA_KV_LORA, MLA_HEADS * MLA_V).astype(BF16)
            wuv_t = jnp.transpose(wkvb[:, :, MLA_NOPE:], (1, 2, 0)).reshape(
                MLA_HEADS * MLA_V, MLA_KV_LORA).astype(BF16)
            mla_scale = (MLA_NOPE + MLA_ROPE) ** -0.5
            g1 = gains[layer, 1]
            cos_p, sin_p = _rope_tables(jnp.arange(s, dtype=jnp.int32))
            pos_s = n_pages * PAGE_SIZE + jnp.arange(t_new, dtype=jnp.int32)
            cos_s, sin_s = _rope_tables(jnp.tile(pos_s, bd))

            qn, qr, c, kr, xq_p, kn, vt = _mla_proj(
                xp, g1, weights, cos_p, sin_p, mla_scale * LOG2E, (wuk, wuv_t), b, s)
            kr2 = jnp.concatenate(
                [kr, kr, jnp.zeros((b * s, LANES - 2 * MLA_ROPE), F32)], axis=1).astype(BF16)
            kr2 = jnp.tile(kr2, (1, PAIRS))
            mix_p = _flash([qn, qr], [kn, kr2], vt, b, s)
            outs["mc_p"] = c.reshape(1, b, s, MLA_KV_LORA)
            outs["mr_p"] = kr.reshape(1, b, s, MLA_ROPE)

            qn, qr, c, kr, xq_s = _mla_proj(xs, g1, weights, cos_s, sin_s, mla_scale)
            outs["mc_s"] = c.reshape(1, bd, t_new, MLA_KV_LORA)
            outs["mr_s"] = kr.reshape(1, bd, t_new, MLA_ROPE)
            qnblk = _block_diag_rows(qn.reshape(bd, t_new, MLA_HEADS * MLA_NOPE), MLA_NOPE)
            qr4 = qr.reshape(bd, t_new, PAIRS, LANES)[..., :2 * MLA_ROPE]
            qr4 = qr4.reshape(bd, t_new, MLA_HEADS, MLA_ROPE)
            qr4 = jnp.pad(qr4, ((0, 0), (0, 0), (0, HEAD_PAD - MLA_HEADS), (0, LANES - MLA_ROPE)))
            qrblk = qr4.reshape(bd, t_new * HEAD_PAD, LANES)
            kr_new = jnp.pad(kr.reshape(bd, t_new, MLA_ROPE).astype(BF16),
                             ((0, 0), (0, PAGE_SIZE - t_new), (0, LANES - MLA_ROPE)))
            wuk_t = jnp.transpose(wkvb[:, :, :MLA_NOPE], (1, 2, 0)).reshape(
                MLA_HEADS * MLA_NOPE, MLA_KV_LORA).astype(BF16)
            cache_krt = jnp.transpose(cache_mla_krope, (0, 1, 3, 2))
            mix_s = _mla_decode(page_table, qnblk, qrblk, c.reshape(bd, t_new, MLA_KV_LORA),
                                kr_new, wuk_t, wuv, cache_mla_ckv[j:j + 1],
                                cache_krt[j:j + 1])
            mix_s = mix_s.reshape(bd * t_new, MLA_HEADS * MLA_V).astype(BF16)

        cross_p = _cross(xq_p.reshape(b, s, X_WIDTH), mk, mv).reshape(b * s, X_WIDTH)
        cross_s = _cross_t(xq_s.reshape(bd, t_new, X_WIDTH), mkt_s, mvt_s, layer).reshape(
            bd * t_new, X_WIDTH)
        fin = final_norm.reshape(1, d) if layer == depth - 1 else None
        xp = _ffn_half(xp, gains[layer, 2], w_up, w_down, layer, 1,
                       (mix_p, cross_p, w_out_b[layer]), fin)
        xs = _ffn_half(xs, gains[layer, 2], w_up, w_down, layer, 1,
                       (mix_s, cross_s, w_out_b[layer]), fin)

    y_prompt = xp.reshape(b, s, d)
    y_sample = xs.reshape(bd, t_new, d)
    return (y_prompt, y_sample,
            outs["fk_p"], outs["fv_p"], outs["fl_p"],
            outs["fk_s"], outs["fv_s"], outs["fl_s"],
            outs["mc_p"], outs["mr_p"], outs["mc_s"], outs["mr_s"],
            jnp.stack(outs["mk"]), jnp.stack(outs["mv"]))
```

```python
import functools

import numpy as np
import jax
import jax.numpy as jnp
from jax import lax
from jax.experimental import pallas as pl
from jax.experimental.pallas import tpu as pltpu

D_MODEL = 1024
PAGE_SIZE = 128
X_HEADS = 4
X_HEAD_DIM = 64
X_WIDTH = X_HEADS * X_HEAD_DIM
FOX_HEADS = 12
FOX_HEAD_DIM = 64
FOX_WIDTH = FOX_HEADS * FOX_HEAD_DIM
MLA_HEADS = 12
MLA_NOPE = 64
MLA_ROPE = 32
MLA_V = 64
MLA_Q_LORA = 384
MLA_KV_LORA = 256
FFN_HIDDEN = 2816
ROPE_THETA = 10000.0
RMS_EPS = 1e-6

LANES = 128
HEAD_PAD = 16
PAIRS = FOX_HEADS // 2
VMEM_LIMIT = 56 * 1024 * 1024

F32 = jnp.float32
BF16 = jnp.bfloat16
NT_DIMS = (((1,), (1,)), ((), ()))
NEG_INF = float("-inf")
LOG2E = 1.4426950408889634


def _params(*sem):
    return pltpu.CompilerParams(dimension_semantics=sem, vmem_limit_bytes=VMEM_LIMIT)


def _rms(x, g):
    ms = jnp.mean(x * x, axis=-1, keepdims=True)
    return x * lax.rsqrt(ms + RMS_EPS) * g


def _dot(a, b):
    return jnp.dot(a, b, preferred_element_type=F32)


def _dot_nt(a, b):
    return lax.dot_general(a, b, NT_DIMS, preferred_element_type=F32)


def _dot_exact(a, b):
    return jnp.dot(a, b, preferred_element_type=F32, precision=lax.Precision.HIGHEST)


def _row_tile(t, pref):
    return pref if t % pref == 0 else t


FFN_CHUNK = FFN_HIDDEN // 2


def _ffn_kernel(*refs, merge, final):
    x_ref = refs[0]
    pos = 1
    if merge:
        mix_ref, cross_ref, wo_ref = refs[1:4]
        pos = 4
    g_ref, wg_ref, wu_ref, wd_ref = refs[pos:pos + 4]
    pos += 4
    if final:
        gf_ref = refs[pos]
        pos += 1
    o_ref = refs[pos]
    x = x_ref[...]
    if merge:
        nm = mix_ref.shape[1]
        x = x + _dot(mix_ref[...], wo_ref[:nm, :]) + _dot(cross_ref[...], wo_ref[nm:, :])
    xn = _rms(x, g_ref[...]).astype(BF16)
    acc = None
    for c0 in range(0, FFN_HIDDEN, FFN_CHUNK):
        cols = slice(c0, c0 + FFN_CHUNK)
        gate = _dot(xn, wg_ref[:, cols])
        up = _dot(xn, wu_ref[:, cols])
        h = (gate / (1.0 + jnp.exp(-gate)) * up).astype(BF16)
        part = _dot(h, wd_ref[cols, :])
        acc = part if acc is None else acc + part
    y = x + 0.5 * acc
    o_ref[...] = _rms(y, gf_ref[...]) if final else y


def _ffn_half(x, g, w_up, w_down, layer, idx, merge=None, final_g=None):
    t = x.shape[0]
    tm = _row_tile(t, 512)
    row = lambda n: pl.BlockSpec((tm, n), lambda m: (m, 0))
    once = pl.Buffered(1)
    full = lambda a: pl.BlockSpec(a.shape, lambda m: (0,) * a.ndim, pipeline_mode=once)
    in_specs = [row(D_MODEL)]
    args = [x]
    if merge is not None:
        mix, cross, w_out = merge
        in_specs += [row(mix.shape[1]), row(cross.shape[1]), full(w_out)]
        args += [mix, cross, w_out]
    in_specs += [
        full(g),
        pl.BlockSpec((None, None, D_MODEL, FFN_HIDDEN), lambda m: (layer, idx, 0, 0),
                     pipeline_mode=once),
        pl.BlockSpec((None, None, D_MODEL, FFN_HIDDEN), lambda m: (layer, idx, 0, 1),
                     pipeline_mode=once),
        pl.BlockSpec((None, None, FFN_HIDDEN, D_MODEL), lambda m: (layer, idx, 0, 0),
                     pipeline_mode=once),
    ]
    args += [g, w_up, w_up, w_down]
    if final_g is not None:
        in_specs.append(full(final_g))
        args.append(final_g)
    return pl.pallas_call(
        functools.partial(_ffn_kernel, merge=merge is not None, final=final_g is not None),
        grid=(t // tm,),
        in_specs=in_specs,
        out_specs=row(D_MODEL),
        out_shape=jax.ShapeDtypeStruct((t, D_MODEL), F32),
        compiler_params=_params("parallel"),
        name="ffn_half",
    )(*args)


def _log_sigmoid(f):
    return jnp.minimum(f, 0.0) - jnp.log(1.0 + jnp.exp(-jnp.abs(f)))


def _fox_proj_kernel(x_ref, g_ref, wqkv_ref, wf_ref, bf_ref, wx_ref,
                     q_ref, k_ref, v_ref, lf_ref, xq_ref, *, qscale):
    hn = _rms(x_ref[...], g_ref[...]).astype(BF16)
    qkv = _dot(hn, wqkv_ref[...])
    q_ref[...] = (qkv[:, :FOX_WIDTH] * qscale).astype(BF16)
    k_ref[...] = qkv[:, FOX_WIDTH:2 * FOX_WIDTH]
    v_ref[...] = qkv[:, 2 * FOX_WIDTH:]
    lf_ref[...] = _log_sigmoid(_dot(hn, wf_ref[...]) + bf_ref[...])
    xq_ref[...] = (_dot(hn, wx_ref[...]) * (X_HEAD_DIM ** -0.5)).astype(BF16)


BIAS_TERMS = 3
TERM_STRIDE = 16


def _split3(x):
    hi = x.astype(BF16)
    r1 = x - hi.astype(F32)
    mid = r1.astype(BF16)
    lo = (r1 - mid.astype(F32)).astype(BF16)
    return hi, mid, lo


def _fox_proj_t_kernel(x_ref, g_ref, wqk_ref, wkvt_ref, wf_ref, bf_ref, wx_ref, tri_ref,
                       q_ref, k_ref, kt_ref, vt_ref, lf_ref, cx_ref, xq_ref, carry_ref,
                       *, qscale, tiles_per_seq):
    hn = _rms(x_ref[...], g_ref[...]).astype(BF16)
    qk = _dot(hn, wqk_ref[...])
    q_ref[...] = (qk[:, :FOX_WIDTH] * qscale).astype(BF16)
    k_ref[...] = qk[:, FOX_WIDTH:].astype(BF16)
    kvt = _dot_nt(wkvt_ref[...], hn)
    kt_ref[0] = kvt[:FOX_WIDTH, :]
    vt_ref[0] = kvt[FOX_WIDTH:, :]
    lf = _log_sigmoid(_dot(hn, wf_ref[...]) + bf_ref[...])
    lf_ref[...] = lf

    @pl.when(pl.program_id(0) % tiles_per_seq == 0)
    def _():
        carry_ref[...] = jnp.zeros_like(carry_ref)

    tri = tri_ref[...]
    span = tri.shape[0]
    carry = carry_ref[...]
    pieces = []
    for r0 in range(0, lf.shape[0], span):
        c = carry
        for term in _split3(lf[r0:r0 + span, :]):
            c = c + _dot(tri, term)
        pieces.append(c)
        carry = c[-1:, :]
    carry_ref[...] = carry
    c2 = jnp.concatenate(pieces, axis=0) * LOG2E
    row = lax.broadcasted_iota(jnp.int32, (LANES, LANES), 0)
    col = lax.broadcasted_iota(jnp.int32, (LANES, LANES), 1)
    packed = None
    for i, term in enumerate(_split3(c2)):
        sel = jnp.where((col == row + TERM_STRIDE * i) & (row < TERM_STRIDE), 1.0, 0.0)
        t = _dot(term, sel.astype(BF16))
        packed = t if packed is None else packed + t
    cx_ref[...] = packed.astype(BF16)
    xq_ref[...] = (_dot(hn, wx_ref[...]) * (X_HEAD_DIM ** -0.5)).astype(BF16)


def _fox_proj(x, g, wqkv, wf, bf, wx, qscale):
    t = x.shape[0]
    tm = _row_tile(t, 512)
    row = lambda n: pl.BlockSpec((tm, n), lambda m: (m, 0))
    full = lambda a: pl.BlockSpec(a.shape, lambda m: (0,) * a.ndim)
    return pl.pallas_call(
        functools.partial(_fox_proj_kernel, qscale=qscale),
        grid=(t // tm,),
        in_specs=[row(D_MODEL), full(g), full(wqkv), full(wf), full(bf), full(wx)],
        out_specs=[row(FOX_WIDTH), row(FOX_WIDTH), row(FOX_WIDTH), row(LANES), row(X_WIDTH)],
        out_shape=[
            jax.ShapeDtypeStruct((t, FOX_WIDTH), BF16),
            jax.ShapeDtypeStruct((t, FOX_WIDTH), F32),
            jax.ShapeDtypeStruct((t, FOX_WIDTH), F32),
            jax.ShapeDtypeStruct((t, LANES), F32),
            jax.ShapeDtypeStruct((t, X_WIDTH), BF16),
        ],
        compiler_params=_params("parallel"),
        name="fox_proj",
    )(x, g, wqkv, wf, bf, wx)


def _fox_proj_t(x, g, wqk, wkvt, wf, bf, wx, qscale, b, s):
    t = x.shape[0]
    tm = _row_tile(s, 512)
    ns = s // tm
    span = _row_tile(tm, 256)
    tri = jnp.asarray(np.tril(np.ones((span, span), np.float32)), dtype=BF16)
    row = lambda n: pl.BlockSpec((tm, n), lambda m: (m, 0))
    full = lambda a: pl.BlockSpec(a.shape, lambda m: (0,) * a.ndim)
    tspec = pl.BlockSpec((1, FOX_WIDTH, tm), lambda m: (m // ns, 0, m % ns))
    return pl.pallas_call(
        functools.partial(_fox_proj_t_kernel, qscale=qscale, tiles_per_seq=ns),
        grid=(t // tm,),
        in_specs=[row(D_MODEL), full(g), full(wqk), full(wkvt), full(wf), full(bf), full(wx),
                  full(tri)],
        out_specs=[row(FOX_WIDTH), row(FOX_WIDTH), tspec, tspec, row(LANES), row(LANES),
                   row(X_WIDTH)],
        out_shape=[
            jax.ShapeDtypeStruct((t, FOX_WIDTH), BF16),
            jax.ShapeDtypeStruct((t, FOX_WIDTH), BF16),
            jax.ShapeDtypeStruct((b, FOX_WIDTH, s), F32),
            jax.ShapeDtypeStruct((b, FOX_WIDTH, s), F32),
            jax.ShapeDtypeStruct((t, LANES), F32),
            jax.ShapeDtypeStruct((t, LANES), BF16),
            jax.ShapeDtypeStruct((t, X_WIDTH), BF16),
        ],
        scratch_shapes=[pltpu.VMEM((1, LANES), F32)],
        compiler_params=_params("arbitrary"),
        name="fox_proj_t",
    )(x, g, wqk, wkvt, wf, bf, wx, tri)


def _bias_select(moves, sign):
    r = lax.broadcasted_iota(jnp.int32, (LANES, LANES), 0)
    c = lax.broadcasted_iota(jnp.int32, (LANES, LANES), 1)
    sel = jnp.zeros((LANES, LANES), F32)
    for head, dst in moves:
        for i in range(BIAS_TERMS):
            sel = jnp.where((r == TERM_STRIDE * i + head) & (c == dst + i), sign, sel)
    return sel.astype(BF16)


def _ones_lanes(shape, lo, n):
    lane = lax.broadcasted_iota(jnp.int32, shape, 1)
    return jnp.where((lane >= lo) & (lane < lo + n), 1.0, 0.0)


def _group8(x, op):
    parts = [x[r * 8:(r + 1) * 8, :] for r in range(x.shape[0] // 8)]
    while len(parts) > 1:
        parts = [op(parts[i], parts[i + 1]) for i in range(0, len(parts), 2)]
    return parts[0]


def _flash_kernel(*refs, tile, n_q, n_k, use_c):
    q_refs = refs[:n_q]
    k_refs = refs[n_q:n_q + n_k]
    vt_ref = refs[n_q + n_k]
    pos = n_q + n_k + 1
    if use_c:
        c_ref = refs[pos]
        pos += 1
    o_ref, kb_ref, vtb_ref, s_ref, p_ref, acc_ref = refs[pos:pos + 6]
    pair = pl.program_id(1)
    seq = vt_ref.shape[2]
    n_tiles = seq // tile
    chunk = 64
    nb = BIAS_TERMS

    for n, k_ref in enumerate(k_refs):
        kb_ref[:, n * LANES:(n + 1) * LANES] = k_ref[...].astype(BF16)
    vtb_ref[...] = vt_ref[0].astype(BF16)
    if use_c:
        feat = (_dot(c_ref[0], _bias_select([(2 * pair, 0), (2 * pair + 1, nb)], -1.0))
                + _ones_lanes((1, LANES), 2 * nb, nb))
        kb_ref[:, n_k * LANES:] = feat.astype(BF16)

    kd_q = n_q * LANES
    lane = lax.broadcasted_iota(jnp.int32, (1, kd_q), 1)
    if n_q == 1:
        own = (lane < FOX_HEAD_DIM, lane >= FOX_HEAD_DIM)
    else:
        own = ((lane < MLA_NOPE) | ((lane >= LANES) & (lane < LANES + MLA_ROPE)),
               ((lane >= MLA_NOPE) & (lane < LANES))
               | ((lane >= LANES + MLA_ROPE) & (lane < LANES + 2 * MLA_ROPE)))

    def q_operands(i):
        rows = slice(i * tile, (i + 1) * tile)
        q = (jnp.concatenate([r[rows, :] for r in q_refs], axis=1) if n_q > 1
             else q_refs[0][rows, :])
        zero = jnp.zeros_like(q)
        qh = [jnp.where(own[h], q, zero) for h in range(2)]
        if use_c:
            cx = c_ref[0, rows, :]
            for h in range(2):
                feat = (_dot(cx, _bias_select([(2 * pair + h, 2 * nb)], 1.0))
                        + _ones_lanes((1, LANES), h * nb, nb))
                qh[h] = jnp.concatenate([qh[h], feat.astype(BF16)], axis=1)
        return qh

    krow = lax.broadcasted_iota(jnp.int32, (chunk, tile), 0)
    qcol = lax.broadcasted_iota(jnp.int32, (chunk, tile), 1)

    def scores(slot, qh, j):
        kblk = kb_ref[j * tile:(j + 1) * tile, :]
        for h in range(2):
            s_ref[slot, h] = _dot_nt(kblk, qh[h])

    def values(slot, par, j, alphas):
        vtblk = vtb_ref[:, j * tile:(j + 1) * tile]
        for h in range(2):
            pv = _dot(vtblk, p_ref[slot, h])
            acc_ref[par, h] = pv if alphas is None else acc_ref[par, h] * alphas[h] + pv

    def softmax(slot, stats, diagonal):
        out = []
        alphas = []
        for h in range(2):
            mx = None
            for c in range(tile // chunk):
                rows = slice(c * chunk, (c + 1) * chunk)
                blk = s_ref[slot, h, rows, :]
                if diagonal:
                    blk = jnp.where(krow + c * chunk > qcol, NEG_INF, blk)
                    s_ref[slot, h, rows, :] = blk
                cm = _group8(blk, jnp.maximum)
                mx = cm if mx is None else jnp.maximum(mx, cm)
            m_new = jnp.max(mx, axis=0, keepdims=True)
            if stats is not None:
                m_old = stats[h][0]
                m_new = jnp.maximum(m_old, m_new)
                alphas.append(jnp.exp2(m_old - m_new))
            ls = None
            for c in range(tile // chunk):
                rows = slice(c * chunk, (c + 1) * chunk)
                p = jnp.exp2(s_ref[slot, h, rows, :] - m_new)
                ps = _group8(p, jnp.add)
                ls = ps if ls is None else ls + ps
                p_ref[slot, h, rows, :] = p.astype(BF16)
            l = jnp.sum(ls, axis=0, keepdims=True)
            if stats is not None:
                l = stats[h][1] * alphas[-1] + l
            out.append((m_new, l))
        return out, (None if stats is None else alphas)

    def finish(i, stats):
        par = i % 2
        row = lax.broadcasted_iota(jnp.int32, (LANES, 1), 0)
        o_t = jnp.where(row < FOX_HEAD_DIM, acc_ref[par, 0] / stats[0][1],
                        acc_ref[par, 1] / stats[1][1])
        o_ref[i * tile:(i + 1) * tile, :] = o_t.T.astype(o_ref.dtype)

    blocks = [(i, j) for i in range(n_tiles) for j in range(i + 1)]
    qh_of = {0: q_operands(0)}
    scores(0, qh_of[0], 0)
    stats = None
    pending = None
    for n, (i, j) in enumerate(blocks):
        if n + 1 < len(blocks):
            i2, j2 = blocks[n + 1]
            if i2 not in qh_of:
                qh_of[i2] = q_operands(i2)
            scores((n + 1) % 2, qh_of[i2], j2)
        if pending is not None:
            slot_p, i_p, j_p, alphas_p, stats_p = pending
            values(slot_p, i_p % 2, j_p, alphas_p)
            if j_p == i_p:
                finish(i_p, stats_p)
        stats, alphas = softmax(n % 2, None if j == 0 else stats, j == i)
        pending = (n % 2, i, j, alphas, stats)
    slot_p, i_p, j_p, alphas_p, stats_p = pending
    values(slot_p, i_p % 2, j_p, alphas_p)
    finish(i_p, stats_p)


FLASH_TILE = 512


def _flash(qs, ks, vt, b, s, c=None):
    tile = _row_tile(s, FLASH_TILE)
    use_c = c is not None
    kd = LANES * (len(ks) + (1 if use_c else 0))
    rows = pl.BlockSpec((s, LANES), lambda i, p: (i, p))
    shared = pl.BlockSpec((s, LANES), lambda i, p: (i, 0))
    in_specs = [rows] * len(qs) + [rows if k.shape[1] > LANES else shared for k in ks]
    in_specs.append(pl.BlockSpec((1, LANES, s), lambda i, p: (i, p, 0)))
    args = list(qs) + list(ks) + [vt]
    if use_c:
        in_specs.append(pl.BlockSpec((1, s, LANES), lambda i, p: (i, 0, 0)))
        args.append(c)
    kern = functools.partial(_flash_kernel, tile=tile, n_q=len(qs), n_k=len(ks), use_c=use_c)
    return pl.pallas_call(
        kern,
        grid=(b, PAIRS),
        in_specs=in_specs,
        out_specs=rows,
        out_shape=jax.ShapeDtypeStruct((b * s, PAIRS * LANES), BF16),
        scratch_shapes=[pltpu.VMEM((s, kd), BF16), pltpu.VMEM((LANES, s), BF16),
                        pltpu.VMEM((2, 2, tile, tile), F32),
                        pltpu.VMEM((2, 2, tile, tile), BF16),
                        pltpu.VMEM((2, 2, LANES, tile), F32)],
        compiler_params=_params("parallel", "parallel"),
        name="flash_mla" if len(qs) > 1 else "flash_fox",
    )(*args)


def _cross_kernel(q_ref, k_ref, v_ref, o_ref):
    q = q_ref[0].astype(BF16)
    kb = k_ref[0].astype(BF16)
    vb = v_ref[0].astype(BF16)
    lane = lax.broadcasted_iota(jnp.int32, (1, LANES), 1)
    first = lane < X_HEAD_DIM
    zero = jnp.zeros_like(q)
    outs = []
    for qh in (jnp.where(first, q, zero), jnp.where(first, zero, q)):
        s = _dot_nt(qh, kb)
        m = jnp.max(s, axis=1, keepdims=True)
        p = jnp.exp(s - m)
        l = jnp.sum(p, axis=1, keepdims=True)
        outs.append(_dot(p.astype(BF16), vb) / l)
    o_ref[0] = jnp.where(first, outs[0], outs[1]).astype(o_ref.dtype)


def _cross(xq, mk, mv):
    b, t, _ = xq.shape
    m = mk.shape[1]
    tq = _row_tile(t, 1024)
    qspec = pl.BlockSpec((1, tq, LANES), lambda i, p, j: (i, j, p))
    kspec = pl.BlockSpec((1, m, LANES), lambda i, p, j: (i, 0, p))
    return pl.pallas_call(
        _cross_kernel,
        grid=(b, X_WIDTH // LANES, t // tq),
        in_specs=[qspec, kspec, kspec],
        out_specs=qspec,
        out_shape=jax.ShapeDtypeStruct((b, t, X_WIDTH), BF16),
        compiler_params=_params("parallel", "parallel", "arbitrary"),
        name="cross_attn",
    )(xq, mk, mv)


def _cross_t_kernel(q_ref, kt_ref, vt_ref, o_ref):
    lane = lax.broadcasted_iota(jnp.int32, (1, LANES), 1)
    first = lane < X_HEAD_DIM
    for g in range(q_ref.shape[0]):
        q = q_ref[g].astype(BF16)
        halves = []
        for p in range(X_WIDTH // LANES):
            cols = slice(p * LANES, (p + 1) * LANES)
            qp = q[:, cols]
            ktb = kt_ref[g, cols, :].astype(BF16)
            vtb = vt_ref[g, cols, :].astype(BF16)
            zero = jnp.zeros_like(qp)
            outs = []
            for qh in (jnp.where(first, qp, zero), jnp.where(first, zero, qp)):
                s = _dot(qh, ktb)
                mx = jnp.max(s, axis=1, keepdims=True)
                pr = jnp.exp(s - mx)
                l = jnp.sum(pr, axis=1, keepdims=True)
                outs.append(_dot_nt(pr.astype(BF16), vtb) / l)
            halves.append(jnp.where(first, outs[0], outs[1]))
        o_ref[g] = jnp.concatenate(halves, axis=1).astype(o_ref.dtype)


CROSS_T_BATCHES = 8


def _cross_t(xq, mkt, mvt, layer):
    b, t, _ = xq.shape
    m = mkt.shape[3]
    g = CROSS_T_BATCHES if b % CROSS_T_BATCHES == 0 else 1
    qspec = pl.BlockSpec((g, t, X_WIDTH), lambda i: (i, 0, 0))
    kspec = pl.BlockSpec((None, g, X_WIDTH, m), lambda i: (layer, i, 0, 0))
    return pl.pallas_call(
        _cross_t_kernel,
        grid=(b // g,),
        in_specs=[qspec, kspec, kspec],
        out_specs=qspec,
        out_shape=jax.ShapeDtypeStruct((b, t, X_WIDTH), BF16),
        compiler_params=_params("parallel"),
        name="cross_attn_t",
    )(xq, mkt, mvt)


def _norm_proj_kernel(x_ref, g_ref, w_ref, o_ref):
    o_ref[...] = _dot(_rms(x_ref[...], g_ref[...]).astype(BF16), w_ref[...])


def _norm_proj(x, g, w):
    t = x.shape[0]
    n = w.shape[1]
    tm = _row_tile(t, 512)
    return pl.pallas_call(
        _norm_proj_kernel,
        grid=(t // tm,),
        in_specs=[pl.BlockSpec((tm, D_MODEL), lambda m: (m, 0)),
                  pl.BlockSpec((1, D_MODEL), lambda m: (0, 0)),
                  pl.BlockSpec(w.shape, lambda m: (0, 0))],
        out_specs=pl.BlockSpec((tm, n), lambda m: (m, 0)),
        out_shape=jax.ShapeDtypeStruct((t, n), F32),
        compiler_params=_params("parallel"),
        name="norm_proj",
    )(x, g, w)


def _mla_proj_kernel(*refs, qscale, with_kv):
    (x_ref, g_ref, wqa_ref, wkva_ref, wkr_ref, wkrr_ref, wx_ref,
     qg_ref, wqn_ref, wqr_ref, wqrr_ref, kvg_ref) = refs[:12]
    pos = 12
    if with_kv:
        wuk_ref, wuvt_ref = refs[pos:pos + 2]
        pos += 2
    cos_ref, sin_ref = refs[pos:pos + 2]
    qn_ref, qr_ref, c_ref, kr_ref, xq_ref = refs[pos + 2:pos + 7]
    hn = _rms(x_ref[...], g_ref[...]).astype(BF16)
    cos = cos_ref[...]
    sin = sin_ref[...]
    qa = _rms(_dot(hn, wqa_ref[...]), qg_ref[...]).astype(BF16)
    qn_ref[...] = (_dot(qa, wqn_ref[...]) * qscale).astype(BF16)
    qr = _dot(qa, wqr_ref[...])
    qrr = _dot(qa, wqrr_ref[...])
    for p in range(PAIRS):
        sl = slice(p * LANES, (p + 1) * LANES)
        qr_ref[:, sl] = ((qr[:, sl] * cos + qrr[:, sl] * sin) * qscale).astype(BF16)
    c = _rms(_dot(hn, wkva_ref[...]), kvg_ref[...])
    c_ref[...] = c
    if with_kv:
        kn_ref, vt_ref = refs[pos + 7:pos + 9]
        cb = c.astype(BF16)
        kn_ref[...] = _dot(cb, wuk_ref[...]).astype(BF16)
        vt_ref[0] = _dot_nt(wuvt_ref[...], cb).astype(BF16)
    kr = _dot(hn, wkr_ref[...]) * cos + _dot(hn, wkrr_ref[...]) * sin
    kr_ref[...] = kr[:, :MLA_ROPE]
    xq_ref[...] = (_dot(hn, wx_ref[...]) * (X_HEAD_DIM ** -0.5)).astype(BF16)


def _mla_proj(x, g, w, cos, sin, qscale, kv_w=None, b=None, s=None):
    t = x.shape[0]
    with_kv = kv_w is not None
    tm = _row_tile(s if with_kv else t, 512)
    ntab = cos.shape[0] // tm
    row = lambda n: pl.BlockSpec((tm, n), lambda m: (m, 0))
    full = lambda a: pl.BlockSpec(a.shape, lambda m: (0,) * a.ndim)
    tab = pl.BlockSpec((tm, LANES), lambda m: (m % ntab, 0))
    wide = PAIRS * LANES
    w = list(w) + (list(kv_w) if with_kv else [])
    out_specs = [row(wide), row(wide), row(MLA_KV_LORA), row(MLA_ROPE), row(X_WIDTH)]
    out_shape = [
        jax.ShapeDtypeStruct((t, wide), BF16),
        jax.ShapeDtypeStruct((t, wide), BF16),
        jax.ShapeDtypeStruct((t, MLA_KV_LORA), F32),
        jax.ShapeDtypeStruct((t, MLA_ROPE), F32),
        jax.ShapeDtypeStruct((t, X_WIDTH), BF16),
    ]
    if with_kv:
        ns = s // tm
        out_specs += [row(wide), pl.BlockSpec((1, wide, tm), lambda m: (m // ns, 0, m % ns))]
        out_shape += [jax.ShapeDtypeStruct((t, wide), BF16),
                      jax.ShapeDtypeStruct((b, wide, s), BF16)]
    return pl.pallas_call(
        functools.partial(_mla_proj_kernel, qscale=qscale, with_kv=with_kv),
        grid=(t // tm,),
        in_specs=[row(D_MODEL), full(g)] + [full(a) for a in w] + [tab, tab],
        out_specs=out_specs,
        out_shape=out_shape,
        compiler_params=_params("parallel"),
        name="mla_proj_kv" if with_kv else "mla_proj",
    )(x, g, *w, cos, sin)


FOX_PAGES_PER_STEP = 16
MLA_PAGES_PER_STEP = 32
PAGE_GROUPS = 4


def _pages_per_step(n_pages, pref):
    while n_pages % pref:
        pref //= 2
    return pref


def _online_step(state, s_parts, v_parts, v_is_transposed):
    m, l, acc = state
    n = len(s_parts)
    per = n // PAGE_GROUPS if n % PAGE_GROUPS == 0 else n
    partials = []
    for g0 in range(0, n, per):
        s = jnp.concatenate(s_parts[g0:g0 + per], axis=1) if per > 1 else s_parts[g0]
        mg = jnp.max(s, axis=1, keepdims=True)
        p = jnp.exp(s - mg)
        lg = jnp.sum(p, axis=1, keepdims=True)
        ag = None
        for r in range(per):
            pb = p[:, r * LANES:(r + 1) * LANES].astype(BF16)
            vb = v_parts[g0 + r]
            t = _dot_nt(pb, vb) if v_is_transposed else _dot(pb, vb)
            ag = t if ag is None else ag + t
        partials.append((mg, lg, ag))
    m_new = m
    for mg, _, _ in partials:
        m_new = jnp.maximum(m_new, mg)
    alpha = jnp.exp(m - m_new)
    l = l * alpha
    acc = acc * alpha
    for mg, lg, ag in partials:
        w = jnp.exp(mg - m_new)
        l = l + lg * w
        acc = acc + ag * w
    return m_new, l, acc


def _head_diag(acc, t_new, width):
    n = acc.shape[1]
    rowh = lax.broadcasted_iota(jnp.int32, (HEAD_PAD, n), 0)
    laneh = lax.broadcasted_iota(jnp.int32, (HEAD_PAD, n), 1) // width
    orow = lax.broadcasted_iota(jnp.int32, (t_new, n), 0)
    out = jnp.zeros((t_new, n), F32)
    for i in range(t_new):
        blk = acc[i * HEAD_PAD:(i + 1) * HEAD_PAD, :]
        r = jnp.sum(jnp.where(rowh == laneh, blk, 0.0), axis=0, keepdims=True)
        out = jnp.where(orow == i, r, out)
    return out


def _fox_dec_kernel(pt_ref, qb_ref, kn_ref, vn_ref, lft_ref, sufm_ref, *rest, pb, t_new):
    kt_refs = rest[:pb]
    vt_refs = rest[pb:2 * pb]
    lf_refs = rest[2 * pb:3 * pb]
    o_ref, m_ref, l_ref, acc_ref, run_ref, pad_ref, lfp_ref = rest[3 * pb:]
    step = pl.program_id(1)
    qb = qb_ref[0]
    lft = lft_ref[0]
    lane = lax.broadcasted_iota(jnp.int32, (HEAD_PAD, LANES), 1)
    cn_cols = []
    run = jnp.zeros((HEAD_PAD, 1), F32)
    for i in range(t_new):
        run = run + lft[:, i:i + 1]
        cn_cols.append(run)
    cn_col = jnp.concatenate(cn_cols, axis=0)

    @pl.when(step == 0)
    def _():
        cn_lanes = jnp.zeros((HEAD_PAD, LANES), F32)
        for i in range(t_new):
            cn_lanes = jnp.where(lane == i, cn_cols[i], cn_lanes)
        pad_ref[...] = jnp.zeros_like(pad_ref)
        pad_ref[0:t_new, :] = kn_ref[0]
        kpad = pad_ref[...].astype(BF16)
        pad_ref[0:t_new, :] = vn_ref[0]
        vpad = pad_ref[...].astype(BF16)
        s = _dot_nt(qb, kpad)
        parts = []
        for i in range(t_new):
            blk = s[i * HEAD_PAD:(i + 1) * HEAD_PAD, :] + (cn_cols[i] - cn_lanes)
            parts.append(jnp.where(lane <= i, blk, NEG_INF))
        s = jnp.concatenate(parts, axis=0)
        m = jnp.max(s, axis=1, keepdims=True)
        p = jnp.exp(s - m)
        m_ref[...] = m
        l_ref[...] = jnp.sum(p, axis=1, keepdims=True)
        acc_ref[...] = _dot(p.astype(BF16), vpad)
        run_ref[...] = jnp.zeros_like(run_ref)
        lfp_ref[...] = jnp.zeros_like(lfp_ref)

    run = run_ref[...]
    s_parts, v_parts = [], []
    for r in range(pb):
        lfp_ref[r * HEAD_PAD:r * HEAD_PAD + FOX_HEADS, :] = lf_refs[r][0]
    lf_all = lfp_ref[...]
    suf_all = _dot_exact(lf_all, sufm_ref[...])
    tot_all = jnp.sum(lf_all, axis=1, keepdims=True)
    for r in range(pb):
        heads = slice(r * HEAD_PAD, (r + 1) * HEAD_PAD)
        suf = suf_all[heads, :] + run
        run = run + tot_all[heads, :]
        ktb = kt_refs[r][0, 0].reshape(FOX_WIDTH, PAGE_SIZE).astype(BF16)
        s = _dot(qb, ktb)
        bias = jnp.concatenate([suf] * t_new, axis=0) + cn_col
        s_parts.append(s + bias)
        v_parts.append(vt_refs[r][0, 0].reshape(FOX_WIDTH, PAGE_SIZE).astype(BF16))
    run_ref[...] = run
    m, l, acc = _online_step((m_ref[...], l_ref[...], acc_ref[...]), s_parts, v_parts, True)
    m_ref[...] = m
    l_ref[...] = l
    acc_ref[...] = acc

    @pl.when(step == pl.num_programs(1) - 1)
    def _():
        o_ref[0] = _head_diag(acc / l, t_new, FOX_HEAD_DIM)


def _fox_decode(page_table, qblk, k_new, v_new, lft_new, cache_kt, cache_vt, cache_lf):
    bd, n_pages = page_table.shape
    t_new = k_new.shape[1]
    rows = t_new * HEAD_PAD
    pb = _pages_per_step(n_pages, FOX_PAGES_PER_STEP)
    sufm = jnp.asarray(np.tril(np.ones((PAGE_SIZE, PAGE_SIZE), np.float32), -1))

    def page(r):
        return lambda b, s, pt: pt[b, n_pages - 1 - (s * pb + r)]

    kv_specs = [pl.BlockSpec((1, 1, FOX_HEADS, FOX_HEAD_DIM, PAGE_SIZE),
                             (lambda b, s, pt, f=page(r): (0, f(b, s, pt), 0, 0, 0)))
                for r in range(pb)]
    lf_specs = [pl.BlockSpec((1, FOX_HEADS, PAGE_SIZE),
                             (lambda b, s, pt, f=page(r): (f(b, s, pt), 0, 0)))
                for r in range(pb)]
    per_b = lambda shape: pl.BlockSpec((1,) + shape, lambda b, s, pt: (b, 0, 0))
    grid_spec = pltpu.PrefetchScalarGridSpec(
        num_scalar_prefetch=1,
        grid=(bd, n_pages // pb),
        in_specs=[per_b((rows, FOX_WIDTH)), per_b((t_new, FOX_WIDTH)), per_b((t_new, FOX_WIDTH)),
                  per_b((HEAD_PAD, LANES)),
                  pl.BlockSpec((PAGE_SIZE, PAGE_SIZE), lambda b, s, pt: (0, 0))]
        + kv_specs + kv_specs + lf_specs,
        out_specs=per_b((t_new, FOX_WIDTH)),
        scratch_shapes=[pltpu.VMEM((rows, 1), F32), pltpu.VMEM((rows, 1), F32),
                        pltpu.VMEM((rows, FOX_WIDTH), F32), pltpu.VMEM((HEAD_PAD, 1), F32),
                        pltpu.VMEM((PAGE_SIZE, FOX_WIDTH), F32),
                        pltpu.VMEM((pb * HEAD_PAD, PAGE_SIZE), F32)],
    )
    return pl.pallas_call(
        functools.partial(_fox_dec_kernel, pb=pb, t_new=t_new),
        grid_spec=grid_spec,
        out_shape=jax.ShapeDtypeStruct((bd, t_new, FOX_WIDTH), F32),
        compiler_params=_params("parallel", "arbitrary"),
        name="fox_decode",
    )(page_table, qblk, k_new, v_new, lft_new, sufm,
      *([cache_kt] * pb), *([cache_vt] * pb), *([cache_lf] * pb))


def _mla_dec_kernel(pt_ref, qn_ref, qr_ref, cn_ref, krn_ref, wuk_ref, wuv_ref, *rest, pb, t_new):
    c_refs = rest[:pb]
    kr_refs = rest[pb:2 * pb]
    o_ref, m_ref, l_ref, acc_ref, ql_ref, cpad_ref, krt_ref = rest[2 * pb:]
    step = pl.program_id(1)
    lane = lax.broadcasted_iota(jnp.int32, (HEAD_PAD, LANES), 1)
    qr = qr_ref[0]

    @pl.when(step == 0)
    def _():
        ql = _dot(qn_ref[0], wuk_ref[...]).astype(BF16)
        ql_ref[...] = ql
        cpad_ref[...] = jnp.zeros_like(cpad_ref)
        cpad_ref[0:t_new, :] = cn_ref[0]
        cpad = cpad_ref[...].astype(BF16)
        s = _dot_nt(ql, cpad) + _dot_nt(qr, krn_ref[0])
        parts = [jnp.where(lane <= i, s[i * HEAD_PAD:(i + 1) * HEAD_PAD, :], NEG_INF)
                 for i in range(t_new)]
        s = jnp.concatenate(parts, axis=0)
        m = jnp.max(s, axis=1, keepdims=True)
        p = jnp.exp(s - m)
        m_ref[...] = m
        l_ref[...] = jnp.sum(p, axis=1, keepdims=True)
        acc_ref[...] = _dot(p.astype(BF16), cpad)
        krt_ref[...] = jnp.zeros_like(krt_ref)

    ql = ql_ref[...]
    s_parts, v_parts = [], []
    for r in range(pb):
        cb = c_refs[r][0, 0].astype(BF16)
        krt_ref[0:MLA_ROPE, :] = kr_refs[r][0, 0].astype(BF16)
        s_parts.append(_dot_nt(ql, cb) + _dot(qr, krt_ref[...]))
        v_parts.append(cb)
    m, l, acc = _online_step((m_ref[...], l_ref[...], acc_ref[...]), s_parts, v_parts, False)
    m_ref[...] = m
    l_ref[...] = l
    acc_ref[...] = acc

    @pl.when(step == pl.num_programs(1) - 1)
    def _():
        full = _dot((acc / l).astype(BF16), wuv_ref[...])
        o_ref[0] = _head_diag(full, t_new, MLA_V)


def _mla_decode(page_table, qnblk, qrblk, c_new, kr_new, wuk_t, wuv, cache_c, cache_krt):
    bd, n_pages = page_table.shape
    t_new = c_new.shape[1]
    rows = t_new * HEAD_PAD
    pb = _pages_per_step(n_pages, MLA_PAGES_PER_STEP)

    def page(r):
        return lambda b, s, pt: pt[b, s * pb + r]

    c_specs = [pl.BlockSpec((1, 1, PAGE_SIZE, MLA_KV_LORA),
                            (lambda b, s, pt, f=page(r): (0, f(b, s, pt), 0, 0)))
               for r in range(pb)]
    kr_specs = [pl.BlockSpec((1, 1, MLA_ROPE, PAGE_SIZE),
                             (lambda b, s, pt, f=page(r): (0, f(b, s, pt), 0, 0)))
                for r in range(pb)]
    per_b = lambda shape: pl.BlockSpec((1,) + shape, lambda b, s, pt: (b, 0, 0))
    full = lambda a: pl.BlockSpec(a.shape, lambda b, s, pt: (0,) * a.ndim)
    wide = MLA_HEADS * MLA_V
    grid_spec = pltpu.PrefetchScalarGridSpec(
        num_scalar_prefetch=1,
        grid=(bd, n_pages // pb),
        in_specs=[per_b((rows, MLA_HEADS * MLA_NOPE)), per_b((rows, LANES)),
                  per_b((t_new, MLA_KV_LORA)), per_b((PAGE_SIZE, LANES)),
                  full(wuk_t), full(wuv)] + c_specs + kr_specs,
        out_specs=per_b((t_new, wide)),
        scratch_shapes=[pltpu.VMEM((rows, 1), F32), pltpu.VMEM((rows, 1), F32),
                        pltpu.VMEM((rows, MLA_KV_LORA), F32),
                        pltpu.VMEM((rows, MLA_KV_LORA), BF16),
                        pltpu.VMEM((PAGE_SIZE, MLA_KV_LORA), F32),
                        pltpu.VMEM((LANES, PAGE_SIZE), BF16)],
    )
    return pl.pallas_call(
        functools.partial(_mla_dec_kernel, pb=pb, t_new=t_new),
        grid_spec=grid_spec,
        out_shape=jax.ShapeDtypeStruct((bd, t_new, wide), F32),
        compiler_params=_params("parallel", "arbitrary"),
        name="mla_decode",
    )(page_table, qnblk, qrblk, c_new, kr_new, wuk_t, wuv,
      *([cache_c] * pb), *([cache_krt] * pb))


def _rope_tables(pos):
    half = MLA_ROPE // 2
    inv = ROPE_THETA ** (-jnp.arange(half, dtype=F32) / half)
    ang = pos.astype(F32)[:, None] * inv
    reps = (LANES // 2) // half
    pad = jnp.zeros((pos.shape[0], LANES // 2), F32)
    cos = jnp.concatenate([jnp.tile(jnp.cos(ang), (1, reps)), pad], axis=1)
    sin = jnp.concatenate([jnp.tile(jnp.sin(ang), (1, reps)), pad], axis=1)
    return cos, sin


def _rot_cols(w):
    half = MLA_ROPE // 2
    return jnp.concatenate([-w[..., half:], w[..., :half]], axis=-1)


def _pair_rope_cols(w):
    r = w.shape[0]
    w = w.reshape(r, PAIRS, 2 * MLA_ROPE)
    return jnp.pad(w, ((0, 0), (0, 0), (0, LANES - 2 * MLA_ROPE))).reshape(r, PAIRS * LANES)


def _block_diag_rows(q, width):
    bd, t, n = q.shape
    rowh = np.arange(t * HEAD_PAD) % HEAD_PAD
    mask = jnp.asarray(rowh[:, None] == (np.arange(n) // width)[None, :])
    rep = jnp.repeat(q, HEAD_PAD, axis=1)
    return jnp.where(mask[None], rep, jnp.zeros_like(rep))


def kernel(x_prompt, x_sample, cache_fox_k, cache_fox_v, cache_fox_logf, cache_mla_ckv,
           cache_mla_krope, cache_mem_k, cache_mem_v, page_table, mem_prompt, norm_gains,
           ffn_w_up, ffn_w_down, fox_w_in, fox_b_f, mla_w_in, mla_q_norm, mla_w_q_b,
           mla_kv_norm, mla_w_kv_b, mem_norm, w_mem_kv, w_out, final_norm):
    b, s, d = x_prompt.shape
    bd, t_new, _ = x_sample.shape
    n_pages = page_table.shape[1]
    mem_len = mem_prompt.shape[1]
    depth = norm_gains.shape[0]
    xp = x_prompt.reshape(b * s, d)
    xs = x_sample.reshape(bd * t_new, d)

    w_up = ffn_w_up.astype(BF16)
    w_down = ffn_w_down.astype(BF16)
    w_out_b = w_out.astype(BF16)
    w_mem_b = w_mem_kv.astype(BF16)
    gains = norm_gains.reshape(depth, 3, 1, d)
    mem2d = mem_prompt.reshape(b * mem_len, d)
    mkt_s = jnp.transpose(cache_mem_k, (0, 1, 3, 4, 2)).reshape(depth, bd, X_WIDTH, mem_len)
    mvt_s = jnp.transpose(cache_mem_v, (0, 1, 3, 4, 2)).reshape(depth, bd, X_WIDTH, mem_len)

    outs = {}
    for layer in range(depth):
        j = layer // 2
        xp = _ffn_half(xp, gains[layer, 0], w_up, w_down, layer, 0)
        xs = _ffn_half(xs, gains[layer, 0], w_up, w_down, layer, 0)

        mkv = _norm_proj(mem2d, mem_norm[layer].reshape(1, d), w_mem_b[layer])
        mk = mkv[:, :X_WIDTH].reshape(b, mem_len, X_WIDTH)
        mv = mkv[:, X_WIDTH:].reshape(b, mem_len, X_WIDTH)
        outs.setdefault("mk", []).append(mk.reshape(b, mem_len, X_HEADS, X_HEAD_DIM))
        outs.setdefault("mv", []).append(mv.reshape(b, mem_len, X_HEADS, X_HEAD_DIM))

        if layer % 2 == 0:
            w_in = fox_w_in[j]
            wqkv = w_in[:, :3 * FOX_WIDTH].astype(BF16)
            wf = jnp.pad(w_in[:, 3 * FOX_WIDTH:3 * FOX_WIDTH + FOX_HEADS],
                         ((0, 0), (0, LANES - FOX_HEADS))).astype(BF16)
            bf = jnp.pad(fox_b_f[j], (0, LANES - FOX_HEADS)).reshape(1, LANES)
            wx = w_in[:, 3 * FOX_WIDTH + FOX_HEADS:].astype(BF16)
            g1 = gains[layer, 1]

            wqk = w_in[:, :2 * FOX_WIDTH].astype(BF16)
            wkvt = jnp.transpose(w_in[:, FOX_WIDTH:3 * FOX_WIDTH]).astype(BF16)
            q, kb, kt, vt, lf, cx, xq_p = _fox_proj_t(
                xp, g1, wqk, wkvt, wf, bf, wx, FOX_HEAD_DIM ** -0.5 * LOG2E, b, s)
            mix_p = _flash([q], [kb], vt, b, s, cx.reshape(b, s, LANES))
            heads_t = (0, 1, 4, 2, 3)
            outs["fk_p"] = jnp.transpose(kt.reshape(1, b, FOX_HEADS, FOX_HEAD_DIM, s), heads_t)
            outs["fv_p"] = jnp.transpose(vt.reshape(1, b, FOX_HEADS, FOX_HEAD_DIM, s), heads_t)
            outs["fl_p"] = lf[:, :FOX_HEADS].reshape(1, b, s, FOX_HEADS)

            q, k, v, lf, xq_s = _fox_proj(xs, g1, wqkv, wf, bf, wx, FOX_HEAD_DIM ** -0.5)
            outs["fk_s"] = k.reshape(1, bd, t_new, FOX_HEADS, FOX_HEAD_DIM)
            outs["fv_s"] = v.reshape(1, bd, t_new, FOX_HEADS, FOX_HEAD_DIM)
            outs["fl_s"] = lf[:, :FOX_HEADS].reshape(1, bd, t_new, FOX_HEADS)
            qblk = _block_diag_rows(q.reshape(bd, t_new, FOX_WIDTH), FOX_HEAD_DIM)
            lft = jnp.transpose(lf.reshape(bd, t_new, LANES)[:, :, :HEAD_PAD], (0, 2, 1))
            lft = jnp.pad(lft, ((0, 0), (0, 0), (0, LANES - t_new)))
            cache_kt = jnp.transpose(cache_fox_k, (0, 1, 3, 4, 2))
            cache_vt = jnp.transpose(cache_fox_v, (0, 1, 3, 4, 2))
            cache_lf = jnp.transpose(cache_fox_logf[j], (0, 2, 1))
            mix_s = _fox_decode(page_table, qblk, k.reshape(bd, t_new, FOX_WIDTH),
                                v.reshape(bd, t_new, FOX_WIDTH), lft,
                                cache_kt[j:j + 1], cache_vt[j:j + 1], cache_lf)
            mix_s = mix_s.reshape(bd * t_new, FOX_WIDTH).astype(BF16)
        else:
            w_in = mla_w_in[j]
            o1 = MLA_Q_LORA
            o2 = o1 + MLA_KV_LORA
            o3 = o2 + MLA_ROPE
            wkr = w_in[:, o2:o3]
            lane_pad = ((0, 0), (0, LANES - MLA_ROPE))
            wqb = mla_w_q_b[j]
            wkvb = mla_w_kv_b[j]
            weights = [
                w_in[:, :o1].astype(BF16),
                w_in[:, o1:o2].astype(BF16),
                jnp.pad(wkr, lane_pad).astype(BF16),
                jnp.pad(_rot_cols(wkr), lane_pad).astype(BF16),
                w_in[:, o3:].astype(BF16),
                mla_q_norm[j].reshape(1, MLA_Q_LORA),
                wqb[:, :, :MLA_NOPE].reshape(MLA_Q_LORA, MLA_HEADS * MLA_NOPE).astype(BF16),
                _pair_rope_cols(wqb[:, :, MLA_NOPE:]).astype(BF16),
                _pair_rope_cols(_rot_cols(wqb[:, :, MLA_NOPE:])).astype(BF16),
                mla_kv_norm[j].reshape(1, MLA_KV_LORA),
            ]
            wuk = wkvb[:, :, :MLA_NOPE].reshape(MLA_KV_LORA, MLA_HEADS * MLA_NOPE).astype(BF16)
            wuv = wkvb[:, :, MLA_NOPE:].reshape(MLA_KV_LORA, MLA_HEADS * MLA_V).astype(BF16)
            wuv_t = jnp.transpose(wkvb[:, :, MLA_NOPE:], (1, 2, 0)).reshape(
                MLA_HEADS * MLA_V, MLA_KV_LORA).astype(BF16)
            mla_scale = (MLA_NOPE + MLA_ROPE) ** -0.5
            g1 = gains[layer, 1]
            cos_p, sin_p = _rope_tables(jnp.arange(s, dtype=jnp.int32))
            pos_s = n_pages * PAGE_SIZE + jnp.arange(t_new, dtype=jnp.int32)
            cos_s, sin_s = _rope_tables(jnp.tile(pos_s, bd))

            qn, qr, c, kr, xq_p, kn, vt = _mla_proj(
                xp, g1, weights, cos_p, sin_p, mla_scale * LOG2E, (wuk, wuv_t), b, s)
            kr2 = jnp.concatenate(
                [kr, kr, jnp.zeros((b * s, LANES - 2 * MLA_ROPE), F32)], axis=1).astype(BF16)
            mix_p = _flash([qn, qr], [kn, kr2], vt, b, s)
            outs["mc_p"] = c.reshape(1, b, s, MLA_KV_LORA)
            outs["mr_p"] = kr.reshape(1, b, s, MLA_ROPE)

            qn, qr, c, kr, xq_s = _mla_proj(xs, g1, weights, cos_s, sin_s, mla_scale)
            outs["mc_s"] = c.reshape(1, bd, t_new, MLA_KV_LORA)
            outs["mr_s"] = kr.reshape(1, bd, t_new, MLA_ROPE)
            qnblk = _block_diag_rows(qn.reshape(bd, t_new, MLA_HEADS * MLA_NOPE), MLA_NOPE)
            qr4 = qr.reshape(bd, t_new, PAIRS, LANES)[..., :2 * MLA_ROPE]
            qr4 = qr4.reshape(bd, t_new, MLA_HEADS, MLA_ROPE)
            qr4 = jnp.pad(qr4, ((0, 0), (0, 0), (0, HEAD_PAD - MLA_HEADS), (0, LANES - MLA_ROPE)))
            qrblk = qr4.reshape(bd, t_new * HEAD_PAD, LANES)
            kr_new = jnp.pad(kr.reshape(bd, t_new, MLA_ROPE).astype(BF16),
                             ((0, 0), (0, PAGE_SIZE - t_new), (0, LANES - MLA_ROPE)))
            wuk_t = jnp.transpose(wkvb[:, :, :MLA_NOPE], (1, 2, 0)).reshape(
                MLA_HEADS * MLA_NOPE, MLA_KV_LORA).astype(BF16)
            cache_krt = jnp.transpose(cache_mla_krope, (0, 1, 3, 2))
            mix_s = _mla_decode(page_table, qnblk, qrblk, c.reshape(bd, t_new, MLA_KV_LORA),
                                kr_new, wuk_t, wuv, cache_mla_ckv[j:j + 1],
                                cache_krt[j:j + 1])
            mix_s = mix_s.reshape(bd * t_new, MLA_HEADS * MLA_V).astype(BF16)

        cross_p = _cross(xq_p.reshape(b, s, X_WIDTH), mk, mv).reshape(b * s, X_WIDTH)
        cross_s = _cross_t(xq_s.reshape(bd, t_new, X_WIDTH), mkt_s, mvt_s, layer).reshape(
            bd * t_new, X_WIDTH)
        fin = final_norm.reshape(1, d) if layer == depth - 1 else None
        xp = _ffn_half(xp, gains[layer, 2], w_up, w_down, layer, 1,
                       (mix_p, cross_p, w_out_b[layer]), fin)
        xs = _ffn_half(xs, gains[layer, 2], w_up, w_down, layer, 1,
                       (mix_s, cross_s, w_out_b[layer]), fin)

    y_prompt = xp.reshape(b, s, d)
    y_sample = xs.reshape(bd, t_new, d)
    return (y_prompt, y_sample,
            outs["fk_p"], outs["fv_p"], outs["fl_p"],
            outs["fk_s"], outs["fv_s"], outs["fl_s"],
            outs["mc_p"], outs["mr_p"], outs["mc_s"], outs["mr_s"],
            jnp.stack(outs["mk"]), jnp.stack(outs["mv"]))
```

```python
import functools

import numpy as np
import jax
import jax.numpy as jnp
from jax import lax
from jax.experimental import pallas as pl
from jax.experimental.pallas import tpu as pltpu

D_MODEL = 1024
PAGE_SIZE = 128
X_HEADS = 4
X_HEAD_DIM = 64
X_WIDTH = X_HEADS * X_HEAD_DIM
FOX_HEADS = 12
FOX_HEAD_DIM = 64
FOX_WIDTH = FOX_HEADS * FOX_HEAD_DIM
MLA_HEADS = 12
MLA_NOPE = 64
MLA_ROPE = 32
MLA_V = 64
MLA_Q_LORA = 384
MLA_KV_LORA = 256
FFN_HIDDEN = 2816
ROPE_THETA = 10000.0
RMS_EPS = 1e-6

LANES = 128
HEAD_PAD = 16
PAIRS = FOX_HEADS // 2
VMEM_LIMIT = 56 * 1024 * 1024

F32 = jnp.float32
BF16 = jnp.bfloat16
NT_DIMS = (((1,), (1,)), ((), ()))
NEG_INF = float("-inf")
LOG2E = 1.4426950408889634


def _params(*sem):
    return pltpu.CompilerParams(dimension_semantics=sem, vmem_limit_bytes=VMEM_LIMIT)


def _rms(x, g):
    ms = jnp.mean(x * x, axis=-1, keepdims=True)
    return x * lax.rsqrt(ms + RMS_EPS) * g


def _dot(a, b):
    return jnp.dot(a, b, preferred_element_type=F32)


def _dot_nt(a, b):
    return lax.dot_general(a, b, NT_DIMS, preferred_element_type=F32)


def _dot_exact(a, b):
    return jnp.dot(a, b, preferred_element_type=F32, precision=lax.Precision.HIGHEST)


def _row_tile(t, pref):
    return pref if t % pref == 0 else t


FFN_CHUNK = FFN_HIDDEN // 2


def _ffn_kernel(*refs, merge, final):
    x_ref = refs[0]
    pos = 1
    if merge:
        mix_ref, cross_ref, wo_ref = refs[1:4]
        pos = 4
    g_ref, wg_ref, wu_ref, wd_ref = refs[pos:pos + 4]
    pos += 4
    if final:
        gf_ref = refs[pos]
        pos += 1
    o_ref = refs[pos]
    x = x_ref[...]
    if merge:
        nm = mix_ref.shape[1]
        x = x + _dot(mix_ref[...], wo_ref[:nm, :]) + _dot(cross_ref[...], wo_ref[nm:, :])
    xn = _rms(x, g_ref[...]).astype(BF16)
    acc = None
    for c0 in range(0, FFN_HIDDEN, FFN_CHUNK):
        cols = slice(c0, c0 + FFN_CHUNK)
        gate = _dot(xn, wg_ref[:, cols])
        up = _dot(xn, wu_ref[:, cols])
        h = (gate / (1.0 + jnp.exp(-gate)) * up).astype(BF16)
        part = _dot(h, wd_ref[cols, :])
        acc = part if acc is None else acc + part
    y = x + 0.5 * acc
    o_ref[...] = _rms(y, gf_ref[...]) if final else y


def _ffn_half(x, g, w_up, w_down, layer, idx, merge=None, final_g=None):
    t = x.shape[0]
    tm = _row_tile(t, 512)
    row = lambda n: pl.BlockSpec((tm, n), lambda m: (m, 0))
    once = pl.Buffered(1)
    full = lambda a: pl.BlockSpec(a.shape, lambda m: (0,) * a.ndim, pipeline_mode=once)
    in_specs = [row(D_MODEL)]
    args = [x]
    if merge is not None:
        mix, cross, w_out = merge
        in_specs += [row(mix.shape[1]), row(cross.shape[1]), full(w_out)]
        args += [mix, cross, w_out]
    in_specs += [
        full(g),
        pl.BlockSpec((None, None, D_MODEL, FFN_HIDDEN), lambda m: (layer, idx, 0, 0),
                     pipeline_mode=once),
        pl.BlockSpec((None, None, D_MODEL, FFN_HIDDEN), lambda m: (layer, idx, 0, 1),
                     pipeline_mode=once),
        pl.BlockSpec((None, None, FFN_HIDDEN, D_MODEL), lambda m: (layer, idx, 0, 0),
                     pipeline_mode=once),
    ]
    args += [g, w_up, w_up, w_down]
    if final_g is not None:
        in_specs.append(full(final_g))
        args.append(final_g)
    return pl.pallas_call(
        functools.partial(_ffn_kernel, merge=merge is not None, final=final_g is not None),
        grid=(t // tm,),
        in_specs=in_specs,
        out_specs=row(D_MODEL),
        out_shape=jax.ShapeDtypeStruct((t, D_MODEL), F32),
        compiler_params=_params("parallel"),
        name="ffn_half",
    )(*args)


def _log_sigmoid(f):
    return jnp.minimum(f, 0.0) - jnp.log(1.0 + jnp.exp(-jnp.abs(f)))


def _fox_proj_kernel(x_ref, g_ref, wqkv_ref, wf_ref, bf_ref, wx_ref,
                     q_ref, k_ref, v_ref, lf_ref, xq_ref, *, qscale):
    hn = _rms(x_ref[...], g_ref[...]).astype(BF16)
    qkv = _dot(hn, wqkv_ref[...])
    q_ref[...] = (qkv[:, :FOX_WIDTH] * qscale).astype(BF16)
    k_ref[...] = qkv[:, FOX_WIDTH:2 * FOX_WIDTH]
    v_ref[...] = qkv[:, 2 * FOX_WIDTH:]
    lf_ref[...] = _log_sigmoid(_dot(hn, wf_ref[...]) + bf_ref[...])
    xq_ref[...] = (_dot(hn, wx_ref[...]) * (X_HEAD_DIM ** -0.5)).astype(BF16)


BIAS_TERMS = 3
TERM_STRIDE = 16


def _split3(x):
    hi = x.astype(BF16)
    r1 = x - hi.astype(F32)
    mid = r1.astype(BF16)
    lo = (r1 - mid.astype(F32)).astype(BF16)
    return hi, mid, lo


def _fox_proj_t_kernel(x_ref, g_ref, wqk_ref, wkvt_ref, wf_ref, bf_ref, wx_ref, tri_ref,
                       q_ref, k_ref, kt_ref, vt_ref, lf_ref, cx_ref, xq_ref, carry_ref,
                       *, qscale, tiles_per_seq):
    hn = _rms(x_ref[...], g_ref[...]).astype(BF16)
    qk = _dot(hn, wqk_ref[...])
    q_ref[...] = (qk[:, :FOX_WIDTH] * qscale).astype(BF16)
    k_ref[...] = qk[:, FOX_WIDTH:].astype(BF16)
    kvt = _dot_nt(wkvt_ref[...], hn)
    kt_ref[0] = kvt[:FOX_WIDTH, :]
    vt_ref[0] = kvt[FOX_WIDTH:, :]
    lf = _log_sigmoid(_dot(hn, wf_ref[...]) + bf_ref[...])
    lf_ref[...] = lf

    @pl.when(pl.program_id(0) % tiles_per_seq == 0)
    def _():
        carry_ref[...] = jnp.zeros_like(carry_ref)

    tri = tri_ref[...]
    span = tri.shape[0]
    carry = carry_ref[...]
    pieces = []
    for r0 in range(0, lf.shape[0], span):
        c = carry
        for term in _split3(lf[r0:r0 + span, :]):
            c = c + _dot(tri, term)
        pieces.append(c)
        carry = c[-1:, :]
    carry_ref[...] = carry
    c2 = jnp.concatenate(pieces, axis=0) * LOG2E
    row = lax.broadcasted_iota(jnp.int32, (LANES, LANES), 0)
    col = lax.broadcasted_iota(jnp.int32, (LANES, LANES), 1)
    packed = None
    for i, term in enumerate(_split3(c2)):
        sel = jnp.where((col == row + TERM_STRIDE * i) & (row < TERM_STRIDE), 1.0, 0.0)
        t = _dot(term, sel.astype(BF16))
        packed = t if packed is None else packed + t
    cx_ref[...] = packed.astype(BF16)
    xq_ref[...] = (_dot(hn, wx_ref[...]) * (X_HEAD_DIM ** -0.5)).astype(BF16)


def _fox_proj(x, g, wqkv, wf, bf, wx, qscale):
    t = x.shape[0]
    tm = _row_tile(t, 512)
    row = lambda n: pl.BlockSpec((tm, n), lambda m: (m, 0))
    full = lambda a: pl.BlockSpec(a.shape, lambda m: (0,) * a.ndim)
    return pl.pallas_call(
        functools.partial(_fox_proj_kernel, qscale=qscale),
        grid=(t // tm,),
        in_specs=[row(D_MODEL), full(g), full(wqkv), full(wf), full(bf), full(wx)],
        out_specs=[row(FOX_WIDTH), row(FOX_WIDTH), row(FOX_WIDTH), row(LANES), row(X_WIDTH)],
        out_shape=[
            jax.ShapeDtypeStruct((t, FOX_WIDTH), BF16),
            jax.ShapeDtypeStruct((t, FOX_WIDTH), F32),
            jax.ShapeDtypeStruct((t, FOX_WIDTH), F32),
            jax.ShapeDtypeStruct((t, LANES), F32),
            jax.ShapeDtypeStruct((t, X_WIDTH), BF16),
        ],
        compiler_params=_params("parallel"),
        name="fox_proj",
    )(x, g, wqkv, wf, bf, wx)


def _fox_proj_t(x, g, wqk, wkvt, wf, bf, wx, qscale, b, s):
    t = x.shape[0]
    tm = _row_tile(s, 512)
    ns = s // tm
    span = _row_tile(tm, 256)
    tri = jnp.asarray(np.tril(np.ones((span, span), np.float32)), dtype=BF16)
    row = lambda n: pl.BlockSpec((tm, n), lambda m: (m, 0))
    full = lambda a: pl.BlockSpec(a.shape, lambda m: (0,) * a.ndim)
    tspec = pl.BlockSpec((1, FOX_WIDTH, tm), lambda m: (m // ns, 0, m % ns))
    return pl.pallas_call(
        functools.partial(_fox_proj_t_kernel, qscale=qscale, tiles_per_seq=ns),
        grid=(t // tm,),
        in_specs=[row(D_MODEL), full(g), full(wqk), full(wkvt), full(wf), full(bf), full(wx),
                  full(tri)],
        out_specs=[row(FOX_WIDTH), row(FOX_WIDTH), tspec, tspec, row(LANES), row(LANES),
                   row(X_WIDTH)],
        out_shape=[
            jax.ShapeDtypeStruct((t, FOX_WIDTH), BF16),
            jax.ShapeDtypeStruct((t, FOX_WIDTH), BF16),
            jax.ShapeDtypeStruct((b, FOX_WIDTH, s), F32),
            jax.ShapeDtypeStruct((b, FOX_WIDTH, s), F32),
            jax.ShapeDtypeStruct((t, LANES), F32),
            jax.ShapeDtypeStruct((t, LANES), BF16),
            jax.ShapeDtypeStruct((t, X_WIDTH), BF16),
        ],
        scratch_shapes=[pltpu.VMEM((1, LANES), F32)],
        compiler_params=_params("arbitrary"),
        name="fox_proj_t",
    )(x, g, wqk, wkvt, wf, bf, wx, tri)


def _bias_select(moves, sign):
    r = lax.broadcasted_iota(jnp.int32, (LANES, LANES), 0)
    c = lax.broadcasted_iota(jnp.int32, (LANES, LANES), 1)
    sel = jnp.zeros((LANES, LANES), F32)
    for head, dst in moves:
        for i in range(BIAS_TERMS):
            sel = jnp.where((r == TERM_STRIDE * i + head) & (c == dst + i), sign, sel)
    return sel.astype(BF16)


def _ones_lanes(shape, lo, n):
    lane = lax.broadcasted_iota(jnp.int32, shape, 1)
    return jnp.where((lane >= lo) & (lane < lo + n), 1.0, 0.0)


def _group8(x, op):
    parts = [x[r * 8:(r + 1) * 8, :] for r in range(x.shape[0] // 8)]
    while len(parts) > 1:
        parts = [op(parts[i], parts[i + 1]) for i in range(0, len(parts), 2)]
    return parts[0]


def _flash_kernel(*refs, tile, n_q, n_k, use_c):
    q_refs = refs[:n_q]
    k_refs = refs[n_q:n_q + n_k]
    vt_ref = refs[n_q + n_k]
    pos = n_q + n_k + 1
    if use_c:
        c_ref = refs[pos]
        pos += 1
    o_ref, kb_ref, vtb_ref, s_ref, p_ref, acc_ref = refs[pos:pos + 6]
    pair = pl.program_id(1)
    seq = vt_ref.shape[2]
    n_tiles = seq // tile
    chunk = 64
    nb = BIAS_TERMS

    for n, k_ref in enumerate(k_refs):
        kb_ref[:, n * LANES:(n + 1) * LANES] = k_ref[...].astype(BF16)
    vtb_ref[...] = vt_ref[0].astype(BF16)
    if use_c:
        feat = (_dot(c_ref[0], _bias_select([(2 * pair, 0), (2 * pair + 1, nb)], -1.0))
                + _ones_lanes((1, LANES), 2 * nb, nb))
        kb_ref[:, n_k * LANES:] = feat.astype(BF16)

    kd_q = n_q * LANES
    lane = lax.broadcasted_iota(jnp.int32, (1, kd_q), 1)
    if n_q == 1:
        own = (lane < FOX_HEAD_DIM, lane >= FOX_HEAD_DIM)
    else:
        own = ((lane < MLA_NOPE) | ((lane >= LANES) & (lane < LANES + MLA_ROPE)),
               ((lane >= MLA_NOPE) & (lane < LANES))
               | ((lane >= LANES + MLA_ROPE) & (lane < LANES + 2 * MLA_ROPE)))

    def q_operands(i):
        rows = slice(i * tile, (i + 1) * tile)
        q = (jnp.concatenate([r[rows, :] for r in q_refs], axis=1) if n_q > 1
             else q_refs[0][rows, :])
        zero = jnp.zeros_like(q)
        qh = [jnp.where(own[h], q, zero) for h in range(2)]
        if use_c:
            cx = c_ref[0, rows, :]
            for h in range(2):
                feat = (_dot(cx, _bias_select([(2 * pair + h, 2 * nb)], 1.0))
                        + _ones_lanes((1, LANES), h * nb, nb))
                qh[h] = jnp.concatenate([qh[h], feat.astype(BF16)], axis=1)
        return qh

    krow = lax.broadcasted_iota(jnp.int32, (chunk, tile), 0)
    qcol = lax.broadcasted_iota(jnp.int32, (chunk, tile), 1)

    def scores(slot, qh, j):
        kblk = kb_ref[j * tile:(j + 1) * tile, :]
        for h in range(2):
            s_ref[slot, h] = _dot_nt(kblk, qh[h])

    def values(slot, par, j, alphas):
        hd = LANES // 2
        for h in range(2):
            pv = _dot(vtb_ref[h * hd:(h + 1) * hd, j * tile:(j + 1) * tile], p_ref[slot, h])
            acc_ref[par, h] = pv if alphas is None else acc_ref[par, h] * alphas[h] + pv

    def softmax(slot, stats, diagonal):
        out = []
        alphas = []
        for h in range(2):
            mx = None
            for c in range(tile // chunk):
                rows = slice(c * chunk, (c + 1) * chunk)
                blk = s_ref[slot, h, rows, :]
                if diagonal:
                    blk = jnp.where(krow + c * chunk > qcol, NEG_INF, blk)
                    s_ref[slot, h, rows, :] = blk
                cm = _group8(blk, jnp.maximum)
                mx = cm if mx is None else jnp.maximum(mx, cm)
            m_new = jnp.max(mx, axis=0, keepdims=True)
            if stats is not None:
                m_old = stats[h][0]
                m_new = jnp.maximum(m_old, m_new)
                alphas.append(jnp.exp2(m_old - m_new))
            ls = None
            for c in range(tile // chunk):
                rows = slice(c * chunk, (c + 1) * chunk)
                p = jnp.exp2(s_ref[slot, h, rows, :] - m_new)
                ps = _group8(p, jnp.add)
                ls = ps if ls is None else ls + ps
                p_ref[slot, h, rows, :] = p.astype(BF16)
            l = jnp.sum(ls, axis=0, keepdims=True)
            if stats is not None:
                l = stats[h][1] * alphas[-1] + l
            out.append((m_new, l))
        return out, (None if stats is None else alphas)

    def finish(i, stats):
        par = i % 2
        o_t = jnp.concatenate([acc_ref[par, h] / stats[h][1] for h in range(2)], axis=0)
        o_ref[i * tile:(i + 1) * tile, :] = o_t.T.astype(o_ref.dtype)

    blocks = [(i, j) for i in range(n_tiles) for j in range(i + 1)]
    qh_of = {0: q_operands(0)}
    scores(0, qh_of[0], 0)
    stats = None
    pending = None
    for n, (i, j) in enumerate(blocks):
        if n + 1 < len(blocks):
            i2, j2 = blocks[n + 1]
            if i2 not in qh_of:
                qh_of[i2] = q_operands(i2)
            scores((n + 1) % 2, qh_of[i2], j2)
        if pending is not None:
            slot_p, i_p, j_p, alphas_p, stats_p = pending
            values(slot_p, i_p % 2, j_p, alphas_p)
            if j_p == i_p:
                finish(i_p, stats_p)
        stats, alphas = softmax(n % 2, None if j == 0 else stats, j == i)
        pending = (n % 2, i, j, alphas, stats)
    slot_p, i_p, j_p, alphas_p, stats_p = pending
    values(slot_p, i_p % 2, j_p, alphas_p)
    finish(i_p, stats_p)


FLASH_TILE = 512


def _flash(qs, ks, vt, b, s, c=None):
    tile = _row_tile(s, FLASH_TILE)
    use_c = c is not None
    kd = LANES * (len(ks) + (1 if use_c else 0))
    rows = pl.BlockSpec((s, LANES), lambda i, p: (i, p))
    shared = pl.BlockSpec((s, LANES), lambda i, p: (i, 0))
    in_specs = [rows] * len(qs) + [rows if k.shape[1] > LANES else shared for k in ks]
    in_specs.append(pl.BlockSpec((1, LANES, s), lambda i, p: (i, p, 0)))
    args = list(qs) + list(ks) + [vt]
    if use_c:
        in_specs.append(pl.BlockSpec((1, s, LANES), lambda i, p: (i, 0, 0)))
        args.append(c)
    kern = functools.partial(_flash_kernel, tile=tile, n_q=len(qs), n_k=len(ks), use_c=use_c)
    return pl.pallas_call(
        kern,
        grid=(b, PAIRS),
        in_specs=in_specs,
        out_specs=rows,
        out_shape=jax.ShapeDtypeStruct((b * s, PAIRS * LANES), BF16),
        scratch_shapes=[pltpu.VMEM((s, kd), BF16), pltpu.VMEM((LANES, s), BF16),
                        pltpu.VMEM((2, 2, tile, tile), F32),
                        pltpu.VMEM((2, 2, tile, tile), BF16),
                        pltpu.VMEM((2, 2, LANES // 2, tile), F32)],
        compiler_params=_params("parallel", "parallel"),
        name="flash_mla" if len(qs) > 1 else "flash_fox",
    )(*args)


def _cross_kernel(q_ref, k_ref, v_ref, o_ref):
    q = q_ref[0].astype(BF16)
    kb = k_ref[0].astype(BF16)
    vb = v_ref[0].astype(BF16)
    lane = lax.broadcasted_iota(jnp.int32, (1, LANES), 1)
    first = lane < X_HEAD_DIM
    zero = jnp.zeros_like(q)
    outs = []
    for qh in (jnp.where(first, q, zero), jnp.where(first, zero, q)):
        s = _dot_nt(qh, kb)
        m = jnp.max(s, axis=1, keepdims=True)
        p = jnp.exp(s - m)
        l = jnp.sum(p, axis=1, keepdims=True)
        outs.append(_dot(p.astype(BF16), vb) / l)
    o_ref[0] = jnp.where(first, outs[0], outs[1]).astype(o_ref.dtype)


def _cross(xq, mk, mv):
    b, t, _ = xq.shape
    m = mk.shape[1]
    tq = _row_tile(t, 1024)
    qspec = pl.BlockSpec((1, tq, LANES), lambda i, p, j: (i, j, p))
    kspec = pl.BlockSpec((1, m, LANES), lambda i, p, j: (i, 0, p))
    return pl.pallas_call(
        _cross_kernel,
        grid=(b, X_WIDTH // LANES, t // tq),
        in_specs=[qspec, kspec, kspec],
        out_specs=qspec,
        out_shape=jax.ShapeDtypeStruct((b, t, X_WIDTH), BF16),
        compiler_params=_params("parallel", "parallel", "arbitrary"),
        name="cross_attn",
    )(xq, mk, mv)


def _cross_t_kernel(q_ref, kt_ref, vt_ref, o_ref):
    lane = lax.broadcasted_iota(jnp.int32, (1, LANES), 1)
    first = lane < X_HEAD_DIM
    for g in range(q_ref.shape[0]):
        q = q_ref[g].astype(BF16)
        halves = []
        for p in range(X_WIDTH // LANES):
            cols = slice(p * LANES, (p + 1) * LANES)
            qp = q[:, cols]
            ktb = kt_ref[g, cols, :].astype(BF16)
            vtb = vt_ref[g, cols, :].astype(BF16)
            zero = jnp.zeros_like(qp)
            outs = []
            for qh in (jnp.where(first, qp, zero), jnp.where(first, zero, qp)):
                s = _dot(qh, ktb)
                mx = jnp.max(s, axis=1, keepdims=True)
                pr = jnp.exp(s - mx)
                l = jnp.sum(pr, axis=1, keepdims=True)
                outs.append(_dot_nt(pr.astype(BF16), vtb) / l)
            halves.append(jnp.where(first, outs[0], outs[1]))
        o_ref[g] = jnp.concatenate(halves, axis=1).astype(o_ref.dtype)


CROSS_T_BATCHES = 8


def _cross_t(xq, mkt, mvt, layer):
    b, t, _ = xq.shape
    m = mkt.shape[3]
    g = CROSS_T_BATCHES if b % CROSS_T_BATCHES == 0 else 1
    qspec = pl.BlockSpec((g, t, X_WIDTH), lambda i: (i, 0, 0))
    kspec = pl.BlockSpec((None, g, X_WIDTH, m), lambda i: (layer, i, 0, 0))
    return pl.pallas_call(
        _cross_t_kernel,
        grid=(b // g,),
        in_specs=[qspec, kspec, kspec],
        out_specs=qspec,
        out_shape=jax.ShapeDtypeStruct((b, t, X_WIDTH), BF16),
        compiler_params=_params("parallel"),
        name="cross_attn_t",
    )(xq, mkt, mvt)


def _norm_proj_kernel(x_ref, g_ref, w_ref, o_ref):
    o_ref[...] = _dot(_rms(x_ref[...], g_ref[...]).astype(BF16), w_ref[...])


def _norm_proj(x, g, w):
    t = x.shape[0]
    n = w.shape[1]
    tm = _row_tile(t, 512)
    return pl.pallas_call(
        _norm_proj_kernel,
        grid=(t // tm,),
        in_specs=[pl.BlockSpec((tm, D_MODEL), lambda m: (m, 0)),
                  pl.BlockSpec((1, D_MODEL), lambda m: (0, 0)),
                  pl.BlockSpec(w.shape, lambda m: (0, 0))],
        out_specs=pl.BlockSpec((tm, n), lambda m: (m, 0)),
        out_shape=jax.ShapeDtypeStruct((t, n), F32),
        compiler_params=_params("parallel"),
        name="norm_proj",
    )(x, g, w)


def _mla_proj_kernel(*refs, qscale, with_kv):
    (x_ref, g_ref, wqa_ref, wkva_ref, wkr_ref, wkrr_ref, wx_ref,
     qg_ref, wqn_ref, wqr_ref, wqrr_ref, kvg_ref) = refs[:12]
    pos = 12
    if with_kv:
        wuk_ref, wuvt_ref = refs[pos:pos + 2]
        pos += 2
    cos_ref, sin_ref = refs[pos:pos + 2]
    qn_ref, qr_ref, c_ref, kr_ref, xq_ref = refs[pos + 2:pos + 7]
    hn = _rms(x_ref[...], g_ref[...]).astype(BF16)
    cos = cos_ref[...]
    sin = sin_ref[...]
    qa = _rms(_dot(hn, wqa_ref[...]), qg_ref[...]).astype(BF16)
    qn_ref[...] = (_dot(qa, wqn_ref[...]) * qscale).astype(BF16)
    qr = _dot(qa, wqr_ref[...])
    qrr = _dot(qa, wqrr_ref[...])
    for p in range(PAIRS):
        sl = slice(p * LANES, (p + 1) * LANES)
        qr_ref[:, sl] = ((qr[:, sl] * cos + qrr[:, sl] * sin) * qscale).astype(BF16)
    c = _rms(_dot(hn, wkva_ref[...]), kvg_ref[...])
    c_ref[...] = c
    if with_kv:
        kn_ref, vt_ref = refs[pos + 7:pos + 9]
        cb = c.astype(BF16)
        kn_ref[...] = _dot(cb, wuk_ref[...]).astype(BF16)
        vt_ref[0] = _dot_nt(wuvt_ref[...], cb).astype(BF16)
    kr = _dot(hn, wkr_ref[...]) * cos + _dot(hn, wkrr_ref[...]) * sin
    kr_ref[...] = kr[:, :MLA_ROPE]
    xq_ref[...] = (_dot(hn, wx_ref[...]) * (X_HEAD_DIM ** -0.5)).astype(BF16)


def _mla_proj(x, g, w, cos, sin, qscale, kv_w=None, b=None, s=None):
    t = x.shape[0]
    with_kv = kv_w is not None
    tm = _row_tile(s if with_kv else t, 512)
    ntab = cos.shape[0] // tm
    row = lambda n: pl.BlockSpec((tm, n), lambda m: (m, 0))
    full = lambda a: pl.BlockSpec(a.shape, lambda m: (0,) * a.ndim)
    tab = pl.BlockSpec((tm, LANES), lambda m: (m % ntab, 0))
    wide = PAIRS * LANES
    w = list(w) + (list(kv_w) if with_kv else [])
    out_specs = [row(wide), row(wide), row(MLA_KV_LORA), row(MLA_ROPE), row(X_WIDTH)]
    out_shape = [
        jax.ShapeDtypeStruct((t, wide), BF16),
        jax.ShapeDtypeStruct((t, wide), BF16),
        jax.ShapeDtypeStruct((t, MLA_KV_LORA), F32),
        jax.ShapeDtypeStruct((t, MLA_ROPE), F32),
        jax.ShapeDtypeStruct((t, X_WIDTH), BF16),
    ]
    if with_kv:
        ns = s // tm
        out_specs += [row(wide), pl.BlockSpec((1, wide, tm), lambda m: (m // ns, 0, m % ns))]
        out_shape += [jax.ShapeDtypeStruct((t, wide), BF16),
                      jax.ShapeDtypeStruct((b, wide, s), BF16)]
    return pl.pallas_call(
        functools.partial(_mla_proj_kernel, qscale=qscale, with_kv=with_kv),
        grid=(t // tm,),
        in_specs=[row(D_MODEL), full(g)] + [full(a) for a in w] + [tab, tab],
        out_specs=out_specs,
        out_shape=out_shape,
        compiler_params=_params("parallel"),
        name="mla_proj_kv" if with_kv else "mla_proj",
    )(x, g, *w, cos, sin)


FOX_PAGES_PER_STEP = 16
MLA_PAGES_PER_STEP = 64
PAGE_GROUPS = 4


def _pages_per_step(n_pages, pref):
    while n_pages % pref:
        pref //= 2
    return pref


def _online_step(state, s_parts, v_parts, v_is_transposed):
    m, l, acc = state
    n = len(s_parts)
    per = n // PAGE_GROUPS if n % PAGE_GROUPS == 0 else n
    partials = []
    for g0 in range(0, n, per):
        s = jnp.concatenate(s_parts[g0:g0 + per], axis=1) if per > 1 else s_parts[g0]
        mg = jnp.max(s, axis=1, keepdims=True)
        p = jnp.exp(s - mg)
        lg = jnp.sum(p, axis=1, keepdims=True)
        ag = None
        for r in range(per):
            pb = p[:, r * LANES:(r + 1) * LANES].astype(BF16)
            vb = v_parts[g0 + r]
            t = _dot_nt(pb, vb) if v_is_transposed else _dot(pb, vb)
            ag = t if ag is None else ag + t
        partials.append((mg, lg, ag))
    m_new = m
    for mg, _, _ in partials:
        m_new = jnp.maximum(m_new, mg)
    alpha = jnp.exp(m - m_new)
    l = l * alpha
    acc = acc * alpha
    for mg, lg, ag in partials:
        w = jnp.exp(mg - m_new)
        l = l + lg * w
        acc = acc + ag * w
    return m_new, l, acc


def _head_diag(acc, t_new, width):
    n = acc.shape[1]
    rowh = lax.broadcasted_iota(jnp.int32, (HEAD_PAD, n), 0)
    laneh = lax.broadcasted_iota(jnp.int32, (HEAD_PAD, n), 1) // width
    orow = lax.broadcasted_iota(jnp.int32, (t_new, n), 0)
    out = jnp.zeros((t_new, n), F32)
    for i in range(t_new):
        blk = acc[i * HEAD_PAD:(i + 1) * HEAD_PAD, :]
        r = jnp.sum(jnp.where(rowh == laneh, blk, 0.0), axis=0, keepdims=True)
        out = jnp.where(orow == i, r, out)
    return out


def _fox_dec_kernel(pt_ref, qb_ref, kn_ref, vn_ref, lft_ref, sufm_ref, *rest, pb, t_new):
    kt_refs = rest[:pb]
    vt_refs = rest[pb:2 * pb]
    lf_refs = rest[2 * pb:3 * pb]
    o_ref, m_ref, l_ref, acc_ref, run_ref, pad_ref, lfp_ref = rest[3 * pb:]
    step = pl.program_id(1)
    qb = qb_ref[0]
    lft = lft_ref[0]
    lane = lax.broadcasted_iota(jnp.int32, (HEAD_PAD, LANES), 1)
    cn_cols = []
    run = jnp.zeros((HEAD_PAD, 1), F32)
    for i in range(t_new):
        run = run + lft[:, i:i + 1]
        cn_cols.append(run)
    cn_col = jnp.concatenate(cn_cols, axis=0)

    @pl.when(step == 0)
    def _():
        cn_lanes = jnp.zeros((HEAD_PAD, LANES), F32)
        for i in range(t_new):
            cn_lanes = jnp.where(lane == i, cn_cols[i], cn_lanes)
        pad_ref[...] = jnp.zeros_like(pad_ref)
        pad_ref[0:t_new, :] = kn_ref[0]
        kpad = pad_ref[...].astype(BF16)
        pad_ref[0:t_new, :] = vn_ref[0]
        vpad = pad_ref[...].astype(BF16)
        s = _dot_nt(qb, kpad)
        parts = []
        for i in range(t_new):
            blk = s[i * HEAD_PAD:(i + 1) * HEAD_PAD, :] + (cn_cols[i] - cn_lanes)
            parts.append(jnp.where(lane <= i, blk, NEG_INF))
        s = jnp.concatenate(parts, axis=0)
        m = jnp.max(s, axis=1, keepdims=True)
        p = jnp.exp(s - m)
        m_ref[...] = m
        l_ref[...] = jnp.sum(p, axis=1, keepdims=True)
        acc_ref[...] = _dot(p.astype(BF16), vpad)
        run_ref[...] = jnp.zeros_like(run_ref)
        lfp_ref[...] = jnp.zeros_like(lfp_ref)

    run = run_ref[...]
    s_parts, v_parts = [], []
    for r in range(pb):
        lfp_ref[r * HEAD_PAD:r * HEAD_PAD + FOX_HEADS, :] = lf_refs[r][0]
    lf_all = lfp_ref[...]
    suf_all = _dot_exact(lf_all, sufm_ref[...])
    tot_all = jnp.sum(lf_all, axis=1, keepdims=True)
    for r in range(pb):
        heads = slice(r * HEAD_PAD, (r + 1) * HEAD_PAD)
        suf = suf_all[heads, :] + run
        run = run + tot_all[heads, :]
        ktb = kt_refs[r][0, 0].reshape(FOX_WIDTH, PAGE_SIZE).astype(BF16)
        s = _dot(qb, ktb)
        bias = jnp.concatenate([suf] * t_new, axis=0) + cn_col
        s_parts.append(s + bias)
        v_parts.append(vt_refs[r][0, 0].reshape(FOX_WIDTH, PAGE_SIZE).astype(BF16))
    run_ref[...] = run
    m, l, acc = _online_step((m_ref[...], l_ref[...], acc_ref[...]), s_parts, v_parts, True)
    m_ref[...] = m
    l_ref[...] = l
    acc_ref[...] = acc

    @pl.when(step == pl.num_programs(1) - 1)
    def _():
        o_ref[0] = _head_diag(acc / l, t_new, FOX_HEAD_DIM)


def _fox_decode(page_table, qblk, k_new, v_new, lft_new, cache_kt, cache_vt, cache_lf):
    bd, n_pages = page_table.shape
    t_new = k_new.shape[1]
    rows = t_new * HEAD_PAD
    pb = _pages_per_step(n_pages, FOX_PAGES_PER_STEP)
    sufm = jnp.asarray(np.tril(np.ones((PAGE_SIZE, PAGE_SIZE), np.float32), -1))

    def page(r):
        return lambda b, s, pt: pt[b, n_pages - 1 - (s * pb + r)]

    kv_specs = [pl.BlockSpec((1, 1, FOX_HEADS, FOX_HEAD_DIM, PAGE_SIZE),
                             (lambda b, s, pt, f=page(r): (0, f(b, s, pt), 0, 0, 0)))
                for r in range(pb)]
    lf_specs = [pl.BlockSpec((1, FOX_HEADS, PAGE_SIZE),
                             (lambda b, s, pt, f=page(r): (f(b, s, pt), 0, 0)))
                for r in range(pb)]
    per_b = lambda shape: pl.BlockSpec((1,) + shape, lambda b, s, pt: (b, 0, 0))
    grid_spec = pltpu.PrefetchScalarGridSpec(
        num_scalar_prefetch=1,
        grid=(bd, n_pages // pb),
        in_specs=[per_b((rows, FOX_WIDTH)), per_b((t_new, FOX_WIDTH)), per_b((t_new, FOX_WIDTH)),
                  per_b((HEAD_PAD, LANES)),
                  pl.BlockSpec((PAGE_SIZE, PAGE_SIZE), lambda b, s, pt: (0, 0))]
        + kv_specs + kv_specs + lf_specs,
        out_specs=per_b((t_new, FOX_WIDTH)),
        scratch_shapes=[pltpu.VMEM((rows, 1), F32), pltpu.VMEM((rows, 1), F32),
                        pltpu.VMEM((rows, FOX_WIDTH), F32), pltpu.VMEM((HEAD_PAD, 1), F32),
                        pltpu.VMEM((PAGE_SIZE, FOX_WIDTH), F32),
                        pltpu.VMEM((pb * HEAD_PAD, PAGE_SIZE), F32)],
    )
    return pl.pallas_call(
        functools.partial(_fox_dec_kernel, pb=pb, t_new=t_new),
        grid_spec=grid_spec,
        out_shape=jax.ShapeDtypeStruct((bd, t_new, FOX_WIDTH), F32),
        compiler_params=_params("parallel", "arbitrary"),
        name="fox_decode",
    )(page_table, qblk, k_new, v_new, lft_new, sufm,
      *([cache_kt] * pb), *([cache_vt] * pb), *([cache_lf] * pb))


def _mla_dec_kernel(pt_ref, qn_ref, qr_ref, cn_ref, krn_ref, wuk_ref, wuv_ref, *rest, pb, t_new):
    c_refs = rest[:pb]
    kr_refs = rest[pb:2 * pb]
    o_ref, m_ref, l_ref, acc_ref, ql_ref, cpad_ref, krt_ref = rest[2 * pb:]
    step = pl.program_id(1)
    lane = lax.broadcasted_iota(jnp.int32, (HEAD_PAD, LANES), 1)
    qr = qr_ref[0]

    @pl.when(step == 0)
    def _():
        ql = _dot(qn_ref[0], wuk_ref[...]).astype(BF16)
        ql_ref[...] = ql
        cpad_ref[...] = jnp.zeros_like(cpad_ref)
        cpad_ref[0:t_new, :] = cn_ref[0]
        cpad = cpad_ref[...].astype(BF16)
        s = _dot_nt(ql, cpad) + _dot_nt(qr, krn_ref[0])
        parts = [jnp.where(lane <= i, s[i * HEAD_PAD:(i + 1) * HEAD_PAD, :], NEG_INF)
                 for i in range(t_new)]
        s = jnp.concatenate(parts, axis=0)
        m = jnp.max(s, axis=1, keepdims=True)
        p = jnp.exp(s - m)
        m_ref[...] = m
        l_ref[...] = jnp.sum(p, axis=1, keepdims=True)
        acc_ref[...] = _dot(p.astype(BF16), cpad)
        krt_ref[...] = jnp.zeros_like(krt_ref)

    ql = ql_ref[...]
    s_parts, v_parts = [], []
    for r in range(pb):
        cb = c_refs[r][0, 0].astype(BF16)
        krt_ref[0:MLA_ROPE, :] = kr_refs[r][0, 0].astype(BF16)
        s_parts.append(_dot_nt(ql, cb) + _dot(qr, krt_ref[...]))
        v_parts.append(cb)
    m, l, acc = _online_step((m_ref[...], l_ref[...], acc_ref[...]), s_parts, v_parts, False)
    m_ref[...] = m
    l_ref[...] = l
    acc_ref[...] = acc

    @pl.when(step == pl.num_programs(1) - 1)
    def _():
        full = _dot((acc / l).astype(BF16), wuv_ref[...])
        o_ref[0] = _head_diag(full, t_new, MLA_V)


def _mla_decode(page_table, qnblk, qrblk, c_new, kr_new, wuk_t, wuv, cache_c, cache_krt):
    bd, n_pages = page_table.shape
    t_new = c_new.shape[1]
    rows = t_new * HEAD_PAD
    pb = _pages_per_step(n_pages, MLA_PAGES_PER_STEP)

    def page(r):
        return lambda b, s, pt: pt[b, s * pb + r]

    c_specs = [pl.BlockSpec((1, 1, PAGE_SIZE, MLA_KV_LORA),
                            (lambda b, s, pt, f=page(r): (0, f(b, s, pt), 0, 0)))
               for r in range(pb)]
    kr_specs = [pl.BlockSpec((1, 1, MLA_ROPE, PAGE_SIZE),
                             (lambda b, s, pt, f=page(r): (0, f(b, s, pt), 0, 0)))
                for r in range(pb)]
    per_b = lambda shape: pl.BlockSpec((1,) + shape, lambda b, s, pt: (b, 0, 0))
    full = lambda a: pl.BlockSpec(a.shape, lambda b, s, pt: (0,) * a.ndim)
    wide = MLA_HEADS * MLA_V
    grid_spec = pltpu.PrefetchScalarGridSpec(
        num_scalar_prefetch=1,
        grid=(bd, n_pages // pb),
        in_specs=[per_b((rows, MLA_HEADS * MLA_NOPE)), per_b((rows, LANES)),
                  per_b((t_new, MLA_KV_LORA)), per_b((PAGE_SIZE, LANES)),
                  full(wuk_t), full(wuv)] + c_specs + kr_specs,
        out_specs=per_b((t_new, wide)),
        scratch_shapes=[pltpu.VMEM((rows, 1), F32), pltpu.VMEM((rows, 1), F32),
                        pltpu.VMEM((rows, MLA_KV_LORA), F32),
                        pltpu.VMEM((rows, MLA_KV_LORA), BF16),
                        pltpu.VMEM((PAGE_SIZE, MLA_KV_LORA), F32),
                        pltpu.VMEM((LANES, PAGE_SIZE), BF16)],
    )
    return pl.pallas_call(
        functools.partial(_mla_dec_kernel, pb=pb, t_new=t_new),
        grid_spec=grid_spec,
        out_shape=jax.ShapeDtypeStruct((bd, t_new, wide), F32),
        compiler_params=_params("parallel", "arbitrary"),
        name="mla_decode",
    )(page_table, qnblk, qrblk, c_new, kr_new, wuk_t, wuv,
      *([cache_c] * pb), *([cache_krt] * pb))


def _rope_tables(pos):
    half = MLA_ROPE // 2
    inv = ROPE_THETA ** (-jnp.arange(half, dtype=F32) / half)
    ang = pos.astype(F32)[:, None] * inv
    reps = (LANES // 2) // half
    pad = jnp.zeros((pos.shape[0], LANES // 2), F32)
    cos = jnp.concatenate([jnp.tile(jnp.cos(ang), (1, reps)), pad], axis=1)
    sin = jnp.concatenate([jnp.tile(jnp.sin(ang), (1, reps)), pad], axis=1)
    return cos, sin


def _rot_cols(w):
    half = MLA_ROPE // 2
    return jnp.concatenate([-w[..., half:], w[..., :half]], axis=-1)


def _pair_rope_cols(w):
    r = w.shape[0]
    w = w.reshape(r, PAIRS, 2 * MLA_ROPE)
    return jnp.pad(w, ((0, 0), (0, 0), (0, LANES - 2 * MLA_ROPE))).reshape(r, PAIRS * LANES)


def _block_diag_rows(q, width):
    bd, t, n = q.shape
    rowh = np.arange(t * HEAD_PAD) % HEAD_PAD
    mask = jnp.asarray(rowh[:, None] == (np.arange(n) // width)[None, :])
    rep = jnp.repeat(q, HEAD_PAD, axis=1)
    return jnp.where(mask[None], rep, jnp.zeros_like(rep))


def kernel(x_prompt, x_sample, cache_fox_k, cache_fox_v, cache_fox_logf, cache_mla_ckv,
           cache_mla_krope, cache_mem_k, cache_mem_v, page_table, mem_prompt, norm_gains,
           ffn_w_up, ffn_w_down, fox_w_in, fox_b_f, mla_w_in, mla_q_norm, mla_w_q_b,
           mla_kv_norm, mla_w_kv_b, mem_norm, w_mem_kv, w_out, final_norm):
    b, s, d = x_prompt.shape
    bd, t_new, _ = x_sample.shape
    n_pages = page_table.shape[1]
    mem_len = mem_prompt.shape[1]
    depth = norm_gains.shape[0]
    xp = x_prompt.reshape(b * s, d)
    xs = x_sample.reshape(bd * t_new, d)

    w_up = ffn_w_up.astype(BF16)
    w_down = ffn_w_down.astype(BF16)
    w_out_b = w_out.astype(BF16)
    w_mem_b = w_mem_kv.astype(BF16)
    gains = norm_gains.reshape(depth, 3, 1, d)
    mem2d = mem_prompt.reshape(b * mem_len, d)
    mkt_s = jnp.transpose(cache_mem_k, (0, 1, 3, 4, 2)).reshape(depth, bd, X_WIDTH, mem_len)
    mvt_s = jnp.transpose(cache_mem_v, (0, 1, 3, 4, 2)).reshape(depth, bd, X_WIDTH, mem_len)

    outs = {}
    for layer in range(depth):
        j = layer // 2
        xp = _ffn_half(xp, gains[layer, 0], w_up, w_down, layer, 0)
        xs = _ffn_half(xs, gains[layer, 0], w_up, w_down, layer, 0)

        mkv = _norm_proj(mem2d, mem_norm[layer].reshape(1, d), w_mem_b[layer])
        mk = mkv[:, :X_WIDTH].reshape(b, mem_len, X_WIDTH)
        mv = mkv[:, X_WIDTH:].reshape(b, mem_len, X_WIDTH)
        outs.setdefault("mk", []).append(mk.reshape(b, mem_len, X_HEADS, X_HEAD_DIM))
        outs.setdefault("mv", []).append(mv.reshape(b, mem_len, X_HEADS, X_HEAD_DIM))

        if layer % 2 == 0:
            w_in = fox_w_in[j]
            wqkv = w_in[:, :3 * FOX_WIDTH].astype(BF16)
            wf = jnp.pad(w_in[:, 3 * FOX_WIDTH:3 * FOX_WIDTH + FOX_HEADS],
                         ((0, 0), (0, LANES - FOX_HEADS))).astype(BF16)
            bf = jnp.pad(fox_b_f[j], (0, LANES - FOX_HEADS)).reshape(1, LANES)
            wx = w_in[:, 3 * FOX_WIDTH + FOX_HEADS:].astype(BF16)
            g1 = gains[layer, 1]

            wqk = w_in[:, :2 * FOX_WIDTH].astype(BF16)
            wkvt = jnp.transpose(w_in[:, FOX_WIDTH:3 * FOX_WIDTH]).astype(BF16)
            q, kb, kt, vt, lf, cx, xq_p = _fox_proj_t(
                xp, g1, wqk, wkvt, wf, bf, wx, FOX_HEAD_DIM ** -0.5 * LOG2E, b, s)
            mix_p = _flash([q], [kb], vt, b, s, cx.reshape(b, s, LANES))
            heads_t = (0, 1, 4, 2, 3)
            outs["fk_p"] = jnp.transpose(kt.reshape(1, b, FOX_HEADS, FOX_HEAD_DIM, s), heads_t)
            outs["fv_p"] = jnp.transpose(vt.reshape(1, b, FOX_HEADS, FOX_HEAD_DIM, s), heads_t)
            outs["fl_p"] = lf[:, :FOX_HEADS].reshape(1, b, s, FOX_HEADS)

            q, k, v, lf, xq_s = _fox_proj(xs, g1, wqkv, wf, bf, wx, FOX_HEAD_DIM ** -0.5)
            outs["fk_s"] = k.reshape(1, bd, t_new, FOX_HEADS, FOX_HEAD_DIM)
            outs["fv_s"] = v.reshape(1, bd, t_new, FOX_HEADS, FOX_HEAD_DIM)
            outs["fl_s"] = lf[:, :FOX_HEADS].reshape(1, bd, t_new, FOX_HEADS)
            qblk = _block_diag_rows(q.reshape(bd, t_new, FOX_WIDTH), FOX_HEAD_DIM)
            lft = jnp.transpose(lf.reshape(bd, t_new, LANES)[:, :, :HEAD_PAD], (0, 2, 1))
            lft = jnp.pad(lft, ((0, 0), (0, 0), (0, LANES - t_new)))
            cache_kt = jnp.transpose(cache_fox_k, (0, 1, 3, 4, 2))
            cache_vt = jnp.transpose(cache_fox_v, (0, 1, 3, 4, 2))
            cache_lf = jnp.transpose(cache_fox_logf[j], (0, 2, 1))
            mix_s = _fox_decode(page_table, qblk, k.reshape(bd, t_new, FOX_WIDTH),
                                v.reshape(bd, t_new, FOX_WIDTH), lft,
                                cache_kt[j:j + 1], cache_vt[j:j + 1], cache_lf)
            mix_s = mix_s.reshape(bd * t_new, FOX_WIDTH).astype(BF16)
        else:
            w_in = mla_w_in[j]
            o1 = MLA_Q_LORA
            o2 = o1 + MLA_KV_LORA
            o3 = o2 + MLA_ROPE
            wkr = w_in[:, o2:o3]
            lane_pad = ((0, 0), (0, LANES - MLA_ROPE))
            wqb = mla_w_q_b[j]
            wkvb = mla_w_kv_b[j]
            weights = [
                w_in[:, :o1].astype(BF16),
                w_in[:, o1:o2].astype(BF16),
                jnp.pad(wkr, lane_pad).astype(BF16),
                jnp.pad(_rot_cols(wkr), lane_pad).astype(BF16),
                w_in[:, o3:].astype(BF16),
                mla_q_norm[j].reshape(1, MLA_Q_LORA),
                wqb[:, :, :MLA_NOPE].reshape(MLA_Q_LORA, MLA_HEADS * MLA_NOPE).astype(BF16),
                _pair_rope_cols(wqb[:, :, MLA_NOPE:]).astype(BF16),
                _pair_rope_cols(_rot_cols(wqb[:, :, MLA_NOPE:])).astype(BF16),
                mla_kv_norm[j].reshape(1, MLA_KV_LORA),
            ]
            wuk = wkvb[:, :, :MLA_NOPE].reshape(MLA_KV_LORA, MLA_HEADS * MLA_NOPE).astype(BF16)
            wuv = wkvb[:, :, MLA_NOPE:].reshape(MLA_KV_LORA, MLA_HEADS * MLA_V).astype(BF16)
            wuv_t = jnp.transpose(wkvb[:, :, MLA_NOPE:], (1, 2, 0)).reshape(
                MLA_HEADS * MLA_V, MLA_KV_LORA).astype(BF16)
            mla_scale = (MLA_NOPE + MLA_ROPE) ** -0.5
            g1 = gains[layer, 1]
            cos_p, sin_p = _rope_tables(jnp.arange(s, dtype=jnp.int32))
            pos_s = n_pages * PAGE_SIZE + jnp.arange(t_new, dtype=jnp.int32)
            cos_s, sin_s = _rope_tables(jnp.tile(pos_s, bd))

            qn, qr, c, kr, xq_p, kn, vt = _mla_proj(
                xp, g1, weights, cos_p, sin_p, mla_scale * LOG2E, (wuk, wuv_t), b, s)
            kr2 = jnp.concatenate(
                [kr, kr, jnp.zeros((b * s, LANES - 2 * MLA_ROPE), F32)], axis=1).astype(BF16)
            mix_p = _flash([qn, qr], [kn, kr2], vt, b, s)
            outs["mc_p"] = c.reshape(1, b, s, MLA_KV_LORA)
            outs["mr_p"] = kr.reshape(1, b, s, MLA_ROPE)

            qn, qr, c, kr, xq_s = _mla_proj(xs, g1, weights, cos_s, sin_s, mla_scale)
            outs["mc_s"] = c.reshape(1, bd, t_new, MLA_KV_LORA)
            outs["mr_s"] = kr.reshape(1, bd, t_new, MLA_ROPE)
            qnblk = _block_diag_rows(qn.reshape(bd, t_new, MLA_HEADS * MLA_NOPE), MLA_NOPE)
            qr4 = qr.reshape(bd, t_new, PAIRS, LANES)[..., :2 * MLA_ROPE]
            qr4 = qr4.reshape(bd, t_new, MLA_HEADS, MLA_ROPE)
            qr4 = jnp.pad(qr4, ((0, 0), (0, 0), (0, HEAD_PAD - MLA_HEADS), (0, LANES - MLA_ROPE)))
            qrblk = qr4.reshape(bd, t_new * HEAD_PAD, LANES)
            kr_new = jnp.pad(kr.reshape(bd, t_new, MLA_ROPE).astype(BF16),
                             ((0, 0), (0, PAGE_SIZE - t_new), (0, LANES - MLA_ROPE)))
            wuk_t = jnp.transpose(wkvb[:, :, :MLA_NOPE], (1, 2, 0)).reshape(
                MLA_HEADS * MLA_NOPE, MLA_KV_LORA).astype(BF16)
            cache_krt = jnp.transpose(cache_mla_krope, (0, 1, 3, 2))
            mix_s = _mla_decode(page_table, qnblk, qrblk, c.reshape(bd, t_new, MLA_KV_LORA),
                                kr_new, wuk_t, wuv, cache_mla_ckv[j:j + 1],
                                cache_krt[j:j + 1])
            mix_s = mix_s.reshape(bd * t_new, MLA_HEADS * MLA_V).astype(BF16)

        cross_p = _cross(xq_p.reshape(b, s, X_WIDTH), mk, mv).reshape(b * s, X_WIDTH)
        cross_s = _cross_t(xq_s.reshape(bd, t_new, X_WIDTH), mkt_s, mvt_s, layer).reshape(
            bd * t_new, X_WIDTH)
        fin = final_norm.reshape(1, d) if layer == depth - 1 else None
        xp = _ffn_half(xp, gains[layer, 2], w_up, w_down, layer, 1,
                       (mix_p, cross_p, w_out_b[layer]), fin)
        xs = _ffn_half(xs, gains[layer, 2], w_up, w_down, layer, 1,
                       (mix_s, cross_s, w_out_b[layer]), fin)

    y_prompt = xp.reshape(b, s, d)
    y_sample = xs.reshape(bd, t_new, d)
    return (y_prompt, y_sample,
            outs["fk_p"], outs["fv_p"], outs["fl_p"],
            outs["fk_s"], outs["fv_s"], outs["fl_s"],
            outs["mc_p"], outs["mr_p"], outs["mc_s"], outs["mr_s"],
            jnp.stack(outs["mk"]), jnp.stack(outs["mv"]))
```

```python
import functools

import numpy as np
import jax
import jax.numpy as jnp
from jax import lax
from jax.experimental import pallas as pl
from jax.experimental.pallas import tpu as pltpu

D_MODEL = 1024
PAGE_SIZE = 128
X_HEADS = 4
X_HEAD_DIM = 64
X_WIDTH = X_HEADS * X_HEAD_DIM
FOX_HEADS = 12
FOX_HEAD_DIM = 64
FOX_WIDTH = FOX_HEADS * FOX_HEAD_DIM
MLA_HEADS = 12
MLA_NOPE = 64
MLA_ROPE = 32
MLA_V = 64
MLA_Q_LORA = 384
MLA_KV_LORA = 256
FFN_HIDDEN = 2816
ROPE_THETA = 10000.0
RMS_EPS = 1e-6

LANES = 128
HEAD_PAD = 16
PAIRS = FOX_HEADS // 2
VMEM_LIMIT = 56 * 1024 * 1024

F32 = jnp.float32
BF16 = jnp.bfloat16
NT_DIMS = (((1,), (1,)), ((), ()))
NEG_INF = float("-inf")
LOG2E = 1.4426950408889634


def _params(*sem):
    return pltpu.CompilerParams(dimension_semantics=sem, vmem_limit_bytes=VMEM_LIMIT)


def _rms(x, g):
    ms = jnp.mean(x * x, axis=-1, keepdims=True)
    return x * lax.rsqrt(ms + RMS_EPS) * g


def _dot(a, b):
    return jnp.dot(a, b, preferred_element_type=F32)


def _dot_nt(a, b):
    return lax.dot_general(a, b, NT_DIMS, preferred_element_type=F32)


def _dot_exact(a, b):
    return jnp.dot(a, b, preferred_element_type=F32, precision=lax.Precision.HIGHEST)


def _row_tile(t, pref):
    return pref if t % pref == 0 else t


FFN_CHUNK = FFN_HIDDEN // 2


def _ffn_kernel(*refs, merge, final):
    x_ref = refs[0]
    pos = 1
    if merge:
        mix_ref, cross_ref, wo_ref = refs[1:4]
        pos = 4
    g_ref, wg_ref, wu_ref, wd_ref = refs[pos:pos + 4]
    pos += 4
    if final:
        gf_ref = refs[pos]
        pos += 1
    o_ref = refs[pos]
    x = x_ref[...]
    if merge:
        nm = mix_ref.shape[1]
        x = x + _dot(mix_ref[...], wo_ref[:nm, :]) + _dot(cross_ref[...], wo_ref[nm:, :])
    xn = _rms(x, g_ref[...]).astype(BF16)
    acc = None
    for c0 in range(0, FFN_HIDDEN, FFN_CHUNK):
        cols = slice(c0, c0 + FFN_CHUNK)
        gate = _dot(xn, wg_ref[:, cols])
        up = _dot(xn, wu_ref[:, cols])
        h = (gate / (1.0 + jnp.exp(-gate)) * up).astype(BF16)
        part = _dot(h, wd_ref[cols, :])
        acc = part if acc is None else acc + part
    y = x + 0.5 * acc
    o_ref[...] = _rms(y, gf_ref[...]) if final else y


def _ffn_half(x, g, w_up, w_down, layer, idx, merge=None, final_g=None):
    t = x.shape[0]
    tm = _row_tile(t, 512)
    row = lambda n: pl.BlockSpec((tm, n), lambda m: (m, 0))
    once = pl.Buffered(1)
    full = lambda a: pl.BlockSpec(a.shape, lambda m: (0,) * a.ndim, pipeline_mode=once)
    in_specs = [row(D_MODEL)]
    args = [x]
    if merge is not None:
        mix, cross, w_out = merge
        in_specs += [row(mix.shape[1]), row(cross.shape[1]), full(w_out)]
        args += [mix, cross, w_out]
    in_specs += [
        full(g),
        pl.BlockSpec((None, None, D_MODEL, FFN_HIDDEN), lambda m: (layer, idx, 0, 0),
                     pipeline_mode=once),
        pl.BlockSpec((None, None, D_MODEL, FFN_HIDDEN), lambda m: (layer, idx, 0, 1),
                     pipeline_mode=once),
        pl.BlockSpec((None, None, FFN_HIDDEN, D_MODEL), lambda m: (layer, idx, 0, 0),
                     pipeline_mode=once),
    ]
    args += [g, w_up, w_up, w_down]
    if final_g is not None:
        in_specs.append(full(final_g))
        args.append(final_g)
    return pl.pallas_call(
        functools.partial(_ffn_kernel, merge=merge is not None, final=final_g is not None),
        grid=(t // tm,),
        in_specs=in_specs,
        out_specs=row(D_MODEL),
        out_shape=jax.ShapeDtypeStruct((t, D_MODEL), F32),
        compiler_params=_params("parallel"),
        name="ffn_half",
    )(*args)


def _log_sigmoid(f):
    return jnp.minimum(f, 0.0) - jnp.log(1.0 + jnp.exp(-jnp.abs(f)))


def _fox_proj_kernel(x_ref, g_ref, wqkv_ref, wf_ref, bf_ref, wx_ref,
                     q_ref, k_ref, v_ref, lf_ref, xq_ref, *, qscale):
    hn = _rms(x_ref[...], g_ref[...]).astype(BF16)
    qkv = _dot(hn, wqkv_ref[...])
    q_ref[...] = (qkv[:, :FOX_WIDTH] * qscale).astype(BF16)
    k_ref[...] = qkv[:, FOX_WIDTH:2 * FOX_WIDTH]
    v_ref[...] = qkv[:, 2 * FOX_WIDTH:]
    lf_ref[...] = _log_sigmoid(_dot(hn, wf_ref[...]) + bf_ref[...])
    xq_ref[...] = (_dot(hn, wx_ref[...]) * (X_HEAD_DIM ** -0.5)).astype(BF16)


BIAS_TERMS = 3
TERM_STRIDE = 16


def _split3(x):
    hi = x.astype(BF16)
    r1 = x - hi.astype(F32)
    mid = r1.astype(BF16)
    lo = (r1 - mid.astype(F32)).astype(BF16)
    return hi, mid, lo


def _fox_proj_t_kernel(x_ref, g_ref, wqk_ref, wkvt_ref, wf_ref, bf_ref, wx_ref, tri_ref,
                       q_ref, k_ref, kt_ref, vt_ref, lf_ref, cx_ref, xq_ref, carry_ref,
                       *, qscale, tiles_per_seq):
    hn = _rms(x_ref[...], g_ref[...]).astype(BF16)
    qk = _dot(hn, wqk_ref[...])
    q_ref[...] = (qk[:, :FOX_WIDTH] * qscale).astype(BF16)
    k_ref[...] = qk[:, FOX_WIDTH:].astype(BF16)
    kvt = _dot_nt(wkvt_ref[...], hn)
    kt_ref[0] = kvt[:FOX_WIDTH, :]
    vt_ref[0] = kvt[FOX_WIDTH:, :]
    lf = _log_sigmoid(_dot(hn, wf_ref[...]) + bf_ref[...])
    lf_ref[...] = lf

    @pl.when(pl.program_id(0) % tiles_per_seq == 0)
    def _():
        carry_ref[...] = jnp.zeros_like(carry_ref)

    tri = tri_ref[...]
    span = tri.shape[0]
    carry = carry_ref[...]
    pieces = []
    for r0 in range(0, lf.shape[0], span):
        c = carry
        for term in _split3(lf[r0:r0 + span, :]):
            c = c + _dot(tri, term)
        pieces.append(c)
        carry = c[-1:, :]
    carry_ref[...] = carry
    c2 = jnp.concatenate(pieces, axis=0) * LOG2E
    row = lax.broadcasted_iota(jnp.int32, (LANES, LANES), 0)
    col = lax.broadcasted_iota(jnp.int32, (LANES, LANES), 1)
    packed = None
    for i, term in enumerate(_split3(c2)):
        sel = jnp.where((col == row + TERM_STRIDE * i) & (row < TERM_STRIDE), 1.0, 0.0)
        t = _dot(term, sel.astype(BF16))
        packed = t if packed is None else packed + t
    cx_ref[...] = packed.astype(BF16)
    xq_ref[...] = (_dot(hn, wx_ref[...]) * (X_HEAD_DIM ** -0.5)).astype(BF16)


def _fox_proj(x, g, wqkv, wf, bf, wx, qscale):
    t = x.shape[0]
    tm = _row_tile(t, 512)
    row = lambda n: pl.BlockSpec((tm, n), lambda m: (m, 0))
    full = lambda a: pl.BlockSpec(a.shape, lambda m: (0,) * a.ndim)
    return pl.pallas_call(
        functools.partial(_fox_proj_kernel, qscale=qscale),
        grid=(t // tm,),
        in_specs=[row(D_MODEL), full(g), full(wqkv), full(wf), full(bf), full(wx)],
        out_specs=[row(FOX_WIDTH), row(FOX_WIDTH), row(FOX_WIDTH), row(LANES), row(X_WIDTH)],
        out_shape=[
            jax.ShapeDtypeStruct((t, FOX_WIDTH), BF16),
            jax.ShapeDtypeStruct((t, FOX_WIDTH), F32),
            jax.ShapeDtypeStruct((t, FOX_WIDTH), F32),
            jax.ShapeDtypeStruct((t, LANES), F32),
            jax.ShapeDtypeStruct((t, X_WIDTH), BF16),
        ],
        compiler_params=_params("parallel"),
        name="fox_proj",
    )(x, g, wqkv, wf, bf, wx)


def _fox_proj_t(x, g, wqk, wkvt, wf, bf, wx, qscale, b, s):
    t = x.shape[0]
    tm = _row_tile(s, 512)
    ns = s // tm
    span = _row_tile(tm, 256)
    tri = jnp.asarray(np.tril(np.ones((span, span), np.float32)), dtype=BF16)
    row = lambda n: pl.BlockSpec((tm, n), lambda m: (m, 0))
    full = lambda a: pl.BlockSpec(a.shape, lambda m: (0,) * a.ndim)
    tspec = pl.BlockSpec((1, FOX_WIDTH, tm), lambda m: (m // ns, 0, m % ns))
    return pl.pallas_call(
        functools.partial(_fox_proj_t_kernel, qscale=qscale, tiles_per_seq=ns),
        grid=(t // tm,),
        in_specs=[row(D_MODEL), full(g), full(wqk), full(wkvt), full(wf), full(bf), full(wx),
                  full(tri)],
        out_specs=[row(FOX_WIDTH), row(FOX_WIDTH), tspec, tspec, row(LANES), row(LANES),
                   row(X_WIDTH)],
        out_shape=[
            jax.ShapeDtypeStruct((t, FOX_WIDTH), BF16),
            jax.ShapeDtypeStruct((t, FOX_WIDTH), BF16),
            jax.ShapeDtypeStruct((b, FOX_WIDTH, s), F32),
            jax.ShapeDtypeStruct((b, FOX_WIDTH, s), F32),
            jax.ShapeDtypeStruct((t, LANES), F32),
            jax.ShapeDtypeStruct((t, LANES), BF16),
            jax.ShapeDtypeStruct((t, X_WIDTH), BF16),
        ],
        scratch_shapes=[pltpu.VMEM((1, LANES), F32)],
        compiler_params=_params("arbitrary"),
        name="fox_proj_t",
    )(x, g, wqk, wkvt, wf, bf, wx, tri)


def _bias_select(moves, sign):
    r = lax.broadcasted_iota(jnp.int32, (LANES, LANES), 0)
    c = lax.broadcasted_iota(jnp.int32, (LANES, LANES), 1)
    sel = jnp.zeros((LANES, LANES), F32)
    for head, dst in moves:
        for i in range(BIAS_TERMS):
            sel = jnp.where((r == TERM_STRIDE * i + head) & (c == dst + i), sign, sel)
    return sel.astype(BF16)


def _ones_lanes(shape, lo, n):
    lane = lax.broadcasted_iota(jnp.int32, shape, 1)
    return jnp.where((lane >= lo) & (lane < lo + n), 1.0, 0.0)


def _group8(x, op):
    parts = [x[r * 8:(r + 1) * 8, :] for r in range(x.shape[0] // 8)]
    while len(parts) > 1:
        parts = [op(parts[i], parts[i + 1]) for i in range(0, len(parts), 2)]
    return parts[0]


def _flash_kernel(*refs, tile, n_q, n_k, use_c):
    q_refs = refs[:n_q]
    k_refs = refs[n_q:n_q + n_k]
    vt_ref = refs[n_q + n_k]
    pos = n_q + n_k + 1
    if use_c:
        c_ref = refs[pos]
        pos += 1
    o_ref, kb_ref, vtb_ref, s_ref, p_ref, acc_ref = refs[pos:pos + 6]
    pair = pl.program_id(1)
    seq = vt_ref.shape[2]
    n_tiles = seq // tile
    chunk = 64
    nb = BIAS_TERMS

    for n, k_ref in enumerate(k_refs):
        kb_ref[:, n * LANES:(n + 1) * LANES] = k_ref[...].astype(BF16)
    vtb_ref[...] = vt_ref[0].astype(BF16)
    if use_c:
        feat = (_dot(c_ref[0], _bias_select([(2 * pair, 0), (2 * pair + 1, nb)], -1.0))
                + _ones_lanes((1, LANES), 2 * nb, nb))
        kb_ref[:, n_k * LANES:] = feat.astype(BF16)

    kd_q = n_q * LANES
    lane = lax.broadcasted_iota(jnp.int32, (1, kd_q), 1)
    if n_q == 1:
        own = (lane < FOX_HEAD_DIM, lane >= FOX_HEAD_DIM)
    else:
        own = ((lane < MLA_NOPE) | ((lane >= LANES) & (lane < LANES + MLA_ROPE)),
               ((lane >= MLA_NOPE) & (lane < LANES))
               | ((lane >= LANES + MLA_ROPE) & (lane < LANES + 2 * MLA_ROPE)))

    def q_operands(i):
        rows = slice(i * tile, (i + 1) * tile)
        q = (jnp.concatenate([r[rows, :] for r in q_refs], axis=1) if n_q > 1
             else q_refs[0][rows, :])
        zero = jnp.zeros_like(q)
        qh = [jnp.where(own[h], q, zero) for h in range(2)]
        if use_c:
            cx = c_ref[0, rows, :]
            for h in range(2):
                feat = (_dot(cx, _bias_select([(2 * pair + h, 2 * nb)], 1.0))
                        + _ones_lanes((1, LANES), h * nb, nb))
                qh[h] = jnp.concatenate([qh[h], feat.astype(BF16)], axis=1)
        return qh

    krow = lax.broadcasted_iota(jnp.int32, (chunk, tile), 0)
    qcol = lax.broadcasted_iota(jnp.int32, (chunk, tile), 1)

    def scores(slot, qh, j):
        kblk = kb_ref[j * tile:(j + 1) * tile, :]
        for h in range(2):
            s_ref[slot, h] = _dot_nt(kblk, qh[h])

    def values(slot, par, j, alphas):
        hd = LANES // 2
        for h in range(2):
            pv = _dot(vtb_ref[h * hd:(h + 1) * hd, j * tile:(j + 1) * tile], p_ref[slot, h])
            acc_ref[par, h] = pv if alphas is None else acc_ref[par, h] * alphas[h] + pv

    def softmax(slot, stats, diagonal):
        out = []
        alphas = []
        for h in range(2):
            mx = None
            for c in range(tile // chunk):
                rows = slice(c * chunk, (c + 1) * chunk)
                blk = s_ref[slot, h, rows, :]
                if diagonal:
                    blk = jnp.where(krow + c * chunk > qcol, NEG_INF, blk)
                    s_ref[slot, h, rows, :] = blk
                cm = _group8(blk, jnp.maximum)
                mx = cm if mx is None else jnp.maximum(mx, cm)
            m_new = jnp.max(mx, axis=0, keepdims=True)
            if stats is not None:
                m_old = stats[h][0]
                m_new = jnp.maximum(m_old, m_new)
                alphas.append(jnp.exp2(m_old - m_new))
            ls = None
            for c in range(tile // chunk):
                rows = slice(c * chunk, (c + 1) * chunk)
                p = jnp.exp2(s_ref[slot, h, rows, :] - m_new)
                ps = _group8(p, jnp.add)
                ls = ps if ls is None else ls + ps
                p_ref[slot, h, rows, :] = p.astype(BF16)
            l = jnp.sum(ls, axis=0, keepdims=True)
            if stats is not None:
                l = stats[h][1] * alphas[-1] + l
            out.append((m_new, l))
        return out, (None if stats is None else alphas)

    def finish(i, stats):
        par = i % 2
        o_t = jnp.concatenate([acc_ref[par, h] / stats[h][1] for h in range(2)], axis=0)
        o_ref[i * tile:(i + 1) * tile, :] = o_t.T.astype(o_ref.dtype)

    blocks = [(i, j) for i in range(n_tiles) for j in range(i + 1)]
    qh_of = {0: q_operands(0)}
    scores(0, qh_of[0], 0)
    stats = None
    pending = None
    for n, (i, j) in enumerate(blocks):
        if n + 1 < len(blocks):
            i2, j2 = blocks[n + 1]
            if i2 not in qh_of:
                qh_of[i2] = q_operands(i2)
            scores((n + 1) % 2, qh_of[i2], j2)
        if pending is not None:
            slot_p, i_p, j_p, alphas_p, stats_p = pending
            values(slot_p, i_p % 2, j_p, alphas_p)
            if j_p == i_p:
                finish(i_p, stats_p)
        stats, alphas = softmax(n % 2, None if j == 0 else stats, j == i)
        pending = (n % 2, i, j, alphas, stats)
    slot_p, i_p, j_p, alphas_p, stats_p = pending
    values(slot_p, i_p % 2, j_p, alphas_p)
    finish(i_p, stats_p)


FLASH_TILE = 512


def _flash(qs, ks, vt, b, s, c=None):
    tile = _row_tile(s, FLASH_TILE)
    use_c = c is not None
    kd = LANES * (len(ks) + (1 if use_c else 0))
    rows = pl.BlockSpec((s, LANES), lambda i, p: (i, p))
    shared = pl.BlockSpec((s, LANES), lambda i, p: (i, 0))
    in_specs = [rows] * len(qs) + [rows if k.shape[1] > LANES else shared for k in ks]
    in_specs.append(pl.BlockSpec((1, LANES, s), lambda i, p: (i, p, 0)))
    args = list(qs) + list(ks) + [vt]
    if use_c:
        in_specs.append(pl.BlockSpec((1, s, LANES), lambda i, p: (i, 0, 0)))
        args.append(c)
    kern = functools.partial(_flash_kernel, tile=tile, n_q=len(qs), n_k=len(ks), use_c=use_c)
    return pl.pallas_call(
        kern,
        grid=(b, PAIRS),
        in_specs=in_specs,
        out_specs=rows,
        out_shape=jax.ShapeDtypeStruct((b * s, PAIRS * LANES), BF16),
        scratch_shapes=[pltpu.VMEM((s, kd), BF16), pltpu.VMEM((LANES, s), BF16),
                        pltpu.VMEM((2, 2, tile, tile), F32),
                        pltpu.VMEM((2, 2, tile, tile), BF16),
                        pltpu.VMEM((2, 2, LANES // 2, tile), F32)],
        compiler_params=_params("parallel", "parallel"),
        name="flash_mla" if len(qs) > 1 else "flash_fox",
    )(*args)


def _cross_kernel(q_ref, k_ref, v_ref, o_ref):
    q = q_ref[0].astype(BF16)
    kb = k_ref[0].astype(BF16)
    vb = v_ref[0].astype(BF16)
    lane = lax.broadcasted_iota(jnp.int32, (1, LANES), 1)
    first = lane < X_HEAD_DIM
    zero = jnp.zeros_like(q)
    outs = []
    for qh in (jnp.where(first, q, zero), jnp.where(first, zero, q)):
        s = _dot_nt(qh, kb)
        m = jnp.max(s, axis=1, keepdims=True)
        p = jnp.exp(s - m)
        l = jnp.sum(p, axis=1, keepdims=True)
        outs.append(_dot(p.astype(BF16), vb) / l)
    o_ref[0] = jnp.where(first, outs[0], outs[1]).astype(o_ref.dtype)


def _cross(xq, mk, mv):
    b, t, _ = xq.shape
    m = mk.shape[1]
    tq = _row_tile(t, 2048)
    qspec = pl.BlockSpec((1, tq, LANES), lambda i, p, j: (i, j, p))
    kspec = pl.BlockSpec((1, m, LANES), lambda i, p, j: (i, 0, p))
    return pl.pallas_call(
        _cross_kernel,
        grid=(b, X_WIDTH // LANES, t // tq),
        in_specs=[qspec, kspec, kspec],
        out_specs=qspec,
        out_shape=jax.ShapeDtypeStruct((b, t, X_WIDTH), BF16),
        compiler_params=_params("parallel", "parallel", "arbitrary"),
        name="cross_attn",
    )(xq, mk, mv)


def _cross_t_kernel(q_ref, kt_ref, vt_ref, o_ref):
    lane = lax.broadcasted_iota(jnp.int32, (1, LANES), 1)
    first = lane < X_HEAD_DIM
    for g in range(q_ref.shape[0]):
        q = q_ref[g].astype(BF16)
        halves = []
        for p in range(X_WIDTH // LANES):
            cols = slice(p * LANES, (p + 1) * LANES)
            qp = q[:, cols]
            ktb = kt_ref[g, cols, :].astype(BF16)
            vtb = vt_ref[g, cols, :].astype(BF16)
            zero = jnp.zeros_like(qp)
            outs = []
            for qh in (jnp.where(first, qp, zero), jnp.where(first, zero, qp)):
                s = _dot(qh, ktb)
                mx = jnp.max(s, axis=1, keepdims=True)
                pr = jnp.exp(s - mx)
                l = jnp.sum(pr, axis=1, keepdims=True)
                outs.append(_dot_nt(pr.astype(BF16), vtb) / l)
            halves.append(jnp.where(first, outs[0], outs[1]))
        o_ref[g] = jnp.concatenate(halves, axis=1).astype(o_ref.dtype)


CROSS_T_BATCHES = 8


def _cross_t(xq, mkt, mvt, layer):
    b, t, _ = xq.shape
    m = mkt.shape[3]
    g = CROSS_T_BATCHES if b % CROSS_T_BATCHES == 0 else 1
    qspec = pl.BlockSpec((g, t, X_WIDTH), lambda i: (i, 0, 0))
    kspec = pl.BlockSpec((None, g, X_WIDTH, m), lambda i: (layer, i, 0, 0))
    return pl.pallas_call(
        _cross_t_kernel,
        grid=(b // g,),
        in_specs=[qspec, kspec, kspec],
        out_specs=qspec,
        out_shape=jax.ShapeDtypeStruct((b, t, X_WIDTH), BF16),
        compiler_params=_params("parallel"),
        name="cross_attn_t",
    )(xq, mkt, mvt)


def _norm_proj_kernel(x_ref, g_ref, w_ref, o_ref):
    o_ref[...] = _dot(_rms(x_ref[...], g_ref[...]).astype(BF16), w_ref[...])


def _norm_proj(x, g, w):
    t = x.shape[0]
    n = w.shape[1]
    tm = _row_tile(t, 512)
    return pl.pallas_call(
        _norm_proj_kernel,
        grid=(t // tm,),
        in_specs=[pl.BlockSpec((tm, D_MODEL), lambda m: (m, 0)),
                  pl.BlockSpec((1, D_MODEL), lambda m: (0, 0)),
                  pl.BlockSpec(w.shape, lambda m: (0, 0))],
        out_specs=pl.BlockSpec((tm, n), lambda m: (m, 0)),
        out_shape=jax.ShapeDtypeStruct((t, n), F32),
        compiler_params=_params("parallel"),
        name="norm_proj",
    )(x, g, w)


def _mla_proj_kernel(*refs, qscale, with_kv):
    (x_ref, g_ref, wqa_ref, wkva_ref, wkr_ref, wkrr_ref, wx_ref,
     qg_ref, wqn_ref, wqr_ref, wqrr_ref, kvg_ref) = refs[:12]
    pos = 12
    if with_kv:
        wuk_ref, wuvt_ref = refs[pos:pos + 2]
        pos += 2
    cos_ref, sin_ref = refs[pos:pos + 2]
    qn_ref, qr_ref, c_ref, kr_ref, xq_ref = refs[pos + 2:pos + 7]
    hn = _rms(x_ref[...], g_ref[...]).astype(BF16)
    cos = cos_ref[...]
    sin = sin_ref[...]
    qa = _rms(_dot(hn, wqa_ref[...]), qg_ref[...]).astype(BF16)
    qn_ref[...] = (_dot(qa, wqn_ref[...]) * qscale).astype(BF16)
    qr = _dot(qa, wqr_ref[...])
    qrr = _dot(qa, wqrr_ref[...])
    for p in range(PAIRS):
        sl = slice(p * LANES, (p + 1) * LANES)
        qr_ref[:, sl] = ((qr[:, sl] * cos + qrr[:, sl] * sin) * qscale).astype(BF16)
    c = _rms(_dot(hn, wkva_ref[...]), kvg_ref[...])
    c_ref[...] = c
    if with_kv:
        kn_ref, vt_ref = refs[pos + 7:pos + 9]
        cb = c.astype(BF16)
        kn_ref[...] = _dot(cb, wuk_ref[...]).astype(BF16)
        vt_ref[0] = _dot_nt(wuvt_ref[...], cb).astype(BF16)
    kr = _dot(hn, wkr_ref[...]) * cos + _dot(hn, wkrr_ref[...]) * sin
    kr_ref[...] = kr[:, :MLA_ROPE]
    xq_ref[...] = (_dot(hn, wx_ref[...]) * (X_HEAD_DIM ** -0.5)).astype(BF16)


def _mla_proj(x, g, w, cos, sin, qscale, kv_w=None, b=None, s=None):
    t = x.shape[0]
    with_kv = kv_w is not None
    tm = _row_tile(s if with_kv else t, 512)
    ntab = cos.shape[0] // tm
    row = lambda n: pl.BlockSpec((tm, n), lambda m: (m, 0))
    full = lambda a: pl.BlockSpec(a.shape, lambda m: (0,) * a.ndim)
    tab = pl.BlockSpec((tm, LANES), lambda m: (m % ntab, 0))
    wide = PAIRS * LANES
    w = list(w) + (list(kv_w) if with_kv else [])
    out_specs = [row(wide), row(wide), row(MLA_KV_LORA), row(MLA_ROPE), row(X_WIDTH)]
    out_shape = [
        jax.ShapeDtypeStruct((t, wide), BF16),
        jax.ShapeDtypeStruct((t, wide), BF16),
        jax.ShapeDtypeStruct((t, MLA_KV_LORA), F32),
        jax.ShapeDtypeStruct((t, MLA_ROPE), F32),
        jax.ShapeDtypeStruct((t, X_WIDTH), BF16),
    ]
    if with_kv:
        ns = s // tm
        out_specs += [row(wide), pl.BlockSpec((1, wide, tm), lambda m: (m // ns, 0, m % ns))]
        out_shape += [jax.ShapeDtypeStruct((t, wide), BF16),
                      jax.ShapeDtypeStruct((b, wide, s), BF16)]
    return pl.pallas_call(
        functools.partial(_mla_proj_kernel, qscale=qscale, with_kv=with_kv),
        grid=(t // tm,),
        in_specs=[row(D_MODEL), full(g)] + [full(a) for a in w] + [tab, tab],
        out_specs=out_specs,
        out_shape=out_shape,
        compiler_params=_params("parallel"),
        name="mla_proj_kv" if with_kv else "mla_proj",
    )(x, g, *w, cos, sin)


FOX_PAGES_PER_STEP = 16
MLA_PAGES_PER_STEP = 64
PAGE_GROUPS = 4


def _pages_per_step(n_pages, pref):
    while n_pages % pref:
        pref //= 2
    return pref


def _online_step(state, s_parts, v_parts, v_is_transposed):
    m, l, acc = state
    n = len(s_parts)
    per = n // PAGE_GROUPS if n % PAGE_GROUPS == 0 else n
    partials = []
    for g0 in range(0, n, per):
        s = jnp.concatenate(s_parts[g0:g0 + per], axis=1) if per > 1 else s_parts[g0]
        mg = jnp.max(s, axis=1, keepdims=True)
        p = jnp.exp(s - mg)
        lg = jnp.sum(p, axis=1, keepdims=True)
        ag = None
        for r in range(per):
            pb = p[:, r * LANES:(r + 1) * LANES].astype(BF16)
            vb = v_parts[g0 + r]
            t = _dot_nt(pb, vb) if v_is_transposed else _dot(pb, vb)
            ag = t if ag is None else ag + t
        partials.append((mg, lg, ag))
    m_new = m
    for mg, _, _ in partials:
        m_new = jnp.maximum(m_new, mg)
    alpha = jnp.exp(m - m_new)
    l = l * alpha
    acc = acc * alpha
    for mg, lg, ag in partials:
        w = jnp.exp(mg - m_new)
        l = l + lg * w
        acc = acc + ag * w
    return m_new, l, acc


def _head_diag(acc, t_new, width):
    n = acc.shape[1]
    rowh = lax.broadcasted_iota(jnp.int32, (HEAD_PAD, n), 0)
    laneh = lax.broadcasted_iota(jnp.int32, (HEAD_PAD, n), 1) // width
    orow = lax.broadcasted_iota(jnp.int32, (t_new, n), 0)
    out = jnp.zeros((t_new, n), F32)
    for i in range(t_new):
        blk = acc[i * HEAD_PAD:(i + 1) * HEAD_PAD, :]
        r = jnp.sum(jnp.where(rowh == laneh, blk, 0.0), axis=0, keepdims=True)
        out = jnp.where(orow == i, r, out)
    return out


def _fox_dec_kernel(pt_ref, qb_ref, kn_ref, vn_ref, lft_ref, sufm_ref, *rest, pb, t_new):
    kt_refs = rest[:pb]
    vt_refs = rest[pb:2 * pb]
    lf_refs = rest[2 * pb:3 * pb]
    o_ref, m_ref, l_ref, acc_ref, run_ref, pad_ref, lfp_ref = rest[3 * pb:]
    step = pl.program_id(1)
    qb = qb_ref[0]
    lft = lft_ref[0]
    lane = lax.broadcasted_iota(jnp.int32, (HEAD_PAD, LANES), 1)
    cn_cols = []
    run = jnp.zeros((HEAD_PAD, 1), F32)
    for i in range(t_new):
        run = run + lft[:, i:i + 1]
        cn_cols.append(run)
    cn_col = jnp.concatenate(cn_cols, axis=0)

    @pl.when(step == 0)
    def _():
        cn_lanes = jnp.zeros((HEAD_PAD, LANES), F32)
        for i in range(t_new):
            cn_lanes = jnp.where(lane == i, cn_cols[i], cn_lanes)
        pad_ref[...] = jnp.zeros_like(pad_ref)
        pad_ref[0:t_new, :] = kn_ref[0]
        kpad = pad_ref[...].astype(BF16)
        pad_ref[0:t_new, :] = vn_ref[0]
        vpad = pad_ref[...].astype(BF16)
        s = _dot_nt(qb, kpad)
        parts = []
        for i in range(t_new):
            blk = s[i * HEAD_PAD:(i + 1) * HEAD_PAD, :] + (cn_cols[i] - cn_lanes)
            parts.append(jnp.where(lane <= i, blk, NEG_INF))
        s = jnp.concatenate(parts, axis=0)
        m = jnp.max(s, axis=1, keepdims=True)
        p = jnp.exp(s - m)
        m_ref[...] = m
        l_ref[...] = jnp.sum(p, axis=1, keepdims=True)
        acc_ref[...] = _dot(p.astype(BF16), vpad)
        run_ref[...] = jnp.zeros_like(run_ref)
        lfp_ref[...] = jnp.zeros_like(lfp_ref)

    run = run_ref[...]
    s_parts, v_parts = [], []
    for r in range(pb):
        lfp_ref[r * HEAD_PAD:r * HEAD_PAD + FOX_HEADS, :] = lf_refs[r][0]
    lf_all = lfp_ref[...]
    suf_all = _dot_exact(lf_all, sufm_ref[...])
    tot_all = jnp.sum(lf_all, axis=1, keepdims=True)
    for r in range(pb):
        heads = slice(r * HEAD_PAD, (r + 1) * HEAD_PAD)
        suf = suf_all[heads, :] + run
        run = run + tot_all[heads, :]
        ktb = kt_refs[r][0, 0].reshape(FOX_WIDTH, PAGE_SIZE).astype(BF16)
        s = _dot(qb, ktb)
        bias = jnp.concatenate([suf] * t_new, axis=0) + cn_col
        s_parts.append(s + bias)
        v_parts.append(vt_refs[r][0, 0].reshape(FOX_WIDTH, PAGE_SIZE).astype(BF16))
    run_ref[...] = run
    m, l, acc = _online_step((m_ref[...], l_ref[...], acc_ref[...]), s_parts, v_parts, True)
    m_ref[...] = m
    l_ref[...] = l
    acc_ref[...] = acc

    @pl.when(step == pl.num_programs(1) - 1)
    def _():
        o_ref[0] = _head_diag(acc / l, t_new, FOX_HEAD_DIM)


def _fox_decode(page_table, qblk, k_new, v_new, lft_new, cache_kt, cache_vt, cache_lf):
    bd, n_pages = page_table.shape
    t_new = k_new.shape[1]
    rows = t_new * HEAD_PAD
    pb = _pages_per_step(n_pages, FOX_PAGES_PER_STEP)
    sufm = jnp.asarray(np.tril(np.ones((PAGE_SIZE, PAGE_SIZE), np.float32), -1))

    def page(r):
        return lambda b, s, pt: pt[b, n_pages - 1 - (s * pb + r)]

    kv_specs = [pl.BlockSpec((1, 1, FOX_HEADS, FOX_HEAD_DIM, PAGE_SIZE),
                             (lambda b, s, pt, f=page(r): (0, f(b, s, pt), 0, 0, 0)))
                for r in range(pb)]
    lf_specs = [pl.BlockSpec((1, FOX_HEADS, PAGE_SIZE),
                             (lambda b, s, pt, f=page(r): (f(b, s, pt), 0, 0)))
                for r in range(pb)]
    per_b = lambda shape: pl.BlockSpec((1,) + shape, lambda b, s, pt: (b, 0, 0))
    grid_spec = pltpu.PrefetchScalarGridSpec(
        num_scalar_prefetch=1,
        grid=(bd, n_pages // pb),
        in_specs=[per_b((rows, FOX_WIDTH)), per_b((t_new, FOX_WIDTH)), per_b((t_new, FOX_WIDTH)),
                  per_b((HEAD_PAD, LANES)),
                  pl.BlockSpec((PAGE_SIZE, PAGE_SIZE), lambda b, s, pt: (0, 0))]
        + kv_specs + kv_specs + lf_specs,
        out_specs=per_b((t_new, FOX_WIDTH)),
        scratch_shapes=[pltpu.VMEM((rows, 1), F32), pltpu.VMEM((rows, 1), F32),
                        pltpu.VMEM((rows, FOX_WIDTH), F32), pltpu.VMEM((HEAD_PAD, 1), F32),
                        pltpu.VMEM((PAGE_SIZE, FOX_WIDTH), F32),
                        pltpu.VMEM((pb * HEAD_PAD, PAGE_SIZE), F32)],
    )
    return pl.pallas_call(
        functools.partial(_fox_dec_kernel, pb=pb, t_new=t_new),
        grid_spec=grid_spec,
        out_shape=jax.ShapeDtypeStruct((bd, t_new, FOX_WIDTH), F32),
        compiler_params=_params("parallel", "arbitrary"),
        name="fox_decode",
    )(page_table, qblk, k_new, v_new, lft_new, sufm,
      *([cache_kt] * pb), *([cache_vt] * pb), *([cache_lf] * pb))


def _mla_dec_kernel(pt_ref, qn_ref, qr_ref, cn_ref, krn_ref, wuk_ref, wuv_ref, *rest, pb, t_new):
    c_refs = rest[:pb]
    kr_refs = rest[pb:2 * pb]
    o_ref, m_ref, l_ref, acc_ref, ql_ref, cpad_ref = rest[2 * pb:]
    step = pl.program_id(1)
    lane = lax.broadcasted_iota(jnp.int32, (HEAD_PAD, LANES), 1)
    qr = qr_ref[0]

    @pl.when(step == 0)
    def _():
        ql = _dot(qn_ref[0], wuk_ref[...]).astype(BF16)
        ql_ref[...] = ql
        cpad_ref[...] = jnp.zeros_like(cpad_ref)
        cpad_ref[0:t_new, :] = cn_ref[0]
        cpad = cpad_ref[...].astype(BF16)
        s = _dot_nt(ql, cpad) + _dot_nt(qr, krn_ref[0])
        parts = [jnp.where(lane <= i, s[i * HEAD_PAD:(i + 1) * HEAD_PAD, :], NEG_INF)
                 for i in range(t_new)]
        s = jnp.concatenate(parts, axis=0)
        m = jnp.max(s, axis=1, keepdims=True)
        p = jnp.exp(s - m)
        m_ref[...] = m
        l_ref[...] = jnp.sum(p, axis=1, keepdims=True)
        acc_ref[...] = _dot(p.astype(BF16), cpad)

    ql = ql_ref[...]
    s_parts, v_parts = [], []
    kr_pad = jnp.zeros((LANES - MLA_ROPE, PAGE_SIZE), BF16)
    for r in range(pb):
        cb = c_refs[r][0, 0].astype(BF16)
        krt = jnp.concatenate([kr_refs[r][0, 0].astype(BF16), kr_pad], axis=0)
        s_parts.append(_dot_nt(ql, cb) + _dot(qr, krt))
        v_parts.append(cb)
    m, l, acc = _online_step((m_ref[...], l_ref[...], acc_ref[...]), s_parts, v_parts, False)
    m_ref[...] = m
    l_ref[...] = l
    acc_ref[...] = acc

    @pl.when(step == pl.num_programs(1) - 1)
    def _():
        full = _dot((acc / l).astype(BF16), wuv_ref[...])
        o_ref[0] = _head_diag(full, t_new, MLA_V)


def _mla_decode(page_table, qnblk, qrblk, c_new, kr_new, wuk_t, wuv, cache_c, cache_krt):
    bd, n_pages = page_table.shape
    t_new = c_new.shape[1]
    rows = t_new * HEAD_PAD
    pb = _pages_per_step(n_pages, MLA_PAGES_PER_STEP)

    def page(r):
        return lambda b, s, pt: pt[b, s * pb + r]

    c_specs = [pl.BlockSpec((1, 1, PAGE_SIZE, MLA_KV_LORA),
                            (lambda b, s, pt, f=page(r): (0, f(b, s, pt), 0, 0)))
               for r in range(pb)]
    kr_specs = [pl.BlockSpec((1, 1, MLA_ROPE, PAGE_SIZE),
                             (lambda b, s, pt, f=page(r): (0, f(b, s, pt), 0, 0)))
                for r in range(pb)]
    per_b = lambda shape: pl.BlockSpec((1,) + shape, lambda b, s, pt: (b, 0, 0))
    full = lambda a: pl.BlockSpec(a.shape, lambda b, s, pt: (0,) * a.ndim)
    wide = MLA_HEADS * MLA_V
    grid_spec = pltpu.PrefetchScalarGridSpec(
        num_scalar_prefetch=1,
        grid=(bd, n_pages // pb),
        in_specs=[per_b((rows, MLA_HEADS * MLA_NOPE)), per_b((rows, LANES)),
                  per_b((t_new, MLA_KV_LORA)), per_b((PAGE_SIZE, LANES)),
                  full(wuk_t), full(wuv)] + c_specs + kr_specs,
        out_specs=per_b((t_new, wide)),
        scratch_shapes=[pltpu.VMEM((rows, 1), F32), pltpu.VMEM((rows, 1), F32),
                        pltpu.VMEM((rows, MLA_KV_LORA), F32),
                        pltpu.VMEM((rows, MLA_KV_LORA), BF16),
                        pltpu.VMEM((PAGE_SIZE, MLA_KV_LORA), F32)],
    )
    return pl.pallas_call(
        functools.partial(_mla_dec_kernel, pb=pb, t_new=t_new),
        grid_spec=grid_spec,
        out_shape=jax.ShapeDtypeStruct((bd, t_new, wide), F32),
        compiler_params=_params("parallel", "arbitrary"),
        name="mla_decode",
    )(page_table, qnblk, qrblk, c_new, kr_new, wuk_t, wuv,
      *([cache_c] * pb), *([cache_krt] * pb))


def _rope_tables(pos):
    half = MLA_ROPE // 2
    inv = ROPE_THETA ** (-jnp.arange(half, dtype=F32) / half)
    ang = pos.astype(F32)[:, None] * inv
    reps = (LANES // 2) // half
    pad = jnp.zeros((pos.shape[0], LANES // 2), F32)
    cos = jnp.concatenate([jnp.tile(jnp.cos(ang), (1, reps)), pad], axis=1)
    sin = jnp.concatenate([jnp.tile(jnp.sin(ang), (1, reps)), pad], axis=1)
    return cos, sin


def _rot_cols(w):
    half = MLA_ROPE // 2
    return jnp.concatenate([-w[..., half:], w[..., :half]], axis=-1)


def _pair_rope_cols(w):
    r = w.shape[0]
    w = w.reshape(r, PAIRS, 2 * MLA_ROPE)
    return jnp.pad(w, ((0, 0), (0, 0), (0, LANES - 2 * MLA_ROPE))).reshape(r, PAIRS * LANES)


def _block_diag_rows(q, width):
    bd, t, n = q.shape
    rowh = np.arange(t * HEAD_PAD) % HEAD_PAD
    mask = jnp.asarray(rowh[:, None] == (np.arange(n) // width)[None, :])
    rep = jnp.repeat(q, HEAD_PAD, axis=1)
    return jnp.where(mask[None], rep, jnp.zeros_like(rep))


def kernel(x_prompt, x_sample, cache_fox_k, cache_fox_v, cache_fox_logf, cache_mla_ckv,
           cache_mla_krope, cache_mem_k, cache_mem_v, page_table, mem_prompt, norm_gains,
           ffn_w_up, ffn_w_down, fox_w_in, fox_b_f, mla_w_in, mla_q_norm, mla_w_q_b,
           mla_kv_norm, mla_w_kv_b, mem_norm, w_mem_kv, w_out, final_norm):
    b, s, d = x_prompt.shape
    bd, t_new, _ = x_sample.shape
    n_pages = page_table.shape[1]
    mem_len = mem_prompt.shape[1]
    depth = norm_gains.shape[0]
    xp = x_prompt.reshape(b * s, d)
    xs = x_sample.reshape(bd * t_new, d)

    w_up = ffn_w_up.astype(BF16)
    w_down = ffn_w_down.astype(BF16)
    w_out_b = w_out.astype(BF16)
    w_mem_b = w_mem_kv.astype(BF16)
    gains = norm_gains.reshape(depth, 3, 1, d)
    mem2d = mem_prompt.reshape(b * mem_len, d)
    mkt_s = jnp.transpose(cache_mem_k, (0, 1, 3, 4, 2)).reshape(depth, bd, X_WIDTH, mem_len)
    mvt_s = jnp.transpose(cache_mem_v, (0, 1, 3, 4, 2)).reshape(depth, bd, X_WIDTH, mem_len)

    outs = {}
    for layer in range(depth):
        j = layer // 2
        xp = _ffn_half(xp, gains[layer, 0], w_up, w_down, layer, 0)
        xs = _ffn_half(xs, gains[layer, 0], w_up, w_down, layer, 0)

        mkv = _norm_proj(mem2d, mem_norm[layer].reshape(1, d), w_mem_b[layer])
        mk = mkv[:, :X_WIDTH].reshape(b, mem_len, X_WIDTH)
        mv = mkv[:, X_WIDTH:].reshape(b, mem_len, X_WIDTH)
        outs.setdefault("mk", []).append(mk.reshape(b, mem_len, X_HEADS, X_HEAD_DIM))
        outs.setdefault("mv", []).append(mv.reshape(b, mem_len, X_HEADS, X_HEAD_DIM))

        if layer % 2 == 0:
            w_in = fox_w_in[j]
            wqkv = w_in[:, :3 * FOX_WIDTH].astype(BF16)
            wf = jnp.pad(w_in[:, 3 * FOX_WIDTH:3 * FOX_WIDTH + FOX_HEADS],
                         ((0, 0), (0, LANES - FOX_HEADS))).astype(BF16)
            bf = jnp.pad(fox_b_f[j], (0, LANES - FOX_HEADS)).reshape(1, LANES)
            wx = w_in[:, 3 * FOX_WIDTH + FOX_HEADS:].astype(BF16)
            g1 = gains[layer, 1]

            wqk = w_in[:, :2 * FOX_WIDTH].astype(BF16)
            wkvt = jnp.transpose(w_in[:, FOX_WIDTH:3 * FOX_WIDTH]).astype(BF16)
            q, kb, kt, vt, lf, cx, xq_p = _fox_proj_t(
                xp, g1, wqk, wkvt, wf, bf, wx, FOX_HEAD_DIM ** -0.5 * LOG2E, b, s)
            mix_p = _flash([q], [kb], vt, b, s, cx.reshape(b, s, LANES))
            heads_t = (0, 1, 4, 2, 3)
            outs["fk_p"] = jnp.transpose(kt.reshape(1, b, FOX_HEADS, FOX_HEAD_DIM, s), heads_t)
            outs["fv_p"] = jnp.transpose(vt.reshape(1, b, FOX_HEADS, FOX_HEAD_DIM, s), heads_t)
            outs["fl_p"] = lf[:, :FOX_HEADS].reshape(1, b, s, FOX_HEADS)

            q, k, v, lf, xq_s = _fox_proj(xs, g1, wqkv, wf, bf, wx, FOX_HEAD_DIM ** -0.5)
            outs["fk_s"] = k.reshape(1, bd, t_new, FOX_HEADS, FOX_HEAD_DIM)
            outs["fv_s"] = v.reshape(1, bd, t_new, FOX_HEADS, FOX_HEAD_DIM)
            outs["fl_s"] = lf[:, :FOX_HEADS].reshape(1, bd, t_new, FOX_HEADS)
            qblk = _block_diag_rows(q.reshape(bd, t_new, FOX_WIDTH), FOX_HEAD_DIM)
            lft = jnp.transpose(lf.reshape(bd, t_new, LANES)[:, :, :HEAD_PAD], (0, 2, 1))
            lft = jnp.pad(lft, ((0, 0), (0, 0), (0, LANES - t_new)))
            cache_kt = jnp.transpose(cache_fox_k, (0, 1, 3, 4, 2))
            cache_vt = jnp.transpose(cache_fox_v, (0, 1, 3, 4, 2))
            cache_lf = jnp.transpose(cache_fox_logf[j], (0, 2, 1))
            mix_s = _fox_decode(page_table, qblk, k.reshape(bd, t_new, FOX_WIDTH),
                                v.reshape(bd, t_new, FOX_WIDTH), lft,
                                cache_kt[j:j + 1], cache_vt[j:j + 1], cache_lf)
            mix_s = mix_s.reshape(bd * t_new, FOX_WIDTH).astype(BF16)
        else:
            w_in = mla_w_in[j]
            o1 = MLA_Q_LORA
            o2 = o1 + MLA_KV_LORA
            o3 = o2 + MLA_ROPE
            wkr = w_in[:, o2:o3]
            lane_pad = ((0, 0), (0, LANES - MLA_ROPE))
            wqb = mla_w_q_b[j]
            wkvb = mla_w_kv_b[j]
            weights = [
                w_in[:, :o1].astype(BF16),
                w_in[:, o1:o2].astype(BF16),
                jnp.pad(wkr, lane_pad).astype(BF16),
                jnp.pad(_rot_cols(wkr), lane_pad).astype(BF16),
                w_in[:, o3:].astype(BF16),
                mla_q_norm[j].reshape(1, MLA_Q_LORA),
                wqb[:, :, :MLA_NOPE].reshape(MLA_Q_LORA, MLA_HEADS * MLA_NOPE).astype(BF16),
                _pair_rope_cols(wqb[:, :, MLA_NOPE:]).astype(BF16),
                _pair_rope_cols(_rot_cols(wqb[:, :, MLA_NOPE:])).astype(BF16),
                mla_kv_norm[j].reshape(1, MLA_KV_LORA),
            ]
            wuk = wkvb[:, :, :MLA_NOPE].reshape(MLA_KV_LORA, MLA_HEADS * MLA_NOPE).astype(BF16)
            wuv = wkvb[:, :, MLA_NOPE:].reshape(MLA_KV_LORA, MLA_HEADS * MLA_V).astype(BF16)
            wuv_t = jnp.transpose(wkvb[:, :, MLA_NOPE:], (1, 2, 0)).reshape(
                MLA_HEADS * MLA_V, MLA_KV_LORA).astype(BF16)
            mla_scale = (MLA_NOPE + MLA_ROPE) ** -0.5
            g1 = gains[layer, 1]
            cos_p, sin_p = _rope_tables(jnp.arange(s, dtype=jnp.int32))
            pos_s = n_pages * PAGE_SIZE + jnp.arange(t_new, dtype=jnp.int32)
            cos_s, sin_s = _rope_tables(jnp.tile(pos_s, bd))

            qn, qr, c, kr, xq_p, kn, vt = _mla_proj(
                xp, g1, weights, cos_p, sin_p, mla_scale * LOG2E, (wuk, wuv_t), b, s)
            kr2 = jnp.concatenate(
                [kr, kr, jnp.zeros((b * s, LANES - 2 * MLA_ROPE), F32)], axis=1).astype(BF16)
            mix_p = _flash([qn, qr], [kn, kr2], vt, b, s)
            outs["mc_p"] = c.reshape(1, b, s, MLA_KV_LORA)
            outs["mr_p"] = kr.reshape(1, b, s, MLA_ROPE)

            qn, qr, c, kr, xq_s = _mla_proj(xs, g1, weights, cos_s, sin_s, mla_scale)
            outs["mc_s"] = c.reshape(1, bd, t_new, MLA_KV_LORA)
            outs["mr_s"] = kr.reshape(1, bd, t_new, MLA_ROPE)
            qnblk = _block_diag_rows(qn.reshape(bd, t_new, MLA_HEADS * MLA_NOPE), MLA_NOPE)
            qr4 = qr.reshape(bd, t_new, PAIRS, LANES)[..., :2 * MLA_ROPE]
            qr4 = qr4.reshape(bd, t_new, MLA_HEADS, MLA_ROPE)
            qr4 = jnp.pad(qr4, ((0, 0), (0, 0), (0, HEAD_PAD - MLA_HEADS), (0, LANES - MLA_ROPE)))
            qrblk = qr4.reshape(bd, t_new * HEAD_PAD, LANES)
            kr_new = jnp.pad(kr.reshape(bd, t_new, MLA_ROPE).astype(BF16),
                             ((0, 0), (0, PAGE_SIZE - t_new), (0, LANES - MLA_ROPE)))
            wuk_t = jnp.transpose(wkvb[:, :, :MLA_NOPE], (1, 2, 0)).reshape(
                MLA_HEADS * MLA_NOPE, MLA_KV_LORA).astype(BF16)
            cache_krt = jnp.transpose(cache_mla_krope, (0, 1, 3, 2))
            mix_s = _mla_decode(page_table, qnblk, qrblk, c.reshape(bd, t_new, MLA_KV_LORA),
                                kr_new, wuk_t, wuv, cache_mla_ckv[j:j + 1],
                                cache_krt[j:j + 1])
            mix_s = mix_s.reshape(bd * t_new, MLA_HEADS * MLA_V).astype(BF16)

        cross_p = _cross(xq_p.reshape(b, s, X_WIDTH), mk, mv).reshape(b * s, X_WIDTH)
        cross_s = _cross_t(xq_s.reshape(bd, t_new, X_WIDTH), mkt_s, mvt_s, layer).reshape(
            bd * t_new, X_WIDTH)
        fin = final_norm.reshape(1, d) if layer == depth - 1 else None
        xp = _ffn_half(xp, gains[layer, 2], w_up, w_down, layer, 1,
                       (mix_p, cross_p, w_out_b[layer]), fin)
        xs = _ffn_half(xs, gains[layer, 2], w_up, w_down, layer, 1,
                       (mix_s, cross_s, w_out_b[layer]), fin)

    y_prompt = xp.reshape(b, s, d)
    y_sample = xs.reshape(bd, t_new, d)
    return (y_prompt, y_sample,
            outs["fk_p"], outs["fv_p"], outs["fl_p"],
            outs["fk_s"], outs["fv_s"], outs["fl_s"],
            outs["mc_p"], outs["mr_p"], outs["mc_s"], outs["mr_s"],
            jnp.stack(outs["mk"]), jnp.stack(outs["mv"]))
```

```python
import functools

import numpy as np
import jax
import jax.numpy as jnp
from jax import lax
from jax.experimental import pallas as pl
from jax.experimental.pallas import tpu as pltpu

D_MODEL = 1024
PAGE_SIZE = 128
X_HEADS = 4
X_HEAD_DIM = 64
X_WIDTH = X_HEADS * X_HEAD_DIM
FOX_HEADS = 12
FOX_HEAD_DIM = 64
FOX_WIDTH = FOX_HEADS * FOX_HEAD_DIM
MLA_HEADS = 12
MLA_NOPE = 64
MLA_ROPE = 32
MLA_V = 64
MLA_Q_LORA = 384
MLA_KV_LORA = 256
FFN_HIDDEN = 2816
ROPE_THETA = 10000.0
RMS_EPS = 1e-6

LANES = 128
HEAD_PAD = 16
PAIRS = FOX_HEADS // 2
VMEM_LIMIT = 56 * 1024 * 1024

F32 = jnp.float32
BF16 = jnp.bfloat16
NT_DIMS = (((1,), (1,)), ((), ()))
NEG_INF = float("-inf")
LOG2E = 1.4426950408889634


def _params(*sem):
    return pltpu.CompilerParams(dimension_semantics=sem, vmem_limit_bytes=VMEM_LIMIT)


def _rms(x, g):
    ms = jnp.mean(x * x, axis=-1, keepdims=True)
    return x * lax.rsqrt(ms + RMS_EPS) * g


def _dot(a, b):
    return jnp.dot(a, b, preferred_element_type=F32)


def _dot_nt(a, b):
    return lax.dot_general(a, b, NT_DIMS, preferred_element_type=F32)


def _dot_exact(a, b):
    return jnp.dot(a, b, preferred_element_type=F32, precision=lax.Precision.HIGHEST)


def _row_tile(t, pref):
    return pref if t % pref == 0 else t


FFN_CHUNK = FFN_HIDDEN // 2


def _ffn_kernel(*refs, merge, final):
    x_ref = refs[0]
    pos = 1
    if merge:
        mix_ref, cross_ref, wo_ref = refs[1:4]
        pos = 4
    g_ref, wg_ref, wu_ref, wd_ref = refs[pos:pos + 4]
    pos += 4
    if final:
        gf_ref = refs[pos]
        pos += 1
    o_ref = refs[pos]
    x = x_ref[...]
    if merge:
        nm = mix_ref.shape[1]
        x = x + _dot(mix_ref[...], wo_ref[:nm, :]) + _dot(cross_ref[...], wo_ref[nm:, :])
    xn = _rms(x, g_ref[...]).astype(BF16)
    acc = None
    for c0 in range(0, FFN_HIDDEN, FFN_CHUNK):
        cols = slice(c0, c0 + FFN_CHUNK)
        gate = _dot(xn, wg_ref[:, cols])
        up = _dot(xn, wu_ref[:, cols])
        h = (gate / (1.0 + jnp.exp(-gate)) * up).astype(BF16)
        part = _dot(h, wd_ref[cols, :])
        acc = part if acc is None else acc + part
    y = x + 0.5 * acc
    o_ref[...] = _rms(y, gf_ref[...]) if final else y


def _ffn_half(x, g, w_up, w_down, layer, idx, merge=None, final_g=None):
    t = x.shape[0]
    tm = _row_tile(t, 512)
    row = lambda n: pl.BlockSpec((tm, n), lambda m: (m, 0))
    once = pl.Buffered(1)
    full = lambda a: pl.BlockSpec(a.shape, lambda m: (0,) * a.ndim, pipeline_mode=once)
    in_specs = [row(D_MODEL)]
    args = [x]
    if merge is not None:
        mix, cross, w_out = merge
        in_specs += [row(mix.shape[1]), row(cross.shape[1]), full(w_out)]
        args += [mix, cross, w_out]
    in_specs += [
        full(g),
        pl.BlockSpec((None, None, D_MODEL, FFN_HIDDEN), lambda m: (layer, idx, 0, 0),
                     pipeline_mode=once),
        pl.BlockSpec((None, None, D_MODEL, FFN_HIDDEN), lambda m: (layer, idx, 0, 1),
                     pipeline_mode=once),
        pl.BlockSpec((None, None, FFN_HIDDEN, D_MODEL), lambda m: (layer, idx, 0, 0),
                     pipeline_mode=once),
    ]
    args += [g, w_up, w_up, w_down]
    if final_g is not None:
        in_specs.append(full(final_g))
        args.append(final_g)
    return pl.pallas_call(
        functools.partial(_ffn_kernel, merge=merge is not None, final=final_g is not None),
        grid=(t // tm,),
        in_specs=in_specs,
        out_specs=row(D_MODEL),
        out_shape=jax.ShapeDtypeStruct((t, D_MODEL), F32),
        compiler_params=_params("parallel"),
        name="ffn_half",
    )(*args)


def _log_sigmoid(f):
    return jnp.minimum(f, 0.0) - jnp.log(1.0 + jnp.exp(-jnp.abs(f)))


def _fox_proj_kernel(x_ref, g_ref, wqkv_ref, wf_ref, bf_ref, wx_ref,
                     q_ref, k_ref, v_ref, lf_ref, xq_ref, *, qscale):
    hn = _rms(x_ref[...], g_ref[...]).astype(BF16)
    qkv = _dot(hn, wqkv_ref[...])
    q_ref[...] = (qkv[:, :FOX_WIDTH] * qscale).astype(BF16)
    k_ref[...] = qkv[:, FOX_WIDTH:2 * FOX_WIDTH]
    v_ref[...] = qkv[:, 2 * FOX_WIDTH:]
    lf_ref[...] = _log_sigmoid(_dot(hn, wf_ref[...]) + bf_ref[...])
    xq_ref[...] = (_dot(hn, wx_ref[...]) * (X_HEAD_DIM ** -0.5)).astype(BF16)


BIAS_TERMS = 3
TERM_STRIDE = 16


def _split3(x):
    hi = x.astype(BF16)
    r1 = x - hi.astype(F32)
    mid = r1.astype(BF16)
    lo = (r1 - mid.astype(F32)).astype(BF16)
    return hi, mid, lo


def _fox_proj_t_kernel(x_ref, g_ref, wqk_ref, wkvt_ref, wf_ref, bf_ref, wx_ref, tri_ref,
                       q_ref, k_ref, kt_ref, vt_ref, lf_ref, cx_ref, xq_ref, carry_ref,
                       *, qscale, tiles_per_seq):
    hn = _rms(x_ref[...], g_ref[...]).astype(BF16)
    qk = _dot(hn, wqk_ref[...])
    q_ref[...] = (qk[:, :FOX_WIDTH] * qscale).astype(BF16)
    k_ref[...] = qk[:, FOX_WIDTH:].astype(BF16)
    kvt = _dot_nt(wkvt_ref[...], hn)
    kt_ref[0] = kvt[:FOX_WIDTH, :]
    vt_ref[0] = kvt[FOX_WIDTH:, :]
    lf = _log_sigmoid(_dot(hn, wf_ref[...]) + bf_ref[...])
    lf_ref[...] = lf

    @pl.when(pl.program_id(0) % tiles_per_seq == 0)
    def _():
        carry_ref[...] = jnp.zeros_like(carry_ref)

    tri = tri_ref[...]
    span = tri.shape[0]
    carry = carry_ref[...]
    pieces = []
    for r0 in range(0, lf.shape[0], span):
        c = carry
        for term in _split3(lf[r0:r0 + span, :]):
            c = c + _dot(tri, term)
        pieces.append(c)
        carry = c[-1:, :]
    carry_ref[...] = carry
    c2 = jnp.concatenate(pieces, axis=0) * LOG2E
    row = lax.broadcasted_iota(jnp.int32, (LANES, LANES), 0)
    col = lax.broadcasted_iota(jnp.int32, (LANES, LANES), 1)
    packed = None
    for i, term in enumerate(_split3(c2)):
        sel = jnp.where((col == row + TERM_STRIDE * i) & (row < TERM_STRIDE), 1.0, 0.0)
        t = _dot(term, sel.astype(BF16))
        packed = t if packed is None else packed + t
    cx_ref[...] = packed.astype(BF16)
    xq_ref[...] = (_dot(hn, wx_ref[...]) * (X_HEAD_DIM ** -0.5)).astype(BF16)


def _fox_proj(x, g, wqkv, wf, bf, wx, qscale):
    t = x.shape[0]
    tm = _row_tile(t, 512)
    row = lambda n: pl.BlockSpec((tm, n), lambda m: (m, 0))
    full = lambda a: pl.BlockSpec(a.shape, lambda m: (0,) * a.ndim)
    return pl.pallas_call(
        functools.partial(_fox_proj_kernel, qscale=qscale),
        grid=(t // tm,),
        in_specs=[row(D_MODEL), full(g), full(wqkv), full(wf), full(bf), full(wx)],
        out_specs=[row(FOX_WIDTH), row(FOX_WIDTH), row(FOX_WIDTH), row(LANES), row(X_WIDTH)],
        out_shape=[
            jax.ShapeDtypeStruct((t, FOX_WIDTH), BF16),
            jax.ShapeDtypeStruct((t, FOX_WIDTH), F32),
            jax.ShapeDtypeStruct((t, FOX_WIDTH), F32),
            jax.ShapeDtypeStruct((t, LANES), F32),
            jax.ShapeDtypeStruct((t, X_WIDTH), BF16),
        ],
        compiler_params=_params("parallel"),
        name="fox_proj",
    )(x, g, wqkv, wf, bf, wx)


def _fox_proj_t(x, g, wqk, wkvt, wf, bf, wx, qscale, b, s):
    t = x.shape[0]
    tm = _row_tile(s, 512)
    ns = s // tm
    span = _row_tile(tm, 256)
    tri = jnp.asarray(np.tril(np.ones((span, span), np.float32)), dtype=BF16)
    row = lambda n: pl.BlockSpec((tm, n), lambda m: (m, 0))
    full = lambda a: pl.BlockSpec(a.shape, lambda m: (0,) * a.ndim)
    tspec = pl.BlockSpec((1, FOX_WIDTH, tm), lambda m: (m // ns, 0, m % ns))
    return pl.pallas_call(
        functools.partial(_fox_proj_t_kernel, qscale=qscale, tiles_per_seq=ns),
        grid=(t // tm,),
        in_specs=[row(D_MODEL), full(g), full(wqk), full(wkvt), full(wf), full(bf), full(wx),
                  full(tri)],
        out_specs=[row(FOX_WIDTH), row(FOX_WIDTH), tspec, tspec, row(LANES), row(LANES),
                   row(X_WIDTH)],
        out_shape=[
            jax.ShapeDtypeStruct((t, FOX_WIDTH), BF16),
            jax.ShapeDtypeStruct((t, FOX_WIDTH), BF16),
            jax.ShapeDtypeStruct((b, FOX_WIDTH, s), F32),
            jax.ShapeDtypeStruct((b, FOX_WIDTH, s), F32),
            jax.ShapeDtypeStruct((t, LANES), F32),
            jax.ShapeDtypeStruct((t, LANES), BF16),
            jax.ShapeDtypeStruct((t, X_WIDTH), BF16),
        ],
        scratch_shapes=[pltpu.VMEM((1, LANES), F32)],
        compiler_params=_params("arbitrary"),
        name="fox_proj_t",
    )(x, g, wqk, wkvt, wf, bf, wx, tri)


def _bias_select(moves, sign):
    r = lax.broadcasted_iota(jnp.int32, (LANES, LANES), 0)
    c = lax.broadcasted_iota(jnp.int32, (LANES, LANES), 1)
    sel = jnp.zeros((LANES, LANES), F32)
    for head, dst in moves:
        for i in range(BIAS_TERMS):
            sel = jnp.where((r == TERM_STRIDE * i + head) & (c == dst + i), sign, sel)
    return sel.astype(BF16)


def _ones_lanes(shape, lo, n):
    lane = lax.broadcasted_iota(jnp.int32, shape, 1)
    return jnp.where((lane >= lo) & (lane < lo + n), 1.0, 0.0)


def _group8(x, op):
    parts = [x[r * 8:(r + 1) * 8, :] for r in range(x.shape[0] // 8)]
    while len(parts) > 1:
        parts = [op(parts[i], parts[i + 1]) for i in range(0, len(parts), 2)]
    return parts[0]


def _flash_kernel(*refs, tile, n_q, n_k, use_c):
    q_refs = refs[:n_q]
    k_refs = refs[n_q:n_q + n_k]
    vt_ref = refs[n_q + n_k]
    pos = n_q + n_k + 1
    if use_c:
        c_ref = refs[pos]
        pos += 1
    o_ref, kb_ref, vtb_ref, s_ref, p_ref, acc_ref = refs[pos:pos + 6]
    pair = pl.program_id(1)
    seq = vt_ref.shape[2]
    n_tiles = seq // tile
    chunk = 64
    nb = BIAS_TERMS

    for n, k_ref in enumerate(k_refs):
        kb_ref[:, n * LANES:(n + 1) * LANES] = k_ref[...].astype(BF16)
    vtb_ref[...] = vt_ref[0].astype(BF16)
    if use_c:
        feat = (_dot(c_ref[0], _bias_select([(2 * pair, 0), (2 * pair + 1, nb)], -1.0))
                + _ones_lanes((1, LANES), 2 * nb, nb))
        kb_ref[:, n_k * LANES:] = feat.astype(BF16)

    kd_q = n_q * LANES
    lane = lax.broadcasted_iota(jnp.int32, (1, kd_q), 1)
    if n_q == 1:
        own = (lane < FOX_HEAD_DIM, lane >= FOX_HEAD_DIM)
    else:
        own = ((lane < MLA_NOPE) | ((lane >= LANES) & (lane < LANES + MLA_ROPE)),
               ((lane >= MLA_NOPE) & (lane < LANES))
               | ((lane >= LANES + MLA_ROPE) & (lane < LANES + 2 * MLA_ROPE)))

    def q_operands(i):
        rows = slice(i * tile, (i + 1) * tile)
        q = (jnp.concatenate([r[rows, :] for r in q_refs], axis=1) if n_q > 1
             else q_refs[0][rows, :])
        zero = jnp.zeros_like(q)
        qh = [jnp.where(own[h], q, zero) for h in range(2)]
        if use_c:
            cx = c_ref[0, rows, :]
            for h in range(2):
                feat = (_dot(cx, _bias_select([(2 * pair + h, 2 * nb)], 1.0))
                        + _ones_lanes((1, LANES), h * nb, nb))
                qh[h] = jnp.concatenate([qh[h], feat.astype(BF16)], axis=1)
        return qh

    krow = lax.broadcasted_iota(jnp.int32, (chunk, tile), 0)
    qcol = lax.broadcasted_iota(jnp.int32, (chunk, tile), 1)

    def scores(slot, qh, j):
        kblk = kb_ref[j * tile:(j + 1) * tile, :]
        for h in range(2):
            s_ref[slot, h] = _dot_nt(kblk, qh[h])

    def values(slot, par, j, alphas):
        hd = LANES // 2
        for h in range(2):
            pv = _dot(vtb_ref[h * hd:(h + 1) * hd, j * tile:(j + 1) * tile], p_ref[slot, h])
            acc_ref[par, h] = pv if alphas is None else acc_ref[par, h] * alphas[h] + pv

    def softmax(slot, stats, diagonal):
        out = []
        alphas = []
        for h in range(2):
            mx = None
            for c in range(tile // chunk):
                rows = slice(c * chunk, (c + 1) * chunk)
                blk = s_ref[slot, h, rows, :]
                if diagonal:
                    blk = jnp.where(krow + c * chunk > qcol, NEG_INF, blk)
                    s_ref[slot, h, rows, :] = blk
                cm = _group8(blk, jnp.maximum)
                mx = cm if mx is None else jnp.maximum(mx, cm)
            m_new = jnp.max(mx, axis=0, keepdims=True)
            if stats is not None:
                m_old = stats[h][0]
                m_new = jnp.maximum(m_old, m_new)
                alphas.append(jnp.exp2(m_old - m_new))
            ls = None
            for c in range(tile // chunk):
                rows = slice(c * chunk, (c + 1) * chunk)
                p = jnp.exp2(s_ref[slot, h, rows, :] - m_new)
                ps = _group8(p, jnp.add)
                ls = ps if ls is None else ls + ps
                p_ref[slot, h, rows, :] = p.astype(BF16)
            l = jnp.sum(ls, axis=0, keepdims=True)
            if stats is not None:
                l = stats[h][1] * alphas[-1] + l
            out.append((m_new, l))
        return out, (None if stats is None else alphas)

    def finish(i, stats):
        par = i % 2
        o_t = jnp.concatenate([acc_ref[par, h] / stats[h][1] for h in range(2)], axis=0)
        o_ref[i * tile:(i + 1) * tile, :] = o_t.T.astype(o_ref.dtype)

    blocks = [(i, j) for i in range(n_tiles) for j in range(i + 1)]
    qh_of = {0: q_operands(0)}
    scores(0, qh_of[0], 0)
    stats = None
    pending = None
    for n, (i, j) in enumerate(blocks):
        if n + 1 < len(blocks):
            i2, j2 = blocks[n + 1]
            if i2 not in qh_of:
                qh_of[i2] = q_operands(i2)
            scores((n + 1) % 2, qh_of[i2], j2)
        if pending is not None:
            slot_p, i_p, j_p, alphas_p, stats_p = pending
            values(slot_p, i_p % 2, j_p, alphas_p)
            if j_p == i_p:
                finish(i_p, stats_p)
        stats, alphas = softmax(n % 2, None if j == 0 else stats, j == i)
        pending = (n % 2, i, j, alphas, stats)
    slot_p, i_p, j_p, alphas_p, stats_p = pending
    values(slot_p, i_p % 2, j_p, alphas_p)
    finish(i_p, stats_p)


FLASH_TILE = 512


def _flash(qs, ks, vt, b, s, c=None):
    tile = _row_tile(s, FLASH_TILE)
    use_c = c is not None
    kd = LANES * (len(ks) + (1 if use_c else 0))
    rows = pl.BlockSpec((s, LANES), lambda i, p: (i, p))
    shared = pl.BlockSpec((s, LANES), lambda i, p: (i, 0))
    in_specs = [rows] * len(qs) + [rows if k.shape[1] > LANES else shared for k in ks]
    in_specs.append(pl.BlockSpec((1, LANES, s), lambda i, p: (i, p, 0)))
    args = list(qs) + list(ks) + [vt]
    if use_c:
        in_specs.append(pl.BlockSpec((1, s, LANES), lambda i, p: (i, 0, 0)))
        args.append(c)
    kern = functools.partial(_flash_kernel, tile=tile, n_q=len(qs), n_k=len(ks), use_c=use_c)
    return pl.pallas_call(
        kern,
        grid=(b, PAIRS),
        in_specs=in_specs,
        out_specs=rows,
        out_shape=jax.ShapeDtypeStruct((b * s, PAIRS * LANES), BF16),
        scratch_shapes=[pltpu.VMEM((s, kd), BF16), pltpu.VMEM((LANES, s), BF16),
                        pltpu.VMEM((2, 2, tile, tile), F32),
                        pltpu.VMEM((2, 2, tile, tile), BF16),
                        pltpu.VMEM((2, 2, LANES // 2, tile), F32)],
        compiler_params=_params("parallel", "parallel"),
        name="flash_mla" if len(qs) > 1 else "flash_fox",
    )(*args)


def _cross_kernel(q_ref, k_ref, v_ref, o_ref):
    q = q_ref[0].astype(BF16)
    kb = k_ref[0].astype(BF16)
    vb = v_ref[0].astype(BF16)
    lane = lax.broadcasted_iota(jnp.int32, (1, LANES), 1)
    first = lane < X_HEAD_DIM
    zero = jnp.zeros_like(q)
    outs = []
    for qh in (jnp.where(first, q, zero), jnp.where(first, zero, q)):
        s = _dot_nt(qh, kb)
        m = jnp.max(s, axis=1, keepdims=True)
        p = jnp.exp(s - m)
        l = jnp.sum(p, axis=1, keepdims=True)
        outs.append(_dot(p.astype(BF16), vb) / l)
    o_ref[0] = jnp.where(first, outs[0], outs[1]).astype(o_ref.dtype)


def _cross(xq, mk, mv):
    b, t, _ = xq.shape
    m = mk.shape[1]
    tq = _row_tile(t, 2048)
    qspec = pl.BlockSpec((1, tq, LANES), lambda i, p, j: (i, j, p))
    kspec = pl.BlockSpec((1, m, LANES), lambda i, p, j: (i, 0, p))
    return pl.pallas_call(
        _cross_kernel,
        grid=(b, X_WIDTH // LANES, t // tq),
        in_specs=[qspec, kspec, kspec],
        out_specs=qspec,
        out_shape=jax.ShapeDtypeStruct((b, t, X_WIDTH), BF16),
        compiler_params=_params("parallel", "parallel", "arbitrary"),
        name="cross_attn",
    )(xq, mk, mv)


def _cross_t_kernel(q_ref, kt_ref, vt_ref, o_ref):
    lane = lax.broadcasted_iota(jnp.int32, (1, LANES), 1)
    first = lane < X_HEAD_DIM
    for g in range(q_ref.shape[0]):
        q = q_ref[g].astype(BF16)
        halves = []
        for p in range(X_WIDTH // LANES):
            cols = slice(p * LANES, (p + 1) * LANES)
            qp = q[:, cols]
            ktb = kt_ref[g, cols, :].astype(BF16)
            vtb = vt_ref[g, cols, :].astype(BF16)
            zero = jnp.zeros_like(qp)
            outs = []
            for qh in (jnp.where(first, qp, zero), jnp.where(first, zero, qp)):
                s = _dot(qh, ktb)
                mx = jnp.max(s, axis=1, keepdims=True)
                pr = jnp.exp(s - mx)
                l = jnp.sum(pr, axis=1, keepdims=True)
                outs.append(_dot_nt(pr.astype(BF16), vtb) / l)
            halves.append(jnp.where(first, outs[0], outs[1]))
        o_ref[g] = jnp.concatenate(halves, axis=1).astype(o_ref.dtype)


CROSS_T_BATCHES = 8


def _cross_t(xq, mkt, mvt, layer):
    b, t, _ = xq.shape
    m = mkt.shape[3]
    g = CROSS_T_BATCHES if b % CROSS_T_BATCHES == 0 else 1
    qspec = pl.BlockSpec((g, t, X_WIDTH), lambda i: (i, 0, 0))
    kspec = pl.BlockSpec((None, g, X_WIDTH, m), lambda i: (layer, i, 0, 0))
    return pl.pallas_call(
        _cross_t_kernel,
        grid=(b // g,),
        in_specs=[qspec, kspec, kspec],
        out_specs=qspec,
        out_shape=jax.ShapeDtypeStruct((b, t, X_WIDTH), BF16),
        compiler_params=_params("parallel"),
        name="cross_attn_t",
    )(xq, mkt, mvt)


def _norm_proj_kernel(x_ref, g_ref, w_ref, o_ref):
    o_ref[...] = _dot(_rms(x_ref[...], g_ref[...]).astype(BF16), w_ref[...])


def _norm_proj(x, g, w):
    t = x.shape[0]
    n = w.shape[1]
    tm = _row_tile(t, 512)
    return pl.pallas_call(
        _norm_proj_kernel,
        grid=(t // tm,),
        in_specs=[pl.BlockSpec((tm, D_MODEL), lambda m: (m, 0)),
                  pl.BlockSpec((1, D_MODEL), lambda m: (0, 0)),
                  pl.BlockSpec(w.shape, lambda m: (0, 0))],
        out_specs=pl.BlockSpec((tm, n), lambda m: (m, 0)),
        out_shape=jax.ShapeDtypeStruct((t, n), F32),
        compiler_params=_params("parallel"),
        name="norm_proj",
    )(x, g, w)


def _mla_proj_kernel(*refs, qscale, with_kv):
    (x_ref, g_ref, wqa_ref, wkva_ref, wkr_ref, wkrr_ref, wx_ref,
     qg_ref, wqn_ref, wqr_ref, wqrr_ref, kvg_ref) = refs[:12]
    pos = 12
    if with_kv:
        wuk_ref, wuvt_ref = refs[pos:pos + 2]
        pos += 2
    cos_ref, sin_ref = refs[pos:pos + 2]
    qn_ref, qr_ref, c_ref, kr_ref, xq_ref = refs[pos + 2:pos + 7]
    hn = _rms(x_ref[...], g_ref[...]).astype(BF16)
    cos = cos_ref[...]
    sin = sin_ref[...]
    qa = _rms(_dot(hn, wqa_ref[...]), qg_ref[...]).astype(BF16)
    qn_ref[...] = (_dot(qa, wqn_ref[...]) * qscale).astype(BF16)
    qr = _dot(qa, wqr_ref[...])
    qrr = _dot(qa, wqrr_ref[...])
    for p in range(PAIRS):
        sl = slice(p * LANES, (p + 1) * LANES)
        qr_ref[:, sl] = ((qr[:, sl] * cos + qrr[:, sl] * sin) * qscale).astype(BF16)
    c = _rms(_dot(hn, wkva_ref[...]), kvg_ref[...])
    c_ref[...] = c
    if with_kv:
        kn_ref, vt_ref = refs[pos + 7:pos + 9]
        cb = c.astype(BF16)
        kn_ref[...] = _dot(cb, wuk_ref[...]).astype(BF16)
        vt_ref[0] = _dot_nt(wuvt_ref[...], cb).astype(BF16)
    kr = _dot(hn, wkr_ref[...]) * cos + _dot(hn, wkrr_ref[...]) * sin
    kr_ref[...] = kr[:, :MLA_ROPE]
    xq_ref[...] = (_dot(hn, wx_ref[...]) * (X_HEAD_DIM ** -0.5)).astype(BF16)


def _mla_proj(x, g, w, cos, sin, qscale, kv_w=None, b=None, s=None):
    t = x.shape[0]
    with_kv = kv_w is not None
    tm = _row_tile(s if with_kv else t, 512)
    ntab = cos.shape[0] // tm
    row = lambda n: pl.BlockSpec((tm, n), lambda m: (m, 0))
    full = lambda a: pl.BlockSpec(a.shape, lambda m: (0,) * a.ndim)
    tab = pl.BlockSpec((tm, LANES), lambda m: (m % ntab, 0))
    wide = PAIRS * LANES
    w = list(w) + (list(kv_w) if with_kv else [])
    out_specs = [row(wide), row(wide), row(MLA_KV_LORA), row(MLA_ROPE), row(X_WIDTH)]
    out_shape = [
        jax.ShapeDtypeStruct((t, wide), BF16),
        jax.ShapeDtypeStruct((t, wide), BF16),
        jax.ShapeDtypeStruct((t, MLA_KV_LORA), F32),
        jax.ShapeDtypeStruct((t, MLA_ROPE), F32),
        jax.ShapeDtypeStruct((t, X_WIDTH), BF16),
    ]
    if with_kv:
        ns = s // tm
        out_specs += [row(wide), pl.BlockSpec((1, wide, tm), lambda m: (m // ns, 0, m % ns))]
        out_shape += [jax.ShapeDtypeStruct((t, wide), BF16),
                      jax.ShapeDtypeStruct((b, wide, s), BF16)]
    return pl.pallas_call(
        functools.partial(_mla_proj_kernel, qscale=qscale, with_kv=with_kv),
        grid=(t // tm,),
        in_specs=[row(D_MODEL), full(g)] + [full(a) for a in w] + [tab, tab],
        out_specs=out_specs,
        out_shape=out_shape,
        compiler_params=_params("parallel"),
        name="mla_proj_kv" if with_kv else "mla_proj",
    )(x, g, *w, cos, sin)


FOX_PAGES_PER_STEP = 16
MLA_PAGES_PER_STEP = 128
PAGE_GROUPS = 4


def _pages_per_step(n_pages, pref):
    while n_pages % pref:
        pref //= 2
    return pref


def _online_step(state, s_parts, v_parts, v_is_transposed):
    m, l, acc = state
    n = len(s_parts)
    per = n // PAGE_GROUPS if n % PAGE_GROUPS == 0 else n
    partials = []
    for g0 in range(0, n, per):
        s = jnp.concatenate(s_parts[g0:g0 + per], axis=1) if per > 1 else s_parts[g0]
        mg = jnp.max(s, axis=1, keepdims=True)
        p = jnp.exp(s - mg)
        lg = jnp.sum(p, axis=1, keepdims=True)
        ag = None
        for r in range(per):
            pb = p[:, r * LANES:(r + 1) * LANES].astype(BF16)
            vb = v_parts[g0 + r]
            t = _dot_nt(pb, vb) if v_is_transposed else _dot(pb, vb)
            ag = t if ag is None else ag + t
        partials.append((mg, lg, ag))
    m_new = m
    for mg, _, _ in partials:
        m_new = jnp.maximum(m_new, mg)
    alpha = jnp.exp(m - m_new)
    l = l * alpha
    acc = acc * alpha
    for mg, lg, ag in partials:
        w = jnp.exp(mg - m_new)
        l = l + lg * w
        acc = acc + ag * w
    return m_new, l, acc


def _head_diag(acc, t_new, width):
    n = acc.shape[1]
    rowh = lax.broadcasted_iota(jnp.int32, (HEAD_PAD, n), 0)
    laneh = lax.broadcasted_iota(jnp.int32, (HEAD_PAD, n), 1) // width
    orow = lax.broadcasted_iota(jnp.int32, (t_new, n), 0)
    out = jnp.zeros((t_new, n), F32)
    for i in range(t_new):
        blk = acc[i * HEAD_PAD:(i + 1) * HEAD_PAD, :]
        r = jnp.sum(jnp.where(rowh == laneh, blk, 0.0), axis=0, keepdims=True)
        out = jnp.where(orow == i, r, out)
    return out


def _fox_dec_kernel(pt_ref, qb_ref, kn_ref, vn_ref, lft_ref, sufm_ref, *rest, pb, t_new):
    kt_refs = rest[:pb]
    vt_refs = rest[pb:2 * pb]
    lf_refs = rest[2 * pb:3 * pb]
    o_ref, m_ref, l_ref, acc_ref, run_ref, pad_ref, lfp_ref = rest[3 * pb:]
    step = pl.program_id(1)
    qb = qb_ref[0]
    lft = lft_ref[0]
    lane = lax.broadcasted_iota(jnp.int32, (HEAD_PAD, LANES), 1)
    cn_cols = []
    run = jnp.zeros((HEAD_PAD, 1), F32)
    for i in range(t_new):
        run = run + lft[:, i:i + 1]
        cn_cols.append(run)
    cn_col = jnp.concatenate(cn_cols, axis=0)

    @pl.when(step == 0)
    def _():
        cn_lanes = jnp.zeros((HEAD_PAD, LANES), F32)
        for i in range(t_new):
            cn_lanes = jnp.where(lane == i, cn_cols[i], cn_lanes)
        pad_ref[...] = jnp.zeros_like(pad_ref)
        pad_ref[0:t_new, :] = kn_ref[0]
        kpad = pad_ref[...].astype(BF16)
        pad_ref[0:t_new, :] = vn_ref[0]
        vpad = pad_ref[...].astype(BF16)
        s = _dot_nt(qb, kpad)
        parts = []
        for i in range(t_new):
            blk = s[i * HEAD_PAD:(i + 1) * HEAD_PAD, :] + (cn_cols[i] - cn_lanes)
            parts.append(jnp.where(lane <= i, blk, NEG_INF))
        s = jnp.concatenate(parts, axis=0)
        m = jnp.max(s, axis=1, keepdims=True)
        p = jnp.exp(s - m)
        m_ref[...] = m
        l_ref[...] = jnp.sum(p, axis=1, keepdims=True)
        acc_ref[...] = _dot(p.astype(BF16), vpad)
        run_ref[...] = jnp.zeros_like(run_ref)
        lfp_ref[...] = jnp.zeros_like(lfp_ref)

    run = run_ref[...]
    s_parts, v_parts = [], []
    for r in range(pb):
        lfp_ref[r * HEAD_PAD:r * HEAD_PAD + FOX_HEADS, :] = lf_refs[r][0]
    lf_all = lfp_ref[...]
    suf_all = _dot_exact(lf_all, sufm_ref[...])
    tot_all = jnp.sum(lf_all, axis=1, keepdims=True)
    for r in range(pb):
        heads = slice(r * HEAD_PAD, (r + 1) * HEAD_PAD)
        suf = suf_all[heads, :] + run
        run = run + tot_all[heads, :]
        ktb = kt_refs[r][0, 0].reshape(FOX_WIDTH, PAGE_SIZE).astype(BF16)
        s = _dot(qb, ktb)
        bias = jnp.concatenate([suf] * t_new, axis=0) + cn_col
        s_parts.append(s + bias)
        v_parts.append(vt_refs[r][0, 0].reshape(FOX_WIDTH, PAGE_SIZE).astype(BF16))
    run_ref[...] = run
    m, l, acc = _online_step((m_ref[...], l_ref[...], acc_ref[...]), s_parts, v_parts, True)
    m_ref[...] = m
    l_ref[...] = l
    acc_ref[...] = acc

    @pl.when(step == pl.num_programs(1) - 1)
    def _():
        o_ref[0] = _head_diag(acc / l, t_new, FOX_HEAD_DIM)


def _fox_decode(page_table, qblk, k_new, v_new, lft_new, cache_kt, cache_vt, cache_lf):
    bd, n_pages = page_table.shape
    t_new = k_new.shape[1]
    rows = t_new * HEAD_PAD
    pb = _pages_per_step(n_pages, FOX_PAGES_PER_STEP)
    sufm = jnp.asarray(np.tril(np.ones((PAGE_SIZE, PAGE_SIZE), np.float32), -1))

    def page(r):
        return lambda b, s, pt: pt[b, n_pages - 1 - (s * pb + r)]

    kv_specs = [pl.BlockSpec((1, 1, FOX_HEADS, FOX_HEAD_DIM, PAGE_SIZE),
                             (lambda b, s, pt, f=page(r): (0, f(b, s, pt), 0, 0, 0)))
                for r in range(pb)]
    lf_specs = [pl.BlockSpec((1, FOX_HEADS, PAGE_SIZE),
                             (lambda b, s, pt, f=page(r): (f(b, s, pt), 0, 0)))
                for r in range(pb)]
    per_b = lambda shape: pl.BlockSpec((1,) + shape, lambda b, s, pt: (b, 0, 0))
    grid_spec = pltpu.PrefetchScalarGridSpec(
        num_scalar_prefetch=1,
        grid=(bd, n_pages // pb),
        in_specs=[per_b((rows, FOX_WIDTH)), per_b((t_new, FOX_WIDTH)), per_b((t_new, FOX_WIDTH)),
                  per_b((HEAD_PAD, LANES)),
                  pl.BlockSpec((PAGE_SIZE, PAGE_SIZE), lambda b, s, pt: (0, 0))]
        + kv_specs + kv_specs + lf_specs,
        out_specs=per_b((t_new, FOX_WIDTH)),
        scratch_shapes=[pltpu.VMEM((rows, 1), F32), pltpu.VMEM((rows, 1), F32),
                        pltpu.VMEM((rows, FOX_WIDTH), F32), pltpu.VMEM((HEAD_PAD, 1), F32),
                        pltpu.VMEM((PAGE_SIZE, FOX_WIDTH), F32),
                        pltpu.VMEM((pb * HEAD_PAD, PAGE_SIZE), F32)],
    )
    return pl.pallas_call(
        functools.partial(_fox_dec_kernel, pb=pb, t_new=t_new),
        grid_spec=grid_spec,
        out_shape=jax.ShapeDtypeStruct((bd, t_new, FOX_WIDTH), F32),
        compiler_params=_params("parallel", "arbitrary"),
        name="fox_decode",
    )(page_table, qblk, k_new, v_new, lft_new, sufm,
      *([cache_kt] * pb), *([cache_vt] * pb), *([cache_lf] * pb))


def _mla_dec_kernel(pt_ref, qn_ref, qr_ref, cn_ref, krn_ref, wuk_ref, wuv_ref, *rest, pb, t_new):
    c_refs = rest[:pb]
    kr_refs = rest[pb:2 * pb]
    o_ref, m_ref, l_ref, acc_ref, ql_ref, cpad_ref = rest[2 * pb:]
    step = pl.program_id(1)
    lane = lax.broadcasted_iota(jnp.int32, (HEAD_PAD, LANES), 1)
    qr = qr_ref[0]

    @pl.when(step == 0)
    def _():
        ql = _dot(qn_ref[0], wuk_ref[...]).astype(BF16)
        ql_ref[...] = ql
        cpad_ref[...] = jnp.zeros_like(cpad_ref)
        cpad_ref[0:t_new, :] = cn_ref[0]
        cpad = cpad_ref[...].astype(BF16)
        s = _dot_nt(ql, cpad) + _dot_nt(qr, krn_ref[0])
        parts = [jnp.where(lane <= i, s[i * HEAD_PAD:(i + 1) * HEAD_PAD, :], NEG_INF)
                 for i in range(t_new)]
        s = jnp.concatenate(parts, axis=0)
        m = jnp.max(s, axis=1, keepdims=True)
        p = jnp.exp(s - m)
        m_ref[...] = m
        l_ref[...] = jnp.sum(p, axis=1, keepdims=True)
        acc_ref[...] = _dot(p.astype(BF16), cpad)

    ql = ql_ref[...]
    s_parts, v_parts = [], []
    kr_pad = jnp.zeros((LANES - MLA_ROPE, PAGE_SIZE), BF16)
    for r in range(pb):
        cb = c_refs[r][0, 0].astype(BF16)
        krt = jnp.concatenate([kr_refs[r][0, 0].astype(BF16), kr_pad], axis=0)
        s_parts.append(_dot_nt(ql, cb) + _dot(qr, krt))
        v_parts.append(cb)
    m, l, acc = _online_step((m_ref[...], l_ref[...], acc_ref[...]), s_parts, v_parts, False)
    m_ref[...] = m
    l_ref[...] = l
    acc_ref[...] = acc

    @pl.when(step == pl.num_programs(1) - 1)
    def _():
        full = _dot((acc / l).astype(BF16), wuv_ref[...])
        o_ref[0] = _head_diag(full, t_new, MLA_V)


def _mla_decode(page_table, qnblk, qrblk, c_new, kr_new, wuk_t, wuv, cache_c, cache_krt):
    bd, n_pages = page_table.shape
    t_new = c_new.shape[1]
    rows = t_new * HEAD_PAD
    pb = _pages_per_step(n_pages, MLA_PAGES_PER_STEP)

    def page(r):
        return lambda b, s, pt: pt[b, s * pb + r]

    c_specs = [pl.BlockSpec((1, 1, PAGE_SIZE, MLA_KV_LORA),
                            (lambda b, s, pt, f=page(r): (0, f(b, s, pt), 0, 0)))
               for r in range(pb)]
    kr_specs = [pl.BlockSpec((1, 1, MLA_ROPE, PAGE_SIZE),
                             (lambda b, s, pt, f=page(r): (0, f(b, s, pt), 0, 0)))
                for r in range(pb)]
    per_b = lambda shape: pl.BlockSpec((1,) + shape, lambda b, s, pt: (b, 0, 0))
    full = lambda a: pl.BlockSpec(a.shape, lambda b, s, pt: (0,) * a.ndim)
    wide = MLA_HEADS * MLA_V
    grid_spec = pltpu.PrefetchScalarGridSpec(
        num_scalar_prefetch=1,
        grid=(bd, n_pages // pb),
        in_specs=[per_b((rows, MLA_HEADS * MLA_NOPE)), per_b((rows, LANES)),
                  per_b((t_new, MLA_KV_LORA)), per_b((PAGE_SIZE, LANES)),
                  full(wuk_t), full(wuv)] + c_specs + kr_specs,
        out_specs=per_b((t_new, wide)),
        scratch_shapes=[pltpu.VMEM((rows, 1), F32), pltpu.VMEM((rows, 1), F32),
                        pltpu.VMEM((rows, MLA_KV_LORA), F32),
                        pltpu.VMEM((rows, MLA_KV_LORA), BF16),
                        pltpu.VMEM((PAGE_SIZE, MLA_KV_LORA), F32)],
    )
    return pl.pallas_call(
        functools.partial(_mla_dec_kernel, pb=pb, t_new=t_new),
        grid_spec=grid_spec,
        out_shape=jax.ShapeDtypeStruct((bd, t_new, wide), F32),
        compiler_params=_params("parallel", "arbitrary"),
        name="mla_decode",
    )(page_table, qnblk, qrblk, c_new, kr_new, wuk_t, wuv,
      *([cache_c] * pb), *([cache_krt] * pb))


def _rope_tables(pos):
    half = MLA_ROPE // 2
    inv = ROPE_THETA ** (-jnp.arange(half, dtype=F32) / half)
    ang = pos.astype(F32)[:, None] * inv
    reps = (LANES // 2) // half
    pad = jnp.zeros((pos.shape[0], LANES // 2), F32)
    cos = jnp.concatenate([jnp.tile(jnp.cos(ang), (1, reps)), pad], axis=1)
    sin = jnp.concatenate([jnp.tile(jnp.sin(ang), (1, reps)), pad], axis=1)
    return cos, sin


def _rot_cols(w):
    half = MLA_ROPE // 2
    return jnp.concatenate([-w[..., half:], w[..., :half]], axis=-1)


def _pair_rope_cols(w):
    r = w.shape[0]
    w = w.reshape(r, PAIRS, 2 * MLA_ROPE)
    return jnp.pad(w, ((0, 0), (0, 0), (0, LANES - 2 * MLA_ROPE))).reshape(r, PAIRS * LANES)


def _block_diag_rows(q, width):
    bd, t, n = q.shape
    rowh = np.arange(t * HEAD_PAD) % HEAD_PAD
    mask = jnp.asarray(rowh[:, None] == (np.arange(n) // width)[None, :])
    rep = jnp.repeat(q, HEAD_PAD, axis=1)
    return jnp.where(mask[None], rep, jnp.zeros_like(rep))


def kernel(x_prompt, x_sample, cache_fox_k, cache_fox_v, cache_fox_logf, cache_mla_ckv,
           cache_mla_krope, cache_mem_k, cache_mem_v, page_table, mem_prompt, norm_gains,
           ffn_w_up, ffn_w_down, fox_w_in, fox_b_f, mla_w_in, mla_q_norm, mla_w_q_b,
           mla_kv_norm, mla_w_kv_b, mem_norm, w_mem_kv, w_out, final_norm):
    b, s, d = x_prompt.shape
    bd, t_new, _ = x_sample.shape
    n_pages = page_table.shape[1]
    mem_len = mem_prompt.shape[1]
    depth = norm_gains.shape[0]
    xp = x_prompt.reshape(b * s, d)
    xs = x_sample.reshape(bd * t_new, d)

    w_up = ffn_w_up.astype(BF16)
    w_down = ffn_w_down.astype(BF16)
    w_out_b = w_out.astype(BF16)
    w_mem_b = w_mem_kv.astype(BF16)
    gains = norm_gains.reshape(depth, 3, 1, d)
    mem2d = mem_prompt.reshape(b * mem_len, d)
    mkt_s = jnp.transpose(cache_mem_k, (0, 1, 3, 4, 2)).reshape(depth, bd, X_WIDTH, mem_len)
    mvt_s = jnp.transpose(cache_mem_v, (0, 1, 3, 4, 2)).reshape(depth, bd, X_WIDTH, mem_len)

    outs = {}
    for layer in range(depth):
        j = layer // 2
        xp = _ffn_half(xp, gains[layer, 0], w_up, w_down, layer, 0)
        xs = _ffn_half(xs, gains[layer, 0], w_up, w_down, layer, 0)

        mkv = _norm_proj(mem2d, mem_norm[layer].reshape(1, d), w_mem_b[layer])
        mk = mkv[:, :X_WIDTH].reshape(b, mem_len, X_WIDTH)
        mv = mkv[:, X_WIDTH:].reshape(b, mem_len, X_WIDTH)
        outs.setdefault("mk", []).append(mk.reshape(b, mem_len, X_HEADS, X_HEAD_DIM))
        outs.setdefault("mv", []).append(mv.reshape(b, mem_len, X_HEADS, X_HEAD_DIM))

        if layer % 2 == 0:
            w_in = fox_w_in[j]
            wqkv = w_in[:, :3 * FOX_WIDTH].astype(BF16)
            wf = jnp.pad(w_in[:, 3 * FOX_WIDTH:3 * FOX_WIDTH + FOX_HEADS],
                         ((0, 0), (0, LANES - FOX_HEADS))).astype(BF16)
            bf = jnp.pad(fox_b_f[j], (0, LANES - FOX_HEADS)).reshape(1, LANES)
            wx = w_in[:, 3 * FOX_WIDTH + FOX_HEADS:].astype(BF16)
            g1 = gains[layer, 1]

            wqk = w_in[:, :2 * FOX_WIDTH].astype(BF16)
            wkvt = jnp.transpose(w_in[:, FOX_WIDTH:3 * FOX_WIDTH]).astype(BF16)
            q, kb, kt, vt, lf, cx, xq_p = _fox_proj_t(
                xp, g1, wqk, wkvt, wf, bf, wx, FOX_HEAD_DIM ** -0.5 * LOG2E, b, s)
            mix_p = _flash([q], [kb], vt, b, s, cx.reshape(b, s, LANES))
            heads_t = (0, 1, 4, 2, 3)
            outs["fk_p"] = jnp.transpose(kt.reshape(1, b, FOX_HEADS, FOX_HEAD_DIM, s), heads_t)
            outs["fv_p"] = jnp.transpose(vt.reshape(1, b, FOX_HEADS, FOX_HEAD_DIM, s), heads_t)
            outs["fl_p"] = lf[:, :FOX_HEADS].reshape(1, b, s, FOX_HEADS)

            q, k, v, lf, xq_s = _fox_proj(xs, g1, wqkv, wf, bf, wx, FOX_HEAD_DIM ** -0.5)
            outs["fk_s"] = k.reshape(1, bd, t_new, FOX_HEADS, FOX_HEAD_DIM)
            outs["fv_s"] = v.reshape(1, bd, t_new, FOX_HEADS, FOX_HEAD_DIM)
            outs["fl_s"] = lf[:, :FOX_HEADS].reshape(1, bd, t_new, FOX_HEADS)
            qblk = _block_diag_rows(q.reshape(bd, t_new, FOX_WIDTH), FOX_HEAD_DIM)
            lft = jnp.transpose(lf.reshape(bd, t_new, LANES)[:, :, :HEAD_PAD], (0, 2, 1))
            lft = jnp.pad(lft, ((0, 0), (0, 0), (0, LANES - t_new)))
            cache_kt = jnp.transpose(cache_fox_k, (0, 1, 3, 4, 2))
            cache_vt = jnp.transpose(cache_fox_v, (0, 1, 3, 4, 2))
            cache_lf = jnp.transpose(cache_fox_logf[j], (0, 2, 1))
            mix_s = _fox_decode(page_table, qblk, k.reshape(bd, t_new, FOX_WIDTH),
                                v.reshape(bd, t_new, FOX_WIDTH), lft,
                                cache_kt[j:j + 1], cache_vt[j:j + 1], cache_lf)
            mix_s = mix_s.reshape(bd * t_new, FOX_WIDTH).astype(BF16)
        else:
            w_in = mla_w_in[j]
            o1 = MLA_Q_LORA
            o2 = o1 + MLA_KV_LORA
            o3 = o2 + MLA_ROPE
            wkr = w_in[:, o2:o3]
            lane_pad = ((0, 0), (0, LANES - MLA_ROPE))
            wqb = mla_w_q_b[j]
            wkvb = mla_w_kv_b[j]
            weights = [
                w_in[:, :o1].astype(BF16),
                w_in[:, o1:o2].astype(BF16),
                jnp.pad(wkr, lane_pad).astype(BF16),
                jnp.pad(_rot_cols(wkr), lane_pad).astype(BF16),
                w_in[:, o3:].astype(BF16),
                mla_q_norm[j].reshape(1, MLA_Q_LORA),
                wqb[:, :, :MLA_NOPE].reshape(MLA_Q_LORA, MLA_HEADS * MLA_NOPE).astype(BF16),
                _pair_rope_cols(wqb[:, :, MLA_NOPE:]).astype(BF16),
                _pair_rope_cols(_rot_cols(wqb[:, :, MLA_NOPE:])).astype(BF16),
                mla_kv_norm[j].reshape(1, MLA_KV_LORA),
            ]
            wuk = wkvb[:, :, :MLA_NOPE].reshape(MLA_KV_LORA, MLA_HEADS * MLA_NOPE).astype(BF16)
            wuv = wkvb[:, :, MLA_NOPE:].reshape(MLA_KV_LORA, MLA_HEADS * MLA_V).astype(BF16)
            wuv_t = jnp.transpose(wkvb[:, :, MLA_NOPE:], (1, 2, 0)).reshape(
                MLA_HEADS * MLA_V, MLA_KV_LORA).astype(BF16)
            mla_scale = (MLA_NOPE + MLA_ROPE) ** -0.5
            g1 = gains[layer, 1]
            cos_p, sin_p = _rope_tables(jnp.arange(s, dtype=jnp.int32))
            pos_s = n_pages * PAGE_SIZE + jnp.arange(t_new, dtype=jnp.int32)
            cos_s, sin_s = _rope_tables(jnp.tile(pos_s, bd))

            qn, qr, c, kr, xq_p, kn, vt = _mla_proj(
                xp, g1, weights, cos_p, sin_p, mla_scale * LOG2E, (wuk, wuv_t), b, s)
            kr2 = jnp.concatenate(
                [kr, kr, jnp.zeros((b * s, LANES - 2 * MLA_ROPE), F32)], axis=1).astype(BF16)
            mix_p = _flash([qn, qr], [kn, kr2], vt, b, s)
            outs["mc_p"] = c.reshape(1, b, s, MLA_KV_LORA)
            outs["mr_p"] = kr.reshape(1, b, s, MLA_ROPE)

            qn, qr, c, kr, xq_s = _mla_proj(xs, g1, weights, cos_s, sin_s, mla_scale)
            outs["mc_s"] = c.reshape(1, bd, t_new, MLA_KV_LORA)
            outs["mr_s"] = kr.reshape(1, bd, t_new, MLA_ROPE)
            qnblk = _block_diag_rows(qn.reshape(bd, t_new, MLA_HEADS * MLA_NOPE), MLA_NOPE)
            qr4 = qr.reshape(bd, t_new, PAIRS, LANES)[..., :2 * MLA_ROPE]
            qr4 = qr4.reshape(bd, t_new, MLA_HEADS, MLA_ROPE)
            qr4 = jnp.pad(qr4, ((0, 0), (0, 0), (0, HEAD_PAD - MLA_HEADS), (0, LANES - MLA_ROPE)))
            qrblk = qr4.reshape(bd, t_new * HEAD_PAD, LANES)
            kr_new = jnp.pad(kr.reshape(bd, t_new, MLA_ROPE).astype(BF16),
                             ((0, 0), (0, PAGE_SIZE - t_new), (0, LANES - MLA_ROPE)))
            wuk_t = jnp.transpose(wkvb[:, :, :MLA_NOPE], (1, 2, 0)).reshape(
                MLA_HEADS * MLA_NOPE, MLA_KV_LORA).astype(BF16)
            cache_krt = jnp.transpose(cache_mla_krope, (0, 1, 3, 2))
            mix_s = _mla_decode(page_table, qnblk, qrblk, c.reshape(bd, t_new, MLA_KV_LORA),
                                kr_new, wuk_t, wuv, cache_mla_ckv[j:j + 1],
                                cache_krt[j:j + 1])
            mix_s = mix_s.reshape(bd * t_new, MLA_HEADS * MLA_V).astype(BF16)

        cross_p = _cross(xq_p.reshape(b, s, X_WIDTH), mk, mv).reshape(b * s, X_WIDTH)
        cross_s = _cross_t(xq_s.reshape(bd, t_new, X_WIDTH), mkt_s, mvt_s, layer).reshape(
            bd * t_new, X_WIDTH)
        fin = final_norm.reshape(1, d) if layer == depth - 1 else None
        xp = _ffn_half(xp, gains[layer, 2], w_up, w_down, layer, 1,
                       (mix_p, cross_p, w_out_b[layer]), fin)
        xs = _ffn_half(xs, gains[layer, 2], w_up, w_down, layer, 1,
                       (mix_s, cross_s, w_out_b[layer]), fin)

    y_prompt = xp.reshape(b, s, d)
    y_sample = xs.reshape(bd, t_new, d)
    return (y_prompt, y_sample,
            outs["fk_p"], outs["fv_p"], outs["fl_p"],
            outs["fk_s"], outs["fv_s"], outs["fl_s"],
            outs["mc_p"], outs["mr_p"], outs["mc_s"], outs["mr_s"],
            jnp.stack(outs["mk"]), jnp.stack(outs["mv"]))
```

```python
import functools

import numpy as np
import jax
import jax.numpy as jnp
from jax import lax
from jax.experimental import pallas as pl
from jax.experimental.pallas import tpu as pltpu

D_MODEL = 1024
PAGE_SIZE = 128
X_HEADS = 4
X_HEAD_DIM = 64
X_WIDTH = X_HEADS * X_HEAD_DIM
FOX_HEADS = 12
FOX_HEAD_DIM = 64
FOX_WIDTH = FOX_HEADS * FOX_HEAD_DIM
MLA_HEADS = 12
MLA_NOPE = 64
MLA_ROPE = 32
MLA_V = 64
MLA_Q_LORA = 384
MLA_KV_LORA = 256
FFN_HIDDEN = 2816
ROPE_THETA = 10000.0
RMS_EPS = 1e-6

LANES = 128
HEAD_PAD = 16
PAIRS = FOX_HEADS // 2
VMEM_LIMIT = 56 * 1024 * 1024

F32 = jnp.float32
BF16 = jnp.bfloat16
NT_DIMS = (((1,), (1,)), ((), ()))
NEG_INF = float("-inf")
LOG2E = 1.4426950408889634


def _params(*sem):
    return pltpu.CompilerParams(dimension_semantics=sem, vmem_limit_bytes=VMEM_LIMIT)


def _rms(x, g):
    ms = jnp.mean(x * x, axis=-1, keepdims=True)
    return x * lax.rsqrt(ms + RMS_EPS) * g


def _dot(a, b):
    return jnp.dot(a, b, preferred_element_type=F32)


def _dot_nt(a, b):
    return lax.dot_general(a, b, NT_DIMS, preferred_element_type=F32)


def _dot_exact(a, b):
    return jnp.dot(a, b, preferred_element_type=F32, precision=lax.Precision.HIGHEST)


def _row_tile(t, pref):
    return pref if t % pref == 0 else t


FFN_CHUNK = FFN_HIDDEN


def _ffn_kernel(*refs, merge, final):
    x_ref = refs[0]
    pos = 1
    if merge:
        mix_ref, cross_ref, wo_ref = refs[1:4]
        pos = 4
    g_ref, wg_ref, wu_ref, wd_ref = refs[pos:pos + 4]
    pos += 4
    if final:
        gf_ref = refs[pos]
        pos += 1
    o_ref = refs[pos]
    x = x_ref[...]
    if merge:
        nm = mix_ref.shape[1]
        x = x + _dot(mix_ref[...], wo_ref[:nm, :]) + _dot(cross_ref[...], wo_ref[nm:, :])
    xn = _rms(x, g_ref[...]).astype(BF16)
    acc = None
    for c0 in range(0, FFN_HIDDEN, FFN_CHUNK):
        cols = slice(c0, c0 + FFN_CHUNK)
        gate = _dot(xn, wg_ref[:, cols])
        up = _dot(xn, wu_ref[:, cols])
        h = (gate / (1.0 + jnp.exp(-gate)) * up).astype(BF16)
        part = _dot(h, wd_ref[cols, :])
        acc = part if acc is None else acc + part
    y = x + 0.5 * acc
    o_ref[...] = _rms(y, gf_ref[...]) if final else y


def _ffn_half(x, g, w_up, w_down, layer, idx, merge=None, final_g=None):
    t = x.shape[0]
    tm = _row_tile(t, 512)
    row = lambda n: pl.BlockSpec((tm, n), lambda m: (m, 0))
    once = pl.Buffered(1)
    full = lambda a: pl.BlockSpec(a.shape, lambda m: (0,) * a.ndim, pipeline_mode=once)
    in_specs = [row(D_MODEL)]
    args = [x]
    if merge is not None:
        mix, cross, w_out = merge
        in_specs += [row(mix.shape[1]), row(cross.shape[1]), full(w_out)]
        args += [mix, cross, w_out]
    in_specs += [
        full(g),
        pl.BlockSpec((None, None, D_MODEL, FFN_HIDDEN), lambda m: (layer, idx, 0, 0),
                     pipeline_mode=once),
        pl.BlockSpec((None, None, D_MODEL, FFN_HIDDEN), lambda m: (layer, idx, 0, 1),
                     pipeline_mode=once),
        pl.BlockSpec((None, None, FFN_HIDDEN, D_MODEL), lambda m: (layer, idx, 0, 0),
                     pipeline_mode=once),
    ]
    args += [g, w_up, w_up, w_down]
    if final_g is not None:
        in_specs.append(full(final_g))
        args.append(final_g)
    return pl.pallas_call(
        functools.partial(_ffn_kernel, merge=merge is not None, final=final_g is not None),
        grid=(t // tm,),
        in_specs=in_specs,
        out_specs=row(D_MODEL),
        out_shape=jax.ShapeDtypeStruct((t, D_MODEL), F32),
        compiler_params=_params("parallel"),
        name="ffn_half",
    )(*args)


def _log_sigmoid(f):
    return jnp.minimum(f, 0.0) - jnp.log(1.0 + jnp.exp(-jnp.abs(f)))


def _fox_proj_kernel(x_ref, g_ref, wqkv_ref, wf_ref, bf_ref, wx_ref,
                     q_ref, k_ref, v_ref, lf_ref, xq_ref, *, qscale):
    hn = _rms(x_ref[...], g_ref[...]).astype(BF16)
    qkv = _dot(hn, wqkv_ref[...])
    q_ref[...] = (qkv[:, :FOX_WIDTH] * qscale).astype(BF16)
    k_ref[...] = qkv[:, FOX_WIDTH:2 * FOX_WIDTH]
    v_ref[...] = qkv[:, 2 * FOX_WIDTH:]
    lf_ref[...] = _log_sigmoid(_dot(hn, wf_ref[...]) + bf_ref[...])
    xq_ref[...] = (_dot(hn, wx_ref[...]) * (X_HEAD_DIM ** -0.5)).astype(BF16)


BIAS_TERMS = 3
TERM_STRIDE = 16


def _split3(x):
    hi = x.astype(BF16)
    r1 = x - hi.astype(F32)
    mid = r1.astype(BF16)
    lo = (r1 - mid.astype(F32)).astype(BF16)
    return hi, mid, lo


def _fox_proj_t_kernel(x_ref, g_ref, wqk_ref, wkvt_ref, wf_ref, bf_ref, wx_ref, tri_ref,
                       q_ref, k_ref, kt_ref, vt_ref, lf_ref, cx_ref, xq_ref, carry_ref,
                       *, qscale, tiles_per_seq):
    hn = _rms(x_ref[...], g_ref[...]).astype(BF16)
    qk = _dot(hn, wqk_ref[...])
    q_ref[...] = (qk[:, :FOX_WIDTH] * qscale).astype(BF16)
    k_ref[...] = qk[:, FOX_WIDTH:].astype(BF16)
    kvt = _dot_nt(wkvt_ref[...], hn)
    kt_ref[0] = kvt[:FOX_WIDTH, :]
    vt_ref[0] = kvt[FOX_WIDTH:, :]
    lf = _log_sigmoid(_dot(hn, wf_ref[...]) + bf_ref[...])
    lf_ref[...] = lf

    @pl.when(pl.program_id(0) % tiles_per_seq == 0)
    def _():
        carry_ref[...] = jnp.zeros_like(carry_ref)

    tri = tri_ref[...]
    span = tri.shape[0]
    carry = carry_ref[...]
    pieces = []
    for r0 in range(0, lf.shape[0], span):
        c = carry
        for term in _split3(lf[r0:r0 + span, :]):
            c = c + _dot(tri, term)
        pieces.append(c)
        carry = c[-1:, :]
    carry_ref[...] = carry
    c2 = jnp.concatenate(pieces, axis=0) * LOG2E
    row = lax.broadcasted_iota(jnp.int32, (LANES, LANES), 0)
    col = lax.broadcasted_iota(jnp.int32, (LANES, LANES), 1)
    packed = None
    for i, term in enumerate(_split3(c2)):
        sel = jnp.where((col == row + TERM_STRIDE * i) & (row < TERM_STRIDE), 1.0, 0.0)
        t = _dot(term, sel.astype(BF16))
        packed = t if packed is None else packed + t
    cx_ref[...] = packed.astype(BF16)
    xq_ref[...] = (_dot(hn, wx_ref[...]) * (X_HEAD_DIM ** -0.5)).astype(BF16)


def _fox_proj(x, g, wqkv, wf, bf, wx, qscale):
    t = x.shape[0]
    tm = _row_tile(t, 512)
    row = lambda n: pl.BlockSpec((tm, n), lambda m: (m, 0))
    full = lambda a: pl.BlockSpec(a.shape, lambda m: (0,) * a.ndim)
    return pl.pallas_call(
        functools.partial(_fox_proj_kernel, qscale=qscale),
        grid=(t // tm,),
        in_specs=[row(D_MODEL), full(g), full(wqkv), full(wf), full(bf), full(wx)],
        out_specs=[row(FOX_WIDTH), row(FOX_WIDTH), row(FOX_WIDTH), row(LANES), row(X_WIDTH)],
        out_shape=[
            jax.ShapeDtypeStruct((t, FOX_WIDTH), BF16),
            jax.ShapeDtypeStruct((t, FOX_WIDTH), F32),
            jax.ShapeDtypeStruct((t, FOX_WIDTH), F32),
            jax.ShapeDtypeStruct((t, LANES), F32),
            jax.ShapeDtypeStruct((t, X_WIDTH), BF16),
        ],
        compiler_params=_params("parallel"),
        name="fox_proj",
    )(x, g, wqkv, wf, bf, wx)


def _fox_proj_t(x, g, wqk, wkvt, wf, bf, wx, qscale, b, s):
    t = x.shape[0]
    tm = _row_tile(s, 512)
    ns = s // tm
    span = _row_tile(tm, 256)
    tri = jnp.asarray(np.tril(np.ones((span, span), np.float32)), dtype=BF16)
    row = lambda n: pl.BlockSpec((tm, n), lambda m: (m, 0))
    full = lambda a: pl.BlockSpec(a.shape, lambda m: (0,) * a.ndim)
    tspec = pl.BlockSpec((1, FOX_WIDTH, tm), lambda m: (m // ns, 0, m % ns))
    return pl.pallas_call(
        functools.partial(_fox_proj_t_kernel, qscale=qscale, tiles_per_seq=ns),
        grid=(t // tm,),
        in_specs=[row(D_MODEL), full(g), full(wqk), full(wkvt), full(wf), full(bf), full(wx),
                  full(tri)],
        out_specs=[row(FOX_WIDTH), row(FOX_WIDTH), tspec, tspec, row(LANES), row(LANES),
                   row(X_WIDTH)],
        out_shape=[
            jax.ShapeDtypeStruct((t, FOX_WIDTH), BF16),
            jax.ShapeDtypeStruct((t, FOX_WIDTH), BF16),
            jax.ShapeDtypeStruct((b, FOX_WIDTH, s), F32),
            jax.ShapeDtypeStruct((b, FOX_WIDTH, s), F32),
            jax.ShapeDtypeStruct((t, LANES), F32),
            jax.ShapeDtypeStruct((t, LANES), BF16),
            jax.ShapeDtypeStruct((t, X_WIDTH), BF16),
        ],
        scratch_shapes=[pltpu.VMEM((1, LANES), F32)],
        compiler_params=_params("arbitrary"),
        name="fox_proj_t",
    )(x, g, wqk, wkvt, wf, bf, wx, tri)


def _bias_select(moves, sign):
    r = lax.broadcasted_iota(jnp.int32, (LANES, LANES), 0)
    c = lax.broadcasted_iota(jnp.int32, (LANES, LANES), 1)
    sel = jnp.zeros((LANES, LANES), F32)
    for head, dst in moves:
        for i in range(BIAS_TERMS):
            sel = jnp.where((r == TERM_STRIDE * i + head) & (c == dst + i), sign, sel)
    return sel.astype(BF16)


def _ones_lanes(shape, lo, n):
    lane = lax.broadcasted_iota(jnp.int32, shape, 1)
    return jnp.where((lane >= lo) & (lane < lo + n), 1.0, 0.0)


def _group8(x, op):
    parts = [x[r * 8:(r + 1) * 8, :] for r in range(x.shape[0] // 8)]
    while len(parts) > 1:
        parts = [op(parts[i], parts[i + 1]) for i in range(0, len(parts), 2)]
    return parts[0]


def _flash_kernel(*refs, tile, n_q, n_k, use_c):
    q_refs = refs[:n_q]
    k_refs = refs[n_q:n_q + n_k]
    vt_ref = refs[n_q + n_k]
    pos = n_q + n_k + 1
    if use_c:
        c_ref = refs[pos]
        pos += 1
    o_ref, kb_ref, vtb_ref, s_ref, p_ref, acc_ref = refs[pos:pos + 6]
    pair = pl.program_id(1)
    seq = vt_ref.shape[2]
    n_tiles = seq // tile
    chunk = 64
    nb = BIAS_TERMS

    for n, k_ref in enumerate(k_refs):
        kb_ref[:, n * LANES:(n + 1) * LANES] = k_ref[...].astype(BF16)
    vtb_ref[...] = vt_ref[0].astype(BF16)
    if use_c:
        feat = (_dot(c_ref[0], _bias_select([(2 * pair, 0), (2 * pair + 1, nb)], -1.0))
                + _ones_lanes((1, LANES), 2 * nb, nb))
        kb_ref[:, n_k * LANES:] = feat.astype(BF16)

    kd_q = n_q * LANES
    lane = lax.broadcasted_iota(jnp.int32, (1, kd_q), 1)
    if n_q == 1:
        own = (lane < FOX_HEAD_DIM, lane >= FOX_HEAD_DIM)
    else:
        own = ((lane < MLA_NOPE) | ((lane >= LANES) & (lane < LANES + MLA_ROPE)),
               ((lane >= MLA_NOPE) & (lane < LANES))
               | ((lane >= LANES + MLA_ROPE) & (lane < LANES + 2 * MLA_ROPE)))

    def q_operands(i):
        rows = slice(i * tile, (i + 1) * tile)
        q = (jnp.concatenate([r[rows, :] for r in q_refs], axis=1) if n_q > 1
             else q_refs[0][rows, :])
        zero = jnp.zeros_like(q)
        qh = [jnp.where(own[h], q, zero) for h in range(2)]
        if use_c:
            cx = c_ref[0, rows, :]
            for h in range(2):
                feat = (_dot(cx, _bias_select([(2 * pair + h, 2 * nb)], 1.0))
                        + _ones_lanes((1, LANES), h * nb, nb))
                qh[h] = jnp.concatenate([qh[h], feat.astype(BF16)], axis=1)
        return qh

    krow = lax.broadcasted_iota(jnp.int32, (chunk, tile), 0)
    qcol = lax.broadcasted_iota(jnp.int32, (chunk, tile), 1)

    def scores(slot, qh, j):
        kblk = kb_ref[j * tile:(j + 1) * tile, :]
        for h in range(2):
            s_ref[slot, h] = _dot_nt(kblk, qh[h])

    def values(slot, par, j, alphas):
        hd = LANES // 2
        for h in range(2):
            pv = _dot(vtb_ref[h * hd:(h + 1) * hd, j * tile:(j + 1) * tile], p_ref[slot, h])
            acc_ref[par, h] = pv if alphas is None else acc_ref[par, h] * alphas[h] + pv

    def softmax(slot, stats, diagonal):
        out = []
        alphas = []
        for h in range(2):
            mx = None
            for c in range(tile // chunk):
                rows = slice(c * chunk, (c + 1) * chunk)
                blk = s_ref[slot, h, rows, :]
                if diagonal:
                    blk = jnp.where(krow + c * chunk > qcol, NEG_INF, blk)
                    s_ref[slot, h, rows, :] = blk
                cm = _group8(blk, jnp.maximum)
                mx = cm if mx is None else jnp.maximum(mx, cm)
            m_new = jnp.max(mx, axis=0, keepdims=True)
            if stats is not None:
                m_old = stats[h][0]
                m_new = jnp.maximum(m_old, m_new)
                alphas.append(jnp.exp2(m_old - m_new))
            ls = None
            for c in range(tile // chunk):
                rows = slice(c * chunk, (c + 1) * chunk)
                p = jnp.exp2(s_ref[slot, h, rows, :] - m_new)
                ps = _group8(p, jnp.add)
                ls = ps if ls is None else ls + ps
                p_ref[slot, h, rows, :] = p.astype(BF16)
            l = jnp.sum(ls, axis=0, keepdims=True)
            if stats is not None:
                l = stats[h][1] * alphas[-1] + l
            out.append((m_new, l))
        return out, (None if stats is None else alphas)

    def finish(i, stats):
        par = i % 2
        o_t = jnp.concatenate([acc_ref[par, h] / stats[h][1] for h in range(2)], axis=0)
        o_ref[i * tile:(i + 1) * tile, :] = o_t.T.astype(o_ref.dtype)

    blocks = [(i, j) for i in range(n_tiles) for j in range(i + 1)]
    qh_of = {0: q_operands(0)}
    scores(0, qh_of[0], 0)
    stats = None
    pending = None
    for n, (i, j) in enumerate(blocks):
        if n + 1 < len(blocks):
            i2, j2 = blocks[n + 1]
            if i2 not in qh_of:
                qh_of[i2] = q_operands(i2)
            scores((n + 1) % 2, qh_of[i2], j2)
        if pending is not None:
            slot_p, i_p, j_p, alphas_p, stats_p = pending
            values(slot_p, i_p % 2, j_p, alphas_p)
            if j_p == i_p:
                finish(i_p, stats_p)
        stats, alphas = softmax(n % 2, None if j == 0 else stats, j == i)
        pending = (n % 2, i, j, alphas, stats)
    slot_p, i_p, j_p, alphas_p, stats_p = pending
    values(slot_p, i_p % 2, j_p, alphas_p)
    finish(i_p, stats_p)


FLASH_TILE = 512


def _flash(qs, ks, vt, b, s, c=None):
    tile = _row_tile(s, FLASH_TILE)
    use_c = c is not None
    kd = LANES * (len(ks) + (1 if use_c else 0))
    rows = pl.BlockSpec((s, LANES), lambda i, p: (i, p))
    shared = pl.BlockSpec((s, LANES), lambda i, p: (i, 0))
    in_specs = [rows] * len(qs) + [rows if k.shape[1] > LANES else shared for k in ks]
    in_specs.append(pl.BlockSpec((1, LANES, s), lambda i, p: (i, p, 0)))
    args = list(qs) + list(ks) + [vt]
    if use_c:
        in_specs.append(pl.BlockSpec((1, s, LANES), lambda i, p: (i, 0, 0)))
        args.append(c)
    kern = functools.partial(_flash_kernel, tile=tile, n_q=len(qs), n_k=len(ks), use_c=use_c)
    return pl.pallas_call(
        kern,
        grid=(b, PAIRS),
        in_specs=in_specs,
        out_specs=rows,
        out_shape=jax.ShapeDtypeStruct((b * s, PAIRS * LANES), BF16),
        scratch_shapes=[pltpu.VMEM((s, kd), BF16), pltpu.VMEM((LANES, s), BF16),
                        pltpu.VMEM((2, 2, tile, tile), F32),
                        pltpu.VMEM((2, 2, tile, tile), BF16),
                        pltpu.VMEM((2, 2, LANES // 2, tile), F32)],
        compiler_params=_params("parallel", "parallel"),
        name="flash_mla" if len(qs) > 1 else "flash_fox",
    )(*args)


def _cross_kernel(q_ref, k_ref, v_ref, o_ref):
    q = q_ref[0].astype(BF16)
    kb = k_ref[0].astype(BF16)
    vb = v_ref[0].astype(BF16)
    lane = lax.broadcasted_iota(jnp.int32, (1, LANES), 1)
    first = lane < X_HEAD_DIM
    zero = jnp.zeros_like(q)
    outs = []
    for qh in (jnp.where(first, q, zero), jnp.where(first, zero, q)):
        s = _dot_nt(qh, kb)
        m = jnp.max(s, axis=1, keepdims=True)
        p = jnp.exp(s - m)
        l = jnp.sum(p, axis=1, keepdims=True)
        outs.append(_dot(p.astype(BF16), vb) / l)
    o_ref[0] = jnp.where(first, outs[0], outs[1]).astype(o_ref.dtype)


def _cross(xq, mk, mv):
    b, t, _ = xq.shape
    m = mk.shape[1]
    tq = _row_tile(t, 2048)
    qspec = pl.BlockSpec((1, tq, LANES), lambda i, p, j: (i, j, p))
    kspec = pl.BlockSpec((1, m, LANES), lambda i, p, j: (i, 0, p))
    return pl.pallas_call(
        _cross_kernel,
        grid=(b, X_WIDTH // LANES, t // tq),
        in_specs=[qspec, kspec, kspec],
        out_specs=qspec,
        out_shape=jax.ShapeDtypeStruct((b, t, X_WIDTH), BF16),
        compiler_params=_params("parallel", "parallel", "arbitrary"),
        name="cross_attn",
    )(xq, mk, mv)


def _cross_t_kernel(q_ref, kt_ref, vt_ref, o_ref):
    lane = lax.broadcasted_iota(jnp.int32, (1, LANES), 1)
    first = lane < X_HEAD_DIM
    for g in range(q_ref.shape[0]):
        q = q_ref[g].astype(BF16)
        halves = []
        for p in range(X_WIDTH // LANES):
            cols = slice(p * LANES, (p + 1) * LANES)
            qp = q[:, cols]
            ktb = kt_ref[g, cols, :].astype(BF16)
            vtb = vt_ref[g, cols, :].astype(BF16)
            zero = jnp.zeros_like(qp)
            outs = []
            for qh in (jnp.where(first, qp, zero), jnp.where(first, zero, qp)):
                s = _dot(qh, ktb)
                mx = jnp.max(s, axis=1, keepdims=True)
                pr = jnp.exp(s - mx)
                l = jnp.sum(pr, axis=1, keepdims=True)
                outs.append(_dot_nt(pr.astype(BF16), vtb) / l)
            halves.append(jnp.where(first, outs[0], outs[1]))
        o_ref[g] = jnp.concatenate(halves, axis=1).astype(o_ref.dtype)


CROSS_T_BATCHES = 8


def _cross_t(xq, mkt, mvt, layer):
    b, t, _ = xq.shape
    m = mkt.shape[3]
    g = CROSS_T_BATCHES if b % CROSS_T_BATCHES == 0 else 1
    qspec = pl.BlockSpec((g, t, X_WIDTH), lambda i: (i, 0, 0))
    kspec = pl.BlockSpec((None, g, X_WIDTH, m), lambda i: (layer, i, 0, 0))
    return pl.pallas_call(
        _cross_t_kernel,
        grid=(b // g,),
        in_specs=[qspec, kspec, kspec],
        out_specs=qspec,
        out_shape=jax.ShapeDtypeStruct((b, t, X_WIDTH), BF16),
        compiler_params=_params("parallel"),
        name="cross_attn_t",
    )(xq, mkt, mvt)


def _norm_proj_kernel(x_ref, g_ref, w_ref, o_ref):
    o_ref[...] = _dot(_rms(x_ref[...], g_ref[...]).astype(BF16), w_ref[...])


def _norm_proj(x, g, w):
    t = x.shape[0]
    n = w.shape[1]
    tm = _row_tile(t, 512)
    return pl.pallas_call(
        _norm_proj_kernel,
        grid=(t // tm,),
        in_specs=[pl.BlockSpec((tm, D_MODEL), lambda m: (m, 0)),
                  pl.BlockSpec((1, D_MODEL), lambda m: (0, 0)),
                  pl.BlockSpec(w.shape, lambda m: (0, 0))],
        out_specs=pl.BlockSpec((tm, n), lambda m: (m, 0)),
        out_shape=jax.ShapeDtypeStruct((t, n), F32),
        compiler_params=_params("parallel"),
        name="norm_proj",
    )(x, g, w)


def _mla_proj_kernel(*refs, qscale, with_kv):
    (x_ref, g_ref, wqa_ref, wkva_ref, wkr_ref, wkrr_ref, wx_ref,
     qg_ref, wqn_ref, wqr_ref, wqrr_ref, kvg_ref) = refs[:12]
    pos = 12
    if with_kv:
        wuk_ref, wuvt_ref = refs[pos:pos + 2]
        pos += 2
    cos_ref, sin_ref = refs[pos:pos + 2]
    qn_ref, qr_ref, c_ref, kr_ref, xq_ref = refs[pos + 2:pos + 7]
    hn = _rms(x_ref[...], g_ref[...]).astype(BF16)
    cos = cos_ref[...]
    sin = sin_ref[...]
    qa = _rms(_dot(hn, wqa_ref[...]), qg_ref[...]).astype(BF16)
    qn_ref[...] = (_dot(qa, wqn_ref[...]) * qscale).astype(BF16)
    qr = _dot(qa, wqr_ref[...])
    qrr = _dot(qa, wqrr_ref[...])
    for p in range(PAIRS):
        sl = slice(p * LANES, (p + 1) * LANES)
        qr_ref[:, sl] = ((qr[:, sl] * cos + qrr[:, sl] * sin) * qscale).astype(BF16)
    c = _rms(_dot(hn, wkva_ref[...]), kvg_ref[...])
    c_ref[...] = c
    if with_kv:
        kn_ref, vt_ref = refs[pos + 7:pos + 9]
        cb = c.astype(BF16)
        kn_ref[...] = _dot(cb, wuk_ref[...]).astype(BF16)
        vt_ref[0] = _dot_nt(wuvt_ref[...], cb).astype(BF16)
    kr = _dot(hn, wkr_ref[...]) * cos + _dot(hn, wkrr_ref[...]) * sin
    kr_ref[...] = kr[:, :MLA_ROPE]
    xq_ref[...] = (_dot(hn, wx_ref[...]) * (X_HEAD_DIM ** -0.5)).astype(BF16)


def _mla_proj(x, g, w, cos, sin, qscale, kv_w=None, b=None, s=None):
    t = x.shape[0]
    with_kv = kv_w is not None
    tm = _row_tile(s if with_kv else t, 512)
    ntab = cos.shape[0] // tm
    row = lambda n: pl.BlockSpec((tm, n), lambda m: (m, 0))
    full = lambda a: pl.BlockSpec(a.shape, lambda m: (0,) * a.ndim)
    tab = pl.BlockSpec((tm, LANES), lambda m: (m % ntab, 0))
    wide = PAIRS * LANES
    w = list(w) + (list(kv_w) if with_kv else [])
    out_specs = [row(wide), row(wide), row(MLA_KV_LORA), row(MLA_ROPE), row(X_WIDTH)]
    out_shape = [
        jax.ShapeDtypeStruct((t, wide), BF16),
        jax.ShapeDtypeStruct((t, wide), BF16),
        jax.ShapeDtypeStruct((t, MLA_KV_LORA), F32),
        jax.ShapeDtypeStruct((t, MLA_ROPE), F32),
        jax.ShapeDtypeStruct((t, X_WIDTH), BF16),
    ]
    if with_kv:
        ns = s // tm
        out_specs += [row(wide), pl.BlockSpec((1, wide, tm), lambda m: (m // ns, 0, m % ns))]
        out_shape += [jax.ShapeDtypeStruct((t, wide), BF16),
                      jax.ShapeDtypeStruct((b, wide, s), BF16)]
    return pl.pallas_call(
        functools.partial(_mla_proj_kernel, qscale=qscale, with_kv=with_kv),
        grid=(t // tm,),
        in_specs=[row(D_MODEL), full(g)] + [full(a) for a in w] + [tab, tab],
        out_specs=out_specs,
        out_shape=out_shape,
        compiler_params=_params("parallel"),
        name="mla_proj_kv" if with_kv else "mla_proj",
    )(x, g, *w, cos, sin)


FOX_PAGES_PER_STEP = 16
MLA_PAGES_PER_STEP = 128
PAGE_GROUPS = 4


def _pages_per_step(n_pages, pref):
    while n_pages % pref:
        pref //= 2
    return pref


def _online_step(state, s_parts, v_parts, v_is_transposed):
    m, l, acc = state
    n = len(s_parts)
    per = n // PAGE_GROUPS if n % PAGE_GROUPS == 0 else n
    partials = []
    for g0 in range(0, n, per):
        s = jnp.concatenate(s_parts[g0:g0 + per], axis=1) if per > 1 else s_parts[g0]
        mg = jnp.max(s, axis=1, keepdims=True)
        p = jnp.exp(s - mg)
        lg = jnp.sum(p, axis=1, keepdims=True)
        ag = None
        for r in range(per):
            pb = p[:, r * LANES:(r + 1) * LANES].astype(BF16)
            vb = v_parts[g0 + r]
            t = _dot_nt(pb, vb) if v_is_transposed else _dot(pb, vb)
            ag = t if ag is None else ag + t
        partials.append((mg, lg, ag))
    m_new = m
    for mg, _, _ in partials:
        m_new = jnp.maximum(m_new, mg)
    alpha = jnp.exp(m - m_new)
    l = l * alpha
    acc = acc * alpha
    for mg, lg, ag in partials:
        w = jnp.exp(mg - m_new)
        l = l + lg * w
        acc = acc + ag * w
    return m_new, l, acc


def _head_diag(acc, t_new, width):
    n = acc.shape[1]
    rowh = lax.broadcasted_iota(jnp.int32, (HEAD_PAD, n), 0)
    laneh = lax.broadcasted_iota(jnp.int32, (HEAD_PAD, n), 1) // width
    orow = lax.broadcasted_iota(jnp.int32, (t_new, n), 0)
    out = jnp.zeros((t_new, n), F32)
    for i in range(t_new):
        blk = acc[i * HEAD_PAD:(i + 1) * HEAD_PAD, :]
        r = jnp.sum(jnp.where(rowh == laneh, blk, 0.0), axis=0, keepdims=True)
        out = jnp.where(orow == i, r, out)
    return out


def _fox_dec_kernel(pt_ref, qb_ref, kn_ref, vn_ref, lft_ref, sufm_ref, *rest, pb, t_new):
    kt_refs = rest[:pb]
    vt_refs = rest[pb:2 * pb]
    lf_refs = rest[2 * pb:3 * pb]
    o_ref, m_ref, l_ref, acc_ref, run_ref, pad_ref, lfp_ref = rest[3 * pb:]
    step = pl.program_id(1)
    qb = qb_ref[0]
    lft = lft_ref[0]
    lane = lax.broadcasted_iota(jnp.int32, (HEAD_PAD, LANES), 1)
    cn_cols = []
    run = jnp.zeros((HEAD_PAD, 1), F32)
    for i in range(t_new):
        run = run + lft[:, i:i + 1]
        cn_cols.append(run)
    cn_col = jnp.concatenate(cn_cols, axis=0)

    @pl.when(step == 0)
    def _():
        cn_lanes = jnp.zeros((HEAD_PAD, LANES), F32)
        for i in range(t_new):
            cn_lanes = jnp.where(lane == i, cn_cols[i], cn_lanes)
        pad_ref[...] = jnp.zeros_like(pad_ref)
        pad_ref[0:t_new, :] = kn_ref[0]
        kpad = pad_ref[...].astype(BF16)
        pad_ref[0:t_new, :] = vn_ref[0]
        vpad = pad_ref[...].astype(BF16)
        s = _dot_nt(qb, kpad)
        parts = []
        for i in range(t_new):
            blk = s[i * HEAD_PAD:(i + 1) * HEAD_PAD, :] + (cn_cols[i] - cn_lanes)
            parts.append(jnp.where(lane <= i, blk, NEG_INF))
        s = jnp.concatenate(parts, axis=0)
        m = jnp.max(s, axis=1, keepdims=True)
        p = jnp.exp(s - m)
        m_ref[...] = m
        l_ref[...] = jnp.sum(p, axis=1, keepdims=True)
        acc_ref[...] = _dot(p.astype(BF16), vpad)
        run_ref[...] = jnp.zeros_like(run_ref)
        lfp_ref[...] = jnp.zeros_like(lfp_ref)

    run = run_ref[...]
    s_parts, v_parts = [], []
    for r in range(pb):
        lfp_ref[r * HEAD_PAD:r * HEAD_PAD + FOX_HEADS, :] = lf_refs[r][0]
    lf_all = lfp_ref[...]
    suf_all = _dot_exact(lf_all, sufm_ref[...])
    tot_all = jnp.sum(lf_all, axis=1, keepdims=True)
    for r in range(pb):
        heads = slice(r * HEAD_PAD, (r + 1) * HEAD_PAD)
        suf = suf_all[heads, :] + run
        run = run + tot_all[heads, :]
        ktb = kt_refs[r][0, 0].reshape(FOX_WIDTH, PAGE_SIZE).astype(BF16)
        s = _dot(qb, ktb)
        bias = jnp.concatenate([suf] * t_new, axis=0) + cn_col
        s_parts.append(s + bias)
        v_parts.append(vt_refs[r][0, 0].reshape(FOX_WIDTH, PAGE_SIZE).astype(BF16))
    run_ref[...] = run
    m, l, acc = _online_step((m_ref[...], l_ref[...], acc_ref[...]), s_parts, v_parts, True)
    m_ref[...] = m
    l_ref[...] = l
    acc_ref[...] = acc

    @pl.when(step == pl.num_programs(1) - 1)
    def _():
        o_ref[0] = _head_diag(acc / l, t_new, FOX_HEAD_DIM)


def _fox_decode(page_table, qblk, k_new, v_new, lft_new, cache_kt, cache_vt, cache_lf):
    bd, n_pages = page_table.shape
    t_new = k_new.shape[1]
    rows = t_new * HEAD_PAD
    pb = _pages_per_step(n_pages, FOX_PAGES_PER_STEP)
    sufm = jnp.asarray(np.tril(np.ones((PAGE_SIZE, PAGE_SIZE), np.float32), -1))

    def page(r):
        return lambda b, s, pt: pt[b, n_pages - 1 - (s * pb + r)]

    kv_specs = [pl.BlockSpec((1, 1, FOX_HEADS, FOX_HEAD_DIM, PAGE_SIZE),
                             (lambda b, s, pt, f=page(r): (0, f(b, s, pt), 0, 0, 0)))
                for r in range(pb)]
    lf_specs = [pl.BlockSpec((1, FOX_HEADS, PAGE_SIZE),
                             (lambda b, s, pt, f=page(r): (f(b, s, pt), 0, 0)))
                for r in range(pb)]
    per_b = lambda shape: pl.BlockSpec((1,) + shape, lambda b, s, pt: (b, 0, 0))
    grid_spec = pltpu.PrefetchScalarGridSpec(
        num_scalar_prefetch=1,
        grid=(bd, n_pages // pb),
        in_specs=[per_b((rows, FOX_WIDTH)), per_b((t_new, FOX_WIDTH)), per_b((t_new, FOX_WIDTH)),
                  per_b((HEAD_PAD, LANES)),
                  pl.BlockSpec((PAGE_SIZE, PAGE_SIZE), lambda b, s, pt: (0, 0))]
        + kv_specs + kv_specs + lf_specs,
        out_specs=per_b((t_new, FOX_WIDTH)),
        scratch_shapes=[pltpu.VMEM((rows, 1), F32), pltpu.VMEM((rows, 1), F32),
                        pltpu.VMEM((rows, FOX_WIDTH), F32), pltpu.VMEM((HEAD_PAD, 1), F32),
                        pltpu.VMEM((PAGE_SIZE, FOX_WIDTH), F32),
                        pltpu.VMEM((pb * HEAD_PAD, PAGE_SIZE), F32)],
    )
    return pl.pallas_call(
        functools.partial(_fox_dec_kernel, pb=pb, t_new=t_new),
        grid_spec=grid_spec,
        out_shape=jax.ShapeDtypeStruct((bd, t_new, FOX_WIDTH), F32),
        compiler_params=_params("parallel", "arbitrary"),
        name="fox_decode",
    )(page_table, qblk, k_new, v_new, lft_new, sufm,
      *([cache_kt] * pb), *([cache_vt] * pb), *([cache_lf] * pb))


def _mla_dec_kernel(pt_ref, qn_ref, qr_ref, cn_ref, krn_ref, wuk_ref, wuv_ref, *rest, pb, t_new):
    c_refs = rest[:pb]
    kr_refs = rest[pb:2 * pb]
    o_ref, m_ref, l_ref, acc_ref, ql_ref, cpad_ref = rest[2 * pb:]
    step = pl.program_id(1)
    lane = lax.broadcasted_iota(jnp.int32, (HEAD_PAD, LANES), 1)
    qr = qr_ref[0]

    @pl.when(step == 0)
    def _():
        ql = _dot(qn_ref[0], wuk_ref[...]).astype(BF16)
        ql_ref[...] = ql
        cpad_ref[...] = jnp.zeros_like(cpad_ref)
        cpad_ref[0:t_new, :] = cn_ref[0]
        cpad = cpad_ref[...].astype(BF16)
        s = _dot_nt(ql, cpad) + _dot_nt(qr, krn_ref[0])
        parts = [jnp.where(lane <= i, s[i * HEAD_PAD:(i + 1) * HEAD_PAD, :], NEG_INF)
                 for i in range(t_new)]
        s = jnp.concatenate(parts, axis=0)
        m = jnp.max(s, axis=1, keepdims=True)
        p = jnp.exp(s - m)
        m_ref[...] = m
        l_ref[...] = jnp.sum(p, axis=1, keepdims=True)
        acc_ref[...] = _dot(p.astype(BF16), cpad)

    ql = ql_ref[...]
    s_parts, v_parts = [], []
    kr_pad = jnp.zeros((LANES - MLA_ROPE, PAGE_SIZE), BF16)
    for r in range(pb):
        cb = c_refs[r][0, 0].astype(BF16)
        krt = jnp.concatenate([kr_refs[r][0, 0].astype(BF16), kr_pad], axis=0)
        s_parts.append(_dot_nt(ql, cb) + _dot(qr, krt))
        v_parts.append(cb)
    m, l, acc = _online_step((m_ref[...], l_ref[...], acc_ref[...]), s_parts, v_parts, False)
    m_ref[...] = m
    l_ref[...] = l
    acc_ref[...] = acc

    @pl.when(step == pl.num_programs(1) - 1)
    def _():
        full = _dot((acc / l).astype(BF16), wuv_ref[...])
        o_ref[0] = _head_diag(full, t_new, MLA_V)


def _mla_decode(page_table, qnblk, qrblk, c_new, kr_new, wuk_t, wuv, cache_c, cache_krt):
    bd, n_pages = page_table.shape
    t_new = c_new.shape[1]
    rows = t_new * HEAD_PAD
    pb = _pages_per_step(n_pages, MLA_PAGES_PER_STEP)

    def page(r):
        return lambda b, s, pt: pt[b, s * pb + r]

    c_specs = [pl.BlockSpec((1, 1, PAGE_SIZE, MLA_KV_LORA),
                            (lambda b, s, pt, f=page(r): (0, f(b, s, pt), 0, 0)))
               for r in range(pb)]
    kr_specs = [pl.BlockSpec((1, 1, MLA_ROPE, PAGE_SIZE),
                             (lambda b, s, pt, f=page(r): (0, f(b, s, pt), 0, 0)))
                for r in range(pb)]
    per_b = lambda shape: pl.BlockSpec((1,) + shape, lambda b, s, pt: (b, 0, 0))
    full = lambda a: pl.BlockSpec(a.shape, lambda b, s, pt: (0,) * a.ndim)
    wide = MLA_HEADS * MLA_V
    grid_spec = pltpu.PrefetchScalarGridSpec(
        num_scalar_prefetch=1,
        grid=(bd, n_pages // pb),
        in_specs=[per_b((rows, MLA_HEADS * MLA_NOPE)), per_b((rows, LANES)),
                  per_b((t_new, MLA_KV_LORA)), per_b((PAGE_SIZE, LANES)),
                  full(wuk_t), full(wuv)] + c_specs + kr_specs,
        out_specs=per_b((t_new, wide)),
        scratch_shapes=[pltpu.VMEM((rows, 1), F32), pltpu.VMEM((rows, 1), F32),
                        pltpu.VMEM((rows, MLA_KV_LORA), F32),
                        pltpu.VMEM((rows, MLA_KV_LORA), BF16),
                        pltpu.VMEM((PAGE_SIZE, MLA_KV_LORA), F32)],
    )
    return pl.pallas_call(
        functools.partial(_mla_dec_kernel, pb=pb, t_new=t_new),
        grid_spec=grid_spec,
        out_shape=jax.ShapeDtypeStruct((bd, t_new, wide), F32),
        compiler_params=_params("parallel", "arbitrary"),
        name="mla_decode",
    )(page_table, qnblk, qrblk, c_new, kr_new, wuk_t, wuv,
      *([cache_c] * pb), *([cache_krt] * pb))


def _rope_tables(pos):
    half = MLA_ROPE // 2
    inv = ROPE_THETA ** (-jnp.arange(half, dtype=F32) / half)
    ang = pos.astype(F32)[:, None] * inv
    reps = (LANES // 2) // half
    pad = jnp.zeros((pos.shape[0], LANES // 2), F32)
    cos = jnp.concatenate([jnp.tile(jnp.cos(ang), (1, reps)), pad], axis=1)
    sin = jnp.concatenate([jnp.tile(jnp.sin(ang), (1, reps)), pad], axis=1)
    return cos, sin


def _rot_cols(w):
    half = MLA_ROPE // 2
    return jnp.concatenate([-w[..., half:], w[..., :half]], axis=-1)


def _pair_rope_cols(w):
    r = w.shape[0]
    w = w.reshape(r, PAIRS, 2 * MLA_ROPE)
    return jnp.pad(w, ((0, 0), (0, 0), (0, LANES - 2 * MLA_ROPE))).reshape(r, PAIRS * LANES)


def _block_diag_rows(q, width):
    bd, t, n = q.shape
    rowh = np.arange(t * HEAD_PAD) % HEAD_PAD
    mask = jnp.asarray(rowh[:, None] == (np.arange(n) // width)[None, :])
    rep = jnp.repeat(q, HEAD_PAD, axis=1)
    return jnp.where(mask[None], rep, jnp.zeros_like(rep))


def kernel(x_prompt, x_sample, cache_fox_k, cache_fox_v, cache_fox_logf, cache_mla_ckv,
           cache_mla_krope, cache_mem_k, cache_mem_v, page_table, mem_prompt, norm_gains,
           ffn_w_up, ffn_w_down, fox_w_in, fox_b_f, mla_w_in, mla_q_norm, mla_w_q_b,
           mla_kv_norm, mla_w_kv_b, mem_norm, w_mem_kv, w_out, final_norm):
    b, s, d = x_prompt.shape
    bd, t_new, _ = x_sample.shape
    n_pages = page_table.shape[1]
    mem_len = mem_prompt.shape[1]
    depth = norm_gains.shape[0]
    xp = x_prompt.reshape(b * s, d)
    xs = x_sample.reshape(bd * t_new, d)

    w_up = ffn_w_up.astype(BF16)
    w_down = ffn_w_down.astype(BF16)
    w_out_b = w_out.astype(BF16)
    w_mem_b = w_mem_kv.astype(BF16)
    gains = norm_gains.reshape(depth, 3, 1, d)
    mem2d = mem_prompt.reshape(b * mem_len, d)
    mkt_s = jnp.transpose(cache_mem_k, (0, 1, 3, 4, 2)).reshape(depth, bd, X_WIDTH, mem_len)
    mvt_s = jnp.transpose(cache_mem_v, (0, 1, 3, 4, 2)).reshape(depth, bd, X_WIDTH, mem_len)

    outs = {}
    for layer in range(depth):
        j = layer // 2
        xp = _ffn_half(xp, gains[layer, 0], w_up, w_down, layer, 0)
        xs = _ffn_half(xs, gains[layer, 0], w_up, w_down, layer, 0)

        mkv = _norm_proj(mem2d, mem_norm[layer].reshape(1, d), w_mem_b[layer])
        mk = mkv[:, :X_WIDTH].reshape(b, mem_len, X_WIDTH)
        mv = mkv[:, X_WIDTH:].reshape(b, mem_len, X_WIDTH)
        outs.setdefault("mk", []).append(mk.reshape(b, mem_len, X_HEADS, X_HEAD_DIM))
        outs.setdefault("mv", []).append(mv.reshape(b, mem_len, X_HEADS, X_HEAD_DIM))

        if layer % 2 == 0:
            w_in = fox_w_in[j]
            wqkv = w_in[:, :3 * FOX_WIDTH].astype(BF16)
            wf = jnp.pad(w_in[:, 3 * FOX_WIDTH:3 * FOX_WIDTH + FOX_HEADS],
                         ((0, 0), (0, LANES - FOX_HEADS))).astype(BF16)
            bf = jnp.pad(fox_b_f[j], (0, LANES - FOX_HEADS)).reshape(1, LANES)
            wx = w_in[:, 3 * FOX_WIDTH + FOX_HEADS:].astype(BF16)
            g1 = gains[layer, 1]

            wqk = w_in[:, :2 * FOX_WIDTH].astype(BF16)
            wkvt = jnp.transpose(w_in[:, FOX_WIDTH:3 * FOX_WIDTH]).astype(BF16)
            q, kb, kt, vt, lf, cx, xq_p = _fox_proj_t(
                xp, g1, wqk, wkvt, wf, bf, wx, FOX_HEAD_DIM ** -0.5 * LOG2E, b, s)
            mix_p = _flash([q], [kb], vt, b, s, cx.reshape(b, s, LANES))
            heads_t = (0, 1, 4, 2, 3)
            outs["fk_p"] = jnp.transpose(kt.reshape(1, b, FOX_HEADS, FOX_HEAD_DIM, s), heads_t)
            outs["fv_p"] = jnp.transpose(vt.reshape(1, b, FOX_HEADS, FOX_HEAD_DIM, s), heads_t)
            outs["fl_p"] = lf[:, :FOX_HEADS].reshape(1, b, s, FOX_HEADS)

            q, k, v, lf, xq_s = _fox_proj(xs, g1, wqkv, wf, bf, wx, FOX_HEAD_DIM ** -0.5)
            outs["fk_s"] = k.reshape(1, bd, t_new, FOX_HEADS, FOX_HEAD_DIM)
            outs["fv_s"] = v.reshape(1, bd, t_new, FOX_HEADS, FOX_HEAD_DIM)
            outs["fl_s"] = lf[:, :FOX_HEADS].reshape(1, bd, t_new, FOX_HEADS)
            qblk = _block_diag_rows(q.reshape(bd, t_new, FOX_WIDTH), FOX_HEAD_DIM)
            lft = jnp.transpose(lf.reshape(bd, t_new, LANES)[:, :, :HEAD_PAD], (0, 2, 1))
            lft = jnp.pad(lft, ((0, 0), (0, 0), (0, LANES - t_new)))
            cache_kt = jnp.transpose(cache_fox_k, (0, 1, 3, 4, 2))
            cache_vt = jnp.transpose(cache_fox_v, (0, 1, 3, 4, 2))
            cache_lf = jnp.transpose(cache_fox_logf[j], (0, 2, 1))
            mix_s = _fox_decode(page_table, qblk, k.reshape(bd, t_new, FOX_WIDTH),
                                v.reshape(bd, t_new, FOX_WIDTH), lft,
                                cache_kt[j:j + 1], cache_vt[j:j + 1], cache_lf)
            mix_s = mix_s.reshape(bd * t_new, FOX_WIDTH).astype(BF16)
        else:
            w_in = mla_w_in[j]
            o1 = MLA_Q_LORA
            o2 = o1 + MLA_KV_LORA
            o3 = o2 + MLA_ROPE
            wkr = w_in[:, o2:o3]
            lane_pad = ((0, 0), (0, LANES - MLA_ROPE))
            wqb = mla_w_q_b[j]
            wkvb = mla_w_kv_b[j]
            weights = [
                w_in[:, :o1].astype(BF16),
                w_in[:, o1:o2].astype(BF16),
                jnp.pad(wkr, lane_pad).astype(BF16),
                jnp.pad(_rot_cols(wkr), lane_pad).astype(BF16),
                w_in[:, o3:].astype(BF16),
                mla_q_norm[j].reshape(1, MLA_Q_LORA),
                wqb[:, :, :MLA_NOPE].reshape(MLA_Q_LORA, MLA_HEADS * MLA_NOPE).astype(BF16),
                _pair_rope_cols(wqb[:, :, MLA_NOPE:]).astype(BF16),
                _pair_rope_cols(_rot_cols(wqb[:, :, MLA_NOPE:])).astype(BF16),
                mla_kv_norm[j].reshape(1, MLA_KV_LORA),
            ]
            wuk = wkvb[:, :, :MLA_NOPE].reshape(MLA_KV_LORA, MLA_HEADS * MLA_NOPE).astype(BF16)
            wuv = wkvb[:, :, MLA_NOPE:].reshape(MLA_KV_LORA, MLA_HEADS * MLA_V).astype(BF16)
            wuv_t = jnp.transpose(wkvb[:, :, MLA_NOPE:], (1, 2, 0)).reshape(
                MLA_HEADS * MLA_V, MLA_KV_LORA).astype(BF16)
            mla_scale = (MLA_NOPE + MLA_ROPE) ** -0.5
            g1 = gains[layer, 1]
            cos_p, sin_p = _rope_tables(jnp.arange(s, dtype=jnp.int32))
            pos_s = n_pages * PAGE_SIZE + jnp.arange(t_new, dtype=jnp.int32)
            cos_s, sin_s = _rope_tables(jnp.tile(pos_s, bd))

            qn, qr, c, kr, xq_p, kn, vt = _mla_proj(
                xp, g1, weights, cos_p, sin_p, mla_scale * LOG2E, (wuk, wuv_t), b, s)
            kr2 = jnp.concatenate(
                [kr, kr, jnp.zeros((b * s, LANES - 2 * MLA_ROPE), F32)], axis=1).astype(BF16)
            mix_p = _flash([qn, qr], [kn, kr2], vt, b, s)
            outs["mc_p"] = c.reshape(1, b, s, MLA_KV_LORA)
            outs["mr_p"] = kr.reshape(1, b, s, MLA_ROPE)

            qn, qr, c, kr, xq_s = _mla_proj(xs, g1, weights, cos_s, sin_s, mla_scale)
            outs["mc_s"] = c.reshape(1, bd, t_new, MLA_KV_LORA)
            outs["mr_s"] = kr.reshape(1, bd, t_new, MLA_ROPE)
            qnblk = _block_diag_rows(qn.reshape(bd, t_new, MLA_HEADS * MLA_NOPE), MLA_NOPE)
            qr4 = qr.reshape(bd, t_new, PAIRS, LANES)[..., :2 * MLA_ROPE]
            qr4 = qr4.reshape(bd, t_new, MLA_HEADS, MLA_ROPE)
            qr4 = jnp.pad(qr4, ((0, 0), (0, 0), (0, HEAD_PAD - MLA_HEADS), (0, LANES - MLA_ROPE)))
            qrblk = qr4.reshape(bd, t_new * HEAD_PAD, LANES)
            kr_new = jnp.pad(kr.reshape(bd, t_new, MLA_ROPE).astype(BF16),
                             ((0, 0), (0, PAGE_SIZE - t_new), (0, LANES - MLA_ROPE)))
            wuk_t = jnp.transpose(wkvb[:, :, :MLA_NOPE], (1, 2, 0)).reshape(
                MLA_HEADS * MLA_NOPE, MLA_KV_LORA).astype(BF16)
            cache_krt = jnp.transpose(cache_mla_krope, (0, 1, 3, 2))
            mix_s = _mla_decode(page_table, qnblk, qrblk, c.reshape(bd, t_new, MLA_KV_LORA),
                                kr_new, wuk_t, wuv, cache_mla_ckv[j:j + 1],
                                cache_krt[j:j + 1])
            mix_s = mix_s.reshape(bd * t_new, MLA_HEADS * MLA_V).astype(BF16)

        cross_p = _cross(xq_p.reshape(b, s, X_WIDTH), mk, mv).reshape(b * s, X_WIDTH)
        cross_s = _cross_t(xq_s.reshape(bd, t_new, X_WIDTH), mkt_s, mvt_s, layer).reshape(
            bd * t_new, X_WIDTH)
        fin = final_norm.reshape(1, d) if layer == depth - 1 else None
        xp = _ffn_half(xp, gains[layer, 2], w_up, w_down, layer, 1,
                       (mix_p, cross_p, w_out_b[layer]), fin)
        xs = _ffn_half(xs, gains[layer, 2], w_up, w_down, layer, 1,
                       (mix_s, cross_s, w_out_b[layer]), fin)

    y_prompt = xp.reshape(b, s, d)
    y_sample = xs.reshape(bd, t_new, d)
    return (y_prompt, y_sample,
            outs["fk_p"], outs["fv_p"], outs["fl_p"],
            outs["fk_s"], outs["fv_s"], outs["fl_s"],
            outs["mc_p"], outs["mr_p"], outs["mc_s"], outs["mr_s"],
            jnp.stack(outs["mk"]), jnp.stack(outs["mv"]))
```
